```python
import math
import jax, jax.numpy as jnp
from jax import lax
import numpy as np

D_MODEL = 2048
BATCH = 4
SEQ = 4096
DEPTH = 2

N_META = 16
CONV_W = D_MODEL
CONV_K = 3
D_INNER = 2 * D_MODEL
SSD_HEADDIM = 64
SSD_HEADS = D_INNER // SSD_HEADDIM
SSD_GROUPS = 8
SSD_STATE = 128
SSD_CONV_K = 4
CHUNK = 256
SSD_XBC = D_INNER + 2 * SSD_GROUPS * SSD_STATE
FF_DENSE = 256 * (-(-8 * D_MODEL // (3 * 256)))
N_EXPERTS = 8
TOP_K = 2
FF_EXPERT = 7 * D_MODEL // 2
N_DENSE = (DEPTH + 1) // 2
N_MOE = DEPTH // 2
ALPHA = (2.0 * DEPTH) ** 0.25
BETA = (8.0 * DEPTH) ** -0.25
LN_EPS = 1e-5
RMS_EPS = 1e-5
N_IN = 3 * CONV_W + D_INNER + SSD_XBC + SSD_HEADS + 2 * D_MODEL
SPLITS = [CONV_W, 2 * CONV_W, 3 * CONV_W,
          3 * CONV_W + D_INNER,
          3 * CONV_W + D_INNER + SSD_XBC,
          3 * CONV_W + D_INNER + SSD_XBC + SSD_HEADS]

kernel_name = "hybrid_gated_shortconv_ssd_moe_deepnorm"


def layer_norm(x, g, b):
    xf = x.astype(jnp.float32)
    mu = xf.mean(-1, keepdims=True)
    var = jnp.square(xf - mu).mean(-1, keepdims=True)
    return ((xf - mu) * lax.rsqrt(var + LN_EPS) * g + b).astype(x.dtype)


def causal_dwconv(x, w):
    K = w.shape[0]
    L = x.shape[1]
    xp = jnp.pad(x, ((0, 0), (K - 1, 0), (0, 0)))
    return sum(w[k] * xp[:, k:k + L] for k in range(K))


def ssd_chunked_scan(xdt, da, bmat, cmat):
    b, Lp, g, r, p = xdt.shape
    nc = Lp // CHUNK

    def to_chunks(t):
        return jnp.moveaxis(t.reshape((b, nc, CHUNK) + t.shape[2:]), 1, 0)

    causal = jnp.tril(jnp.ones((CHUNK, CHUNK), dtype=bool))[None, :, :, None, None]

    def step(state, inp):
        xc, dac, bc, cc = inp
        cum = jnp.cumsum(dac, axis=1)
        seg = cum[:, :, None] - cum[:, None, :]
        decay = jnp.exp(jnp.where(causal, seg, -jnp.inf))
        cb = jnp.einsum("btgn,bsgn->btsg", cc, bc)
        y_diag = jnp.einsum("btsg,btsgr,bsgrp->btgrp", cb, decay, xc)
        y_off = jnp.einsum("btgn,bgrpn,btgr->btgrp", cc, state, jnp.exp(cum))
        last = cum[:, -1]
        to_end = jnp.exp(last[:, None] - cum)
        new_state = state * jnp.exp(last)[..., None, None] + jnp.einsum(
            "bsgn,bsgr,bsgrp->bgrpn", bc, to_end, xc)
        return new_state, y_diag + y_off

    state0 = jnp.zeros((b, g, r, p, bmat.shape[-1]), jnp.float32)
    _, ys = lax.scan(step, state0, (to_chunks(xdt), to_chunks(da), to_chunks(bmat), to_chunks(cmat)))
    return jnp.moveaxis(ys, 0, 1).reshape(b, Lp, g, r, p)


def token_mixer(h, w_in, b_gate, conv_a_w, w_out_a, ssd_conv_w, ssd_conv_b,
                dt_bias, a_log, d_skip, ssd_norm_w, w_out_b, w_o):
    bsz, L, _ = h.shape
    proj = h @ w_in
    c_pre, b_post, v, z, xbc, dt_raw, gates = jnp.split(proj, SPLITS, axis=-1)

    y_a = (b_post * causal_dwconv(c_pre * v, conv_a_w)) @ w_out_a

    xbc = jax.nn.silu(causal_dwconv(xbc, ssd_conv_w) + ssd_conv_b)
    xs, bm, cm = jnp.split(xbc, [D_INNER, D_INNER + SSD_GROUPS * SSD_STATE], axis=-1)
    dt = jax.nn.softplus(dt_raw.astype(jnp.float32) + dt_bias.astype(jnp.float32))
    a = -jnp.exp(a_log.astype(jnp.float32))
    r = SSD_HEADS // SSD_GROUPS
    xh = xs.astype(jnp.float32).reshape(bsz, L, SSD_GROUPS, r, SSD_HEADDIM)
    dtg = dt.reshape(bsz, L, SSD_GROUPS, r)
    xdt = xh * dtg[..., None]
    da = dtg * a.reshape(SSD_GROUPS, r)
    bm = bm.astype(jnp.float32).reshape(bsz, L, SSD_GROUPS, SSD_STATE)
    cm = cm.astype(jnp.float32).reshape(bsz, L, SSD_GROUPS, SSD_STATE)
    pad_front = CHUNK - N_META
    pad_back = (-(L - N_META)) % CHUNK

    def pad_t(t):
        return jnp.pad(t, ((0, 0), (pad_front, pad_back)) + ((0, 0),) * (t.ndim - 2))

    y = ssd_chunked_scan(pad_t(xdt), pad_t(da), pad_t(bm), pad_t(cm))[:, pad_front:pad_front + L]
    y = y + xh * d_skip.astype(jnp.float32).reshape(SSD_GROUPS, r)[..., None]
    y = y.reshape(bsz, L, D_INNER) * jax.nn.silu(z.astype(jnp.float32))
    yg = y.reshape(bsz, L, SSD_GROUPS, D_INNER // SSD_GROUPS)
    yg = yg * lax.rsqrt(jnp.mean(jnp.square(yg), -1, keepdims=True) + RMS_EPS)
    y = (yg.reshape(bsz, L, D_INNER) * ssd_norm_w).astype(h.dtype)
    y_b = y @ w_out_b

    g = jax.nn.sigmoid(gates + b_gate)
    g_a, g_b = jnp.split(g, 2, axis=-1)
    return (g_a * y_a + g_b * y_b) @ w_o


def swiglu(h, w_gate, w_up, w_down):
    return (jax.nn.silu(h @ w_gate) * (h @ w_up)) @ w_down


def moe_ffn(h, router, w_gate, w_up, w_down):
    t = h.reshape(-1, D_MODEL)
    logits = (t @ router).astype(jnp.float32)
    top_v, top_i = lax.top_k(logits, TOP_K)
    wts = jax.nn.softmax(top_v, axis=-1)
    combine = jnp.einsum("tk,tke->te", wts,
                         jax.nn.one_hot(top_i, N_EXPERTS, dtype=jnp.float32)).astype(h.dtype)
    out = jnp.zeros_like(t)
    for e in range(N_EXPERTS):
        out = out + combine[:, e:e + 1] * swiglu(t, w_gate[e], w_up[e], w_down[e])
    return out.reshape(h.shape)


def setup_inputs(seed: int = 0) -> dict:
    key = jax.random.key(seed)
    ks = jax.random.split(key, 32)
    f32 = jnp.float32
    nrm = lambda k, shape, scale: jax.random.normal(k, shape, f32) * scale
    dt0 = jnp.exp(jax.random.uniform(ks[9], (DEPTH, SSD_HEADS), f32, math.log(1e-3), math.log(1e-1)))
    return {
        "x": nrm(ks[0], (BATCH, SEQ, D_MODEL), 1.0),
        "meta_tokens": nrm(ks[1], (N_META, D_MODEL), 1.0),
        "ln_in_g": 1.0 + nrm(ks[2], (D_MODEL,), 0.02),
        "ln_in_b": nrm(ks[3], (D_MODEL,), 0.02),
        "w_in": nrm(ks[4], (DEPTH, D_MODEL, N_IN), D_MODEL ** -0.5),
        "b_gate": nrm(ks[5], (DEPTH, 2 * D_MODEL), 0.02),
        "conv_a_w": nrm(ks[6], (DEPTH, CONV_K, CONV_W), CONV_K ** -0.5),
        "w_out_a": nrm(ks[7], (DEPTH, CONV_W, D_MODEL), BETA * CONV_W ** -0.5),
        "ssd_conv_w": nrm(ks[8], (DEPTH, SSD_CONV_K, SSD_XBC), SSD_CONV_K ** -0.5),
        "ssd_conv_b": nrm(ks[10], (DEPTH, SSD_XBC), 0.02),
        "dt_bias": dt0 + jnp.log(-jnp.expm1(-dt0)),
        "a_log": jnp.log(jax.random.uniform(ks[11], (DEPTH, SSD_HEADS), f32, 1.0, 16.0)),
        "d_skip": 1.0 + nrm(ks[12], (DEPTH, SSD_HEADS), 0.1),
        "ssd_norm_w": 1.0 + nrm(ks[13], (DEPTH, D_INNER), 0.02),
        "w_out_b": nrm(ks[14], (DEPTH, D_INNER, D_MODEL), BETA * D_INNER ** -0.5),
        "w_o": nrm(ks[15], (DEPTH, D_MODEL, D_MODEL), BETA * D_MODEL ** -0.5),
        "ln1_g": 1.0 + nrm(ks[16], (DEPTH, D_MODEL), 0.02),
        "ln1_b": nrm(ks[17], (DEPTH, D_MODEL), 0.02),
        "ffn_w_gate": nrm(ks[18], (N_DENSE, D_MODEL, FF_DENSE), D_MODEL ** -0.5),
        "ffn_w_up": nrm(ks[19], (N_DENSE, D_MODEL, FF_DENSE), D_MODEL ** -0.5),
        "ffn_w_down": nrm(ks[20], (N_DENSE, FF_DENSE, D_MODEL), BETA * FF_DENSE ** -0.5),
        "router": nrm(ks[21], (N_MOE, D_MODEL, N_EXPERTS), D_MODEL ** -0.5),
        "moe_w_gate": nrm(ks[22], (N_MOE, N_EXPERTS, D_MODEL, FF_EXPERT), D_MODEL ** -0.5),
        "moe_w_up": nrm(ks[23], (N_MOE, N_EXPERTS, D_MODEL, FF_EXPERT), D_MODEL ** -0.5),
        "moe_w_down": nrm(ks[24], (N_MOE, N_EXPERTS, FF_EXPERT, D_MODEL), BETA * FF_EXPERT ** -0.5),
        "ln2_g": 1.0 + nrm(ks[25], (DEPTH, D_MODEL), 0.02),
        "ln2_b": nrm(ks[26], (DEPTH, D_MODEL), 0.02),
    }


def reference(x, meta_tokens, ln_in_g, ln_in_b, w_in, b_gate, conv_a_w, w_out_a,
              ssd_conv_w, ssd_conv_b, dt_bias, a_log, d_skip, ssd_norm_w, w_out_b, w_o,
              ln1_g, ln1_b, ffn_w_gate, ffn_w_up, ffn_w_down, router, moe_w_gate,
              moe_w_up, moe_w_down, ln2_g, ln2_b):
    bsz = x.shape[0]
    meta = jnp.broadcast_to(meta_tokens.astype(x.dtype)[None], (bsz, N_META, D_MODEL))
    h = layer_norm(jnp.concatenate([meta, x], axis=1), ln_in_g, ln_in_b)
    for i in range(DEPTH):
        mix = token_mixer(h, w_in[i], b_gate[i], conv_a_w[i], w_out_a[i], ssd_conv_w[i],
                          ssd_conv_b[i], dt_bias[i], a_log[i], d_skip[i], ssd_norm_w[i],
                          w_out_b[i], w_o[i])
        h = layer_norm(ALPHA * h + mix, ln1_g[i], ln1_b[i])
        j = i // 2
        if i % 2 == 0:
            f = swiglu(h, ffn_w_gate[j], ffn_w_up[j], ffn_w_down[j])
        else:
            f = moe_ffn(h, router[j], moe_w_gate[j], moe_w_up[j], moe_w_down[j])
        h = layer_norm(ALPHA * h + f, ln2_g[i], ln2_b[i])
    return h[:, N_META:]
```

```python
import functools

import jax
import jax.numpy as jnp
from jax import lax
from jax.experimental import pallas as pl
from jax.experimental.pallas import tpu as pltpu

F32 = jnp.float32
BF16 = jnp.bfloat16

D = 2048
BATCH = 4
SEQ = 4096
N_META = 16
DEPTH = 2
D_INNER = 2 * D
HEADDIM = 64
HEADS = D_INNER // HEADDIM
GROUPS = 8
HPG = HEADS // GROUPS
STATE = 128
GW = D_INNER // GROUPS
XBC = D_INNER + 2 * GROUPS * STATE
FF_DENSE = 5632
N_EXPERTS = 8
FF_EXPERT = 7168
ALPHA = (2.0 * DEPTH) ** 0.25
LN_EPS = 1e-5
RMS_EPS = 1e-5

CH = 256
PADF = CH - N_META
LP = PADF + N_META + SEQ
NCH = LP // CH
TP = BATCH * LP

C0, B0, V0 = 0, D, 2 * D
Z0 = 3 * D
X0 = Z0 + D_INNER
G0 = X0 + XBC
NP = G0 + 2 * D

LANES = 128
BF16_SUBLANES = 16

TM_MOE = 512
N_MOE_TILES = (2 * BATCH * (SEQ + N_META)) // TM_MOE + N_EXPERTS
P_MOE = N_MOE_TILES * TM_MOE
TF_MOE = 1024
TM_FFN = 512
TF_FFN = 512
TOK_TILE = CH

MIB = 1024 * 1024


def _cp(sem, vmem_mib):
    return pltpu.CompilerParams(dimension_semantics=sem, vmem_limit_bytes=int(vmem_mib * MIB))


def _sigmoid(x):
    return 1.0 / (1.0 + jnp.exp(-x))


def _silu(x):
    return x * _sigmoid(x)


def _softplus(x):
    return jnp.maximum(x, 0.0) + jnp.log1p(jnp.exp(-jnp.abs(x)))


def _layer_norm(x, g, b):
    mu = jnp.mean(x, axis=-1, keepdims=True)
    xc = x - mu
    var = jnp.mean(xc * xc, axis=-1, keepdims=True)
    return xc * lax.rsqrt(var + LN_EPS) * g + b


def _valid_rows(tile_idx, tm):
    row = lax.broadcasted_iota(jnp.int32, (tm, 1), 0)
    valid = None
    for s in range(tm // CH):
        chunk = tile_idx * (tm // CH) + s
        hi = jnp.where(chunk % NCH == 0, s * CH + PADF, s * CH)
        ok = jnp.logical_or(row < s * CH, row >= hi)
        valid = ok if valid is None else jnp.logical_and(valid, ok)
    return valid


def _ln_in_kernel(x_ref, meta_ref, g_ref, b_ref, hf_ref, hb_ref):
    j = pl.program_id(1)

    @pl.when(j == 0)
    def _():
        hf_ref[...] = jnp.zeros_like(hf_ref)
        hb_ref[...] = jnp.zeros_like(hb_ref)
        y = _layer_norm(meta_ref[...], g_ref[...], b_ref[...])
        hf_ref[PADF:, :] = y
        hb_ref[PADF:, :] = y.astype(BF16)

    @pl.when(j > 0)
    def _():
        y = _layer_norm(x_ref[0], g_ref[...], b_ref[...])
        hf_ref[...] = y
        hb_ref[...] = y.astype(BF16)


def _ln_in(x, meta, g, b):
    return pl.pallas_call(
        _ln_in_kernel,
        grid=(BATCH, NCH),
        in_specs=[
            pl.BlockSpec((1, CH, D), lambda bi, j: (bi, jnp.maximum(j - 1, 0), 0)),
            pl.BlockSpec((N_META, D), lambda bi, j: (0, 0)),
            pl.BlockSpec((1, D), lambda bi, j: (0, 0)),
            pl.BlockSpec((1, D), lambda bi, j: (0, 0)),
        ],
        out_specs=[
            pl.BlockSpec((CH, D), lambda bi, j: (bi * NCH + j, 0)),
            pl.BlockSpec((CH, D), lambda bi, j: (bi * NCH + j, 0)),
        ],
        out_shape=[jax.ShapeDtypeStruct((TP, D), F32), jax.ShapeDtypeStruct((TP, D), BF16)],
        compiler_params=_cp(("arbitrary", "arbitrary"), 32),
        name="ln_in",
    )(x, meta, g, b)


def _mm_kernel(a_ref, w_ref, o_ref):
    o_ref[...] = jnp.dot(a_ref[...], w_ref[...], preferred_element_type=F32).astype(o_ref.dtype)


def _in_proj(h_bf, w_main):
    tm, tn = 1024, 2048
    return pl.pallas_call(
        _mm_kernel,
        grid=(NP // tn, TP // tm),
        in_specs=[
            pl.BlockSpec((tm, D), lambda j, i: (i, 0)),
            pl.BlockSpec((D, tn), lambda j, i: (0, j)),
        ],
        out_specs=pl.BlockSpec((tm, tn), lambda j, i: (i, j)),
        out_shape=jax.ShapeDtypeStruct((TP, NP), BF16),
        compiler_params=_cp(("arbitrary", "arbitrary"), 48),
        name="in_proj",
    )(h_bf, w_main)


def _dt_kernel(a_ref, w_ref, bias_ref, alog_ref, dt_ref, cum_ref):
    i = pl.program_id(0)
    raw = jnp.dot(a_ref[...], w_ref[...], preferred_element_type=F32) + bias_ref[...]
    dt = _softplus(raw)
    a_neg = -jnp.exp(alog_ref[...])
    row = lax.broadcasted_iota(jnp.int32, (CH, 1), 0)
    first = jnp.where(i % NCH == 0, PADF, 0)
    da = jnp.where(row >= first, dt * a_neg, 0.0)
    r = lax.broadcasted_iota(jnp.int32, (CH, CH), 0)
    c = lax.broadcasted_iota(jnp.int32, (CH, CH), 1)
    tril = (r >= c).astype(F32)
    cum = jnp.dot(tril, da, precision=lax.Precision.HIGHEST, preferred_element_type=F32)
    dt_ref[...] = dt
    cum_ref[...] = cum


def _dt_proj(h_bf, w_dt, dt_bias, a_log):
    return pl.pallas_call(
        _dt_kernel,
        grid=(TP // CH,),
        in_specs=[
            pl.BlockSpec((CH, D), lambda i: (i, 0)),
            pl.BlockSpec((D, LANES), lambda i: (0, 0)),
            pl.BlockSpec((1, LANES), lambda i: (0, 0)),
            pl.BlockSpec((1, LANES), lambda i: (0, 0)),
        ],
        out_specs=[
            pl.BlockSpec((CH, LANES), lambda i: (i, 0)),
            pl.BlockSpec((CH, LANES), lambda i: (i, 0)),
        ],
        out_shape=[jax.ShapeDtypeStruct((TP, LANES), F32), jax.ShapeDtypeStruct((TP, LANES), F32)],
        compiler_params=_cp(("arbitrary",), 16),
        name="dt_proj",
    )(h_bf, w_dt, dt_bias, a_log)


def _causal_conv(x, halo, w, k):
    rows = x.shape[0]
    row = lax.broadcasted_iota(jnp.int32, (rows, 1), 0)
    acc = w[k - 1:k, :] * x
    for s in range(1, k):
        shifted = pltpu.roll(x, s, 0)
        for j in range(s):
            src = BF16_SUBLANES - s + j
            shifted = jnp.where(row == j, halo[src:src + 1, :], shifted)
        acc = acc + w[k - 1 - s:k - s, :] * shifted
    return acc


def _mixa_kernel(c_ref, b_ref, v_ref, ch_ref, vh_ref, g_ref, cw_ref, wo_ref, bg_ref, o_ref):
    u = c_ref[...].astype(F32) * v_ref[...].astype(F32)
    hu = ch_ref[...].astype(F32) * vh_ref[...].astype(F32)
    conv = _causal_conv(u, hu, cw_ref[...], 3)
    lhs = (b_ref[...].astype(F32) * conv).astype(BF16)
    ya = jnp.dot(lhs, wo_ref[...], preferred_element_type=F32)
    ga = _sigmoid(g_ref[...].astype(F32) + bg_ref[...])
    o_ref[...] = (ga * ya).astype(BF16)


def _mixer_a(proj, conv_w, w_out_a, bgate_a):
    tm = 512
    hb = tm // BF16_SUBLANES
    halo = lambda col: (lambda i: (jnp.maximum(i * hb - 1, 0), col))
    return pl.pallas_call(
        _mixa_kernel,
        grid=(TP // tm,),
        in_specs=[
            pl.BlockSpec((tm, D), lambda i: (i, C0 // D)),
            pl.BlockSpec((tm, D), lambda i: (i, B0 // D)),
            pl.BlockSpec((tm, D), lambda i: (i, V0 // D)),
            pl.BlockSpec((BF16_SUBLANES, D), halo(C0 // D)),
            pl.BlockSpec((BF16_SUBLANES, D), halo(V0 // D)),
            pl.BlockSpec((tm, D), lambda i: (i, G0 // D)),
            pl.BlockSpec((3, D), lambda i: (0, 0)),
            pl.BlockSpec((D, D), lambda i: (0, 0), pipeline_mode=pl.Buffered(1)),
            pl.BlockSpec((1, D), lambda i: (0, 0)),
        ],
        out_specs=pl.BlockSpec((tm, D), lambda i: (i, 0)),
        out_shape=jax.ShapeDtypeStruct((TP, D), BF16),
        compiler_params=_cp(("arbitrary",), 56),
        name="mixer_a",
    )(proj, proj, proj, proj, proj, proj, conv_w, w_out_a, bgate_a)


def _ssd_kernel(xs_ref, xh_ref, bm_ref, bh_ref, cm_ref, chh_ref, z_ref, dt_ref, cum_ref,
                wx_ref, wb_ref, wc_ref, bx_ref, bb_ref, bc_ref, dsk_ref, nw_ref,
                o_ref, st_ref):
    g = pl.program_id(1)
    c = pl.program_id(2)

    @pl.when(c == 0)
    def _():
        st_ref[...] = jnp.zeros_like(st_ref)

    row = lax.broadcasted_iota(jnp.int32, (CH, 1), 0)
    first = jnp.where(c == 0, PADF, 0)

    def conv_silu(x_ref, h_ref, w_ref, b_ref):
        y = _causal_conv(x_ref[...].astype(F32), h_ref[...].astype(F32), w_ref[...], 4)
        return _silu(y + b_ref[...])

    xs = conv_silu(xs_ref, xh_ref, wx_ref, bx_ref)
    xs = jnp.where(row >= first, xs, 0.0)
    bm = conv_silu(bm_ref, bh_ref, wb_ref, bb_ref)
    cm = conv_silu(cm_ref, chh_ref, wc_ref, bc_ref)

    shift = (LANES - HPG * g) % LANES
    dtg = pltpu.roll(dt_ref[...], shift, 1)
    cumg = pltpu.roll(cum_ref[...], shift, 1)
    cum_t = cumg.T
    dt_t = dtg.T
    last = cumg[CH - 1:CH, :]
    ecum = jnp.exp(cumg)
    wts = jnp.exp(last - cumg) * dtg
    elast = jnp.exp(last)

    bm_bf = bm.astype(BF16)
    cm_bf = cm.astype(BF16)
    xs_bf = xs.astype(BF16)
    cb = lax.dot_general(cm_bf, bm_bf, (((1,), (1,)), ((), ())), preferred_element_type=F32)
    r2 = lax.broadcasted_iota(jnp.int32, (CH, CH), 0)
    c2 = lax.broadcasted_iota(jnp.int32, (CH, CH), 1)
    cbm = jnp.where(r2 >= c2, cb, 0.0)
    st = st_ref[...]
    y_off = jnp.dot(cm_bf, st.astype(BF16), preferred_element_type=F32)

    lane = lax.broadcasted_iota(jnp.int32, (1, LANES), 1)
    lo = lane < HEADDIM

    def head_mat(j):
        seg = cumg[:, j:j + 1] - cum_t[j:j + 1, :]
        return (cbm * jnp.exp(jnp.minimum(seg, 0.0)) * dt_t[j:j + 1, :]).astype(BF16)

    ys, xws, els = [], [], []
    for p in range(GW // LANES):
        ja, jb = 2 * p, 2 * p + 1
        sl = slice(p * LANES, (p + 1) * LANES)
        xp = xs_bf[:, sl]
        ya = jnp.dot(head_mat(ja), xp, preferred_element_type=F32)
        yb = jnp.dot(head_mat(jb), xp, preferred_element_type=F32)
        e_p = jnp.where(lo, ecum[:, ja:ja + 1], ecum[:, jb:jb + 1])
        ys.append(jnp.where(lo, ya, yb) + e_p * y_off[:, sl])
        w_p = jnp.where(lo, wts[:, ja:ja + 1], wts[:, jb:jb + 1])
        xws.append((xs[:, sl] * w_p).astype(BF16))
        els.append(jnp.where(lo, elast[:, ja:ja + 1], elast[:, jb:jb + 1]))
    y = jnp.concatenate(ys, axis=1)
    xw = jnp.concatenate(xws, axis=1)
    el = jnp.concatenate(els, axis=1)

    bm_t = bm.T.astype(BF16)
    st_ref[...] = st * el + jnp.dot(bm_t, xw, preferred_element_type=F32)

    y = y + xs * dsk_ref[...]
    yg = y * _silu(z_ref[...].astype(F32))
    ms = jnp.mean(yg * yg, axis=-1, keepdims=True)
    o_ref[...] = (yg * lax.rsqrt(ms + RMS_EPS) * nw_ref[...]).astype(BF16)


def _ssd(proj, dt, cum, conv_w, conv_b, dskip, norm_w):
    hb = CH // BF16_SUBLANES

    def rows(col_fn):
        return lambda b, g, c: (b * NCH + c, col_fn(g))

    def halo(col_fn):
        return lambda b, g, c: (jnp.maximum((b * NCH + c) * hb - 1, 0), col_fn(g))

    xcol = lambda g: X0 // GW + g
    bcol = lambda g: (X0 + D_INNER) // STATE + g
    ccol = lambda g: (X0 + D_INNER + GROUPS * STATE) // STATE + g
    zcol = lambda g: Z0 // GW + g
    wbcol = lambda g: D_INNER // STATE + g
    wccol = lambda g: (D_INNER + GROUPS * STATE) // STATE + g
    par = lambda col_fn: (lambda b, g, c: (0, col_fn(g)))
    return pl.pallas_call(
        _ssd_kernel,
        grid=(BATCH, GROUPS, NCH),
        in_specs=[
            pl.BlockSpec((CH, GW), rows(xcol)),
            pl.BlockSpec((BF16_SUBLANES, GW), halo(xcol)),
            pl.BlockSpec((CH, STATE), rows(bcol)),
            pl.BlockSpec((BF16_SUBLANES, STATE), halo(bcol)),
            pl.BlockSpec((CH, STATE), rows(ccol)),
            pl.BlockSpec((BF16_SUBLANES, STATE), halo(ccol)),
            pl.BlockSpec((CH, GW), rows(zcol)),
            pl.BlockSpec((CH, LANES), lambda b, g, c: (b * NCH + c, 0)),
            pl.BlockSpec((CH, LANES), lambda b, g, c: (b * NCH + c, 0)),
            pl.BlockSpec((4, GW), par(lambda g: g)),
            pl.BlockSpec((4, STATE), par(wbcol)),
            pl.BlockSpec((4, STATE), par(wccol)),
            pl.BlockSpec((1, GW), par(lambda g: g)),
            pl.BlockSpec((1, STATE), par(wbcol)),
            pl.BlockSpec((1, STATE), par(wccol)),
            pl.BlockSpec((1, GW), par(lambda g: g)),
            pl.BlockSpec((1, GW), par(lambda g: g)),
        ],
        out_specs=pl.BlockSpec((CH, GW), lambda b, g, c: (b * NCH + c, g)),
        out_shape=jax.ShapeDtypeStruct((TP, D_INNER), BF16),
        scratch_shapes=[pltpu.VMEM((STATE, GW), F32)],
        compiler_params=_cp(("arbitrary", "arbitrary", "arbitrary"), 32),
        name="ssd",
    )(proj, proj, proj, proj, proj, proj, proj, dt, cum,
      conv_w, conv_w, conv_w, conv_b, conv_b, conv_b, dskip, norm_w)


def _outb_kernel(y_ref, w_ref, g_ref, bg_ref, ya_ref, o_ref):
    yb = jnp.dot(y_ref[...], w_ref[...], preferred_element_type=F32)
    gb = _sigmoid(g_ref[...].astype(F32) + bg_ref[...])
    o_ref[...] = (gb * yb + ya_ref[...].astype(F32)).astype(BF16)


def _out_b(yn, w_out_b, proj, bgate_b, ga_ya):
    tm = 512
    return pl.pallas_call(
        _outb_kernel,
        grid=(TP // tm,),
        in_specs=[
            pl.BlockSpec((tm, D_INNER), lambda i: (i, 0)),
            pl.BlockSpec((D_INNER, D), lambda i: (0, 0), pipeline_mode=pl.Buffered(1)),
            pl.BlockSpec((tm, D), lambda i: (i, G0 // D + 1)),
            pl.BlockSpec((1, D), lambda i: (0, 0)),
            pl.BlockSpec((tm, D), lambda i: (i, 0)),
        ],
        out_specs=pl.BlockSpec((tm, D), lambda i: (i, 0)),
        out_shape=jax.ShapeDtypeStruct((TP, D), BF16),
        compiler_params=_cp(("arbitrary",), 56),
        name="out_b",
    )(yn, w_out_b, proj, bgate_b, ga_ya)


def _wo_kernel(m_ref, w_ref, h_ref, g_ref, b_ref, hf_ref, hb_ref):
    mix = jnp.dot(m_ref[...], w_ref[...], preferred_element_type=F32)
    y = _layer_norm(ALPHA * h_ref[...] + mix, g_ref[...], b_ref[...])
    hf_ref[...] = y
    hb_ref[...] = y.astype(BF16)


def _wo_ln(m, w_o, h, g, b):
    tm = 512
    return pl.pallas_call(
        _wo_kernel,
        grid=(TP // tm,),
        in_specs=[
            pl.BlockSpec((tm, D), lambda i: (i, 0)),
            pl.BlockSpec((D, D), lambda i: (0, 0), pipeline_mode=pl.Buffered(1)),
            pl.BlockSpec((tm, D), lambda i: (i, 0)),
            pl.BlockSpec((1, D), lambda i: (0, 0)),
            pl.BlockSpec((1, D), lambda i: (0, 0)),
        ],
        out_specs=[
            pl.BlockSpec((tm, D), lambda i: (i, 0)),
            pl.BlockSpec((tm, D), lambda i: (i, 0)),
        ],
        out_shape=[jax.ShapeDtypeStruct((TP, D), F32), jax.ShapeDtypeStruct((TP, D), BF16)],
        compiler_params=_cp(("arbitrary",), 56),
        name="wo_ln",
    )(m, w_o, h, g, b)


def _swiglu_step(x_bf, wg_ref, wu_ref, wd_ref):
    gate = jnp.dot(x_bf, wg_ref[...], preferred_element_type=F32)
    up = jnp.dot(x_bf, wu_ref[...], preferred_element_type=F32)
    mid = (_silu(gate) * up).astype(BF16)
    return jnp.dot(mid, wd_ref[...], preferred_element_type=F32)


def _ffn_dense_kernel(x_ref, wg_ref, wu_ref, wd_ref, h_ref, g_ref, b_ref, hf_ref, hb_ref, acc_ref):
    i = pl.program_id(0)
    f = pl.program_id(1)

    @pl.when(f == 0)
    def _():
        acc_ref[...] = jnp.zeros_like(acc_ref)

    acc_ref[...] += _swiglu_step(x_ref[...], wg_ref, wu_ref, wd_ref)

    @pl.when(f == pl.num_programs(1) - 1)
    def _():
        y = _layer_norm(ALPHA * h_ref[...] + acc_ref[...], g_ref[...], b_ref[...])
        y = jnp.where(_valid_rows(i, TM_FFN), y, 0.0)
        hf_ref[...] = y
        hb_ref[...] = y.astype(BF16)


def _ffn_dense(h_bf, h_f, wg, wu, wd, g, b):
    tm, tf = TM_FFN, TF_FFN
    return pl.pallas_call(
        _ffn_dense_kernel,
        grid=(TP // tm, FF_DENSE // tf),
        in_specs=[
            pl.BlockSpec((tm, D), lambda i, f: (i, 0)),
            pl.BlockSpec((D, tf), lambda i, f: (0, f)),
            pl.BlockSpec((D, tf), lambda i, f: (0, f)),
            pl.BlockSpec((tf, D), lambda i, f: (f, 0)),
            pl.BlockSpec((tm, D), lambda i, f: (i, 0)),
            pl.BlockSpec((1, D), lambda i, f: (0, 0)),
            pl.BlockSpec((1, D), lambda i, f: (0, 0)),
        ],
        out_specs=[
            pl.BlockSpec((tm, D), lambda i, f: (i, 0)),
            pl.BlockSpec((tm, D), lambda i, f: (i, 0)),
        ],
        out_shape=[jax.ShapeDtypeStruct((TP, D), F32), jax.ShapeDtypeStruct((TP, D), BF16)],
        scratch_shapes=[pltpu.VMEM((tm, D), F32)],
        compiler_params=_cp(("arbitrary", "arbitrary"), 56),
        name="ffn_dense",
    )(h_bf, wg, wu, wd, h_f, g, b)


def _router_kernel(h_ref, r_ref, meta_ref, cnt_ref, carry_ref):
    i = pl.program_id(0)
    tm = TOK_TILE

    @pl.when(i == 0)
    def _():
        carry_ref[...] = jnp.zeros_like(carry_ref)

    logits = jnp.dot(h_ref[...], r_ref[...], precision=lax.Precision.HIGHEST,
                     preferred_element_type=F32)
    lane = lax.broadcasted_iota(jnp.int32, (tm, LANES), 1).astype(F32)
    neg = jnp.float32(-jnp.inf)
    logits = jnp.where(lane < N_EXPERTS, logits, neg)
    m1 = jnp.max(logits, axis=-1, keepdims=True)
    i1 = jnp.min(jnp.where(logits == m1, lane, float(LANES)), axis=-1, keepdims=True)
    rest = jnp.where(lane == i1, neg, logits)
    m2 = jnp.max(rest, axis=-1, keepdims=True)
    i2 = jnp.min(jnp.where(rest == m2, lane, float(LANES)), axis=-1, keepdims=True)
    e = jnp.exp(m2 - m1)
    w1 = 1.0 / (1.0 + e)
    w2 = e / (1.0 + e)

    valid = _valid_rows(i, tm)
    sel = jnp.logical_and(jnp.logical_or(lane == i1, lane == i2), valid)
    onehot = sel.astype(F32)
    r = lax.broadcasted_iota(jnp.int32, (tm, tm), 0)
    c = lax.broadcasted_iota(jnp.int32, (tm, tm), 1)
    strict = (r > c).astype(BF16)
    before = jnp.dot(strict, onehot.astype(BF16), preferred_element_type=F32) + carry_ref[0:1, :]
    rank1 = jnp.sum(jnp.where(lane == i1, before, 0.0), axis=-1, keepdims=True)
    rank2 = jnp.sum(jnp.where(lane == i2, before, 0.0), axis=-1, keepdims=True)
    total = carry_ref[0:1, :] + jnp.sum(onehot, axis=0, keepdims=True)
    carry_ref[...] = jnp.broadcast_to(total, carry_ref.shape)
    cnt_ref[...] = jnp.broadcast_to(total, cnt_ref.shape)

    meta = jnp.where(lane == 0, i1, 0.0)
    meta = jnp.where(lane == 1, i2, meta)
    meta = jnp.where(lane == 2, w1, meta)
    meta = jnp.where(lane == 3, w2, meta)
    meta = jnp.where(lane == 4, rank1, meta)
    meta = jnp.where(lane == 5, rank2, meta)
    meta_ref[...] = meta


def _router(h_f, router_w):
    tm = TOK_TILE
    return pl.pallas_call(
        _router_kernel,
        grid=(TP // tm,),
        in_specs=[
            pl.BlockSpec((tm, D), lambda i: (i, 0)),
            pl.BlockSpec((D, LANES), lambda i: (0, 0)),
        ],
        out_specs=[
            pl.BlockSpec((tm, LANES), lambda i: (i, 0)),
            pl.BlockSpec((8, LANES), lambda i: (0, 0)),
        ],
        out_shape=[jax.ShapeDtypeStruct((TP, LANES), F32), jax.ShapeDtypeStruct((8, LANES), F32)],
        scratch_shapes=[pltpu.VMEM((8, LANES), F32)],
        compiler_params=_cp(("arbitrary",), 32),
        name="router",
    )(h_f, router_w)


def _scatter_kernel(pos_ref, h_ref, init_ref, o_ref, sem):
    del init_ref
    i = pl.program_id(0)
    first = jnp.where(i % NCH == 0, PADF, 0)
    base = i * TOK_TILE

    def copy(t, k):
        dst = pos_ref[0, 0, 2 * t + k]
        return pltpu.make_async_copy(h_ref.at[pl.ds(base + t, 1)], o_ref.at[pl.ds(dst, 1)], sem)

    def start(t, carry):
        copy(t, 0).start()
        copy(t, 1).start()
        return carry

    def wait(t, carry):
        copy(t, 0).wait()
        copy(t, 1).wait()
        return carry

    lax.fori_loop(first, TOK_TILE, start, 0)
    lax.fori_loop(first, TOK_TILE, wait, 0)


def _scatter_rows(pos, h_f, init):
    return pl.pallas_call(
        _scatter_kernel,
        grid=(TP // TOK_TILE,),
        in_specs=[
            pl.BlockSpec((1, 1, 2 * TOK_TILE), lambda i: (i, 0, 0), memory_space=pltpu.SMEM),
            pl.BlockSpec(memory_space=pl.ANY),
            pl.BlockSpec(memory_space=pl.ANY),
        ],
        out_specs=pl.BlockSpec(memory_space=pl.ANY),
        out_shape=jax.ShapeDtypeStruct((P_MOE, D), F32),
        scratch_shapes=[pltpu.SemaphoreType.DMA(())],
        input_output_aliases={2: 0},
        compiler_params=pltpu.CompilerParams(dimension_semantics=("arbitrary",), has_side_effects=True),
        name="moe_scatter",
    )(pos, h_f, init)


def _ffn_moe_kernel(te_ref, nu_ref, x_ref, wg_ref, wu_ref, wd_ref, o_ref, acc_ref, xb_ref):
    del te_ref
    i = pl.program_id(0)
    f = pl.program_id(1)
    used = i < nu_ref[0]

    @pl.when(jnp.logical_and(used, f == 0))
    def _():
        acc_ref[...] = jnp.zeros_like(acc_ref)
        xb_ref[...] = x_ref[...].astype(BF16)

    @pl.when(used)
    def _():
        acc_ref[...] += _swiglu_step(xb_ref[...], wg_ref, wu_ref, wd_ref)

    @pl.when(jnp.logical_and(used, f == pl.num_programs(1) - 1))
    def _():
        o_ref[...] = acc_ref[...]

    @pl.when(jnp.logical_and(jnp.logical_not(used), f == 0))
    def _():
        o_ref[...] = jnp.zeros_like(o_ref)


def _ffn_moe(tile_e, n_used, xs, wg, wu, wd):
    tm, tf = TM_MOE, TF_MOE
    nf = FF_EXPERT // tf

    def fblk(i, f, nu):
        return jnp.where(i < nu[0], f, nf - 1)

    grid_spec = pltpu.PrefetchScalarGridSpec(
        num_scalar_prefetch=2,
        grid=(N_MOE_TILES, nf),
        in_specs=[
            pl.BlockSpec((tm, D), lambda i, f, te, nu: (i, 0)),
            pl.BlockSpec((None, D, tf), lambda i, f, te, nu: (te[i], 0, fblk(i, f, nu))),
            pl.BlockSpec((None, D, tf), lambda i, f, te, nu: (te[i], 0, fblk(i, f, nu))),
            pl.BlockSpec((None, tf, D), lambda i, f, te, nu: (te[i], fblk(i, f, nu), 0)),
        ],
        out_specs=pl.BlockSpec((tm, D), lambda i, f, te, nu: (i, 0)),
        scratch_shapes=[pltpu.VMEM((tm, D), F32), pltpu.VMEM((tm, D), BF16)],
    )
    return pl.pallas_call(
        _ffn_moe_kernel,
        grid_spec=grid_spec,
        out_shape=jax.ShapeDtypeStruct((P_MOE, D), F32),
        compiler_params=_cp(("arbitrary", "arbitrary"), 58),
        name="ffn_moe",
    )(tile_e, n_used, xs, wg, wu, wd)


def _combine_kernel(pos_ref, y_ref, meta_ref, h_ref, g_ref, b_ref, hf_ref, hb_ref, buf1, buf2, sem):
    i = pl.program_id(0)
    tm = TOK_TILE

    def copy(t, k):
        src = pos_ref[0, 0, 2 * t + k]
        buf = buf1 if k == 0 else buf2
        return pltpu.make_async_copy(y_ref.at[pl.ds(src, 1)], buf.at[pl.ds(t, 1)], sem)

    def start(t, carry):
        copy(t, 0).start()
        copy(t, 1).start()
        return carry

    def wait(t, carry):
        copy(t, 0).wait()
        copy(t, 1).wait()
        return carry

    lax.fori_loop(0, tm, start, 0)
    lax.fori_loop(0, tm, wait, 0)

    meta = meta_ref[...]
    w1 = meta[:, 2:3]
    w2 = meta[:, 3:4]
    f = w1 * buf1[...] + w2 * buf2[...]
    y = _layer_norm(ALPHA * h_ref[...] + f, g_ref[...], b_ref[...])
    y = jnp.where(_valid_rows(i, tm), y, 0.0)
    hf_ref[...] = y
    hb_ref[...] = y.astype(BF16)


def _combine(pos, y, meta, h_f, g, b):
    tm = TOK_TILE
    return pl.pallas_call(
        _combine_kernel,
        grid=(TP // tm,),
        in_specs=[
            pl.BlockSpec((1, 1, 2 * tm), lambda i: (i, 0, 0), memory_space=pltpu.SMEM),
            pl.BlockSpec(memory_space=pl.ANY),
            pl.BlockSpec((tm, LANES), lambda i: (i, 0)),
            pl.BlockSpec((tm, D), lambda i: (i, 0)),
            pl.BlockSpec((1, D), lambda i: (0, 0)),
            pl.BlockSpec((1, D), lambda i: (0, 0)),
        ],
        out_specs=[
            pl.BlockSpec((tm, D), lambda i: (i, 0)),
            pl.BlockSpec((tm, D), lambda i: (i, 0)),
        ],
        out_shape=[jax.ShapeDtypeStruct((TP, D), F32), jax.ShapeDtypeStruct((TP, D), BF16)],
        scratch_shapes=[pltpu.VMEM((tm, D), F32), pltpu.VMEM((tm, D), F32), pltpu.SemaphoreType.DMA(())],
        compiler_params=_cp(("arbitrary",), 32),
        name="moe_combine",
    )(pos, y, meta, h_f, g, b)


def _pad_lanes(v, width=LANES):
    return jnp.pad(v, ((0, 0), (0, width - v.shape[1])))


def _token_mixer(h_f, h_bf, w_in, b_gate, conv_a_w, w_out_a, ssd_conv_w, ssd_conv_b, dt_bias, a_log,
                 d_skip, ssd_norm_w, w_out_b, w_o, ln_g, ln_b):
    dt0 = G0
    w_main = jnp.concatenate([w_in[:, :dt0], w_in[:, dt0 + HEADS:]], axis=1).astype(BF16)
    w_dt = _pad_lanes(w_in[:, dt0:dt0 + HEADS]).astype(BF16)
    proj = _in_proj(h_bf, w_main)
    dt, cum = _dt_proj(h_bf, w_dt, _pad_lanes(dt_bias[None, :]), _pad_lanes(a_log[None, :]))
    ga_ya = _mixer_a(proj, conv_a_w, w_out_a.astype(BF16), b_gate[None, :D])
    dskip = jnp.repeat(d_skip, HEADDIM)[None, :]
    yn = _ssd(proj, dt, cum, ssd_conv_w, ssd_conv_b[None, :], dskip, ssd_norm_w[None, :])
    m = _out_b(yn, w_out_b.astype(BF16), proj, b_gate[None, D:], ga_ya)
    return _wo_ln(m, w_o.astype(BF16), h_f, ln_g[None, :], ln_b[None, :])


def _moe_layer(h_f, router_w, wg, wu, wd, ln_g, ln_b):
    meta, cnt = _router(h_f, _pad_lanes(router_w))
    i1 = meta[:, 0].astype(jnp.int32)
    i2 = meta[:, 1].astype(jnp.int32)
    counts = cnt[0, :N_EXPERTS].astype(jnp.int32)
    padded = ((counts + TM_MOE - 1) // TM_MOE) * TM_MOE
    ends = jnp.cumsum(padded)
    starts = ends - padded
    pos = jnp.stack([starts[i1] + meta[:, 4].astype(jnp.int32),
                     starts[i2] + meta[:, 5].astype(jnp.int32)], axis=1)
    pos = jnp.clip(pos, 0, P_MOE - 1).reshape(TP // TOK_TILE, 1, 2 * TOK_TILE)
    n_used = ends[-1] // TM_MOE
    tile_start = jnp.arange(N_MOE_TILES, dtype=jnp.int32) * TM_MOE
    tile_e = jnp.searchsorted(ends, tile_start, side="right").astype(jnp.int32)
    last_e = jnp.minimum(tile_e[jnp.maximum(n_used - 1, 0)], N_EXPERTS - 1)
    tile_e = jnp.where(tile_start < ends[-1], jnp.minimum(tile_e, N_EXPERTS - 1), last_e)
    xs = _scatter_rows(pos, h_f, jnp.zeros((P_MOE, D), F32))
    y = _ffn_moe(tile_e, n_used.reshape(1).astype(jnp.int32), xs,
                 wg.astype(BF16), wu.astype(BF16), wd.astype(BF16))
    return _combine(pos, y, meta, h_f, ln_g[None, :], ln_b[None, :])


def kernel(x, meta_tokens, ln_in_g, ln_in_b, w_in, b_gate, conv_a_w, w_out_a, ssd_conv_w, ssd_conv_b,
           dt_bias, a_log, d_skip, ssd_norm_w, w_out_b, w_o, ln1_g, ln1_b, ffn_w_gate, ffn_w_up,
           ffn_w_down, router, moe_w_gate, moe_w_up, moe_w_down, ln2_g, ln2_b):
    h_f, h_bf = _ln_in(x, meta_tokens.astype(x.dtype), ln_in_g[None, :], ln_in_b[None, :])
    for i in range(DEPTH):
        h_f, h_bf = _token_mixer(h_f, h_bf, w_in[i], b_gate[i], conv_a_w[i], w_out_a[i], ssd_conv_w[i],
                                 ssd_conv_b[i], dt_bias[i], a_log[i], d_skip[i], ssd_norm_w[i],
                                 w_out_b[i], w_o[i], ln1_g[i], ln1_b[i])
        j = i // 2
        if i % 2 == 0:
            h_f, h_bf = _ffn_dense(h_bf, h_f, ffn_w_gate[j].astype(BF16), ffn_w_up[j].astype(BF16),
                                   ffn_w_down[j].astype(BF16), ln2_g[i][None, :], ln2_b[i][None, :])
        else:
            h_f, h_bf = _moe_layer(h_f, router[j], moe_w_gate[j], moe_w_up[j], moe_w_down[j],
                                   ln2_g[i], ln2_b[i])
    return h_f.reshape(BATCH, LP, D)[:, PADF + N_META:, :]
```

```python
import functools

import jax
import jax.numpy as jnp
from jax import lax
from jax.experimental import pallas as pl
from jax.experimental.pallas import tpu as pltpu

F32 = jnp.float32
BF16 = jnp.bfloat16

D = 2048
BATCH = 4
SEQ = 4096
N_META = 16
DEPTH = 2
D_INNER = 2 * D
HEADDIM = 64
HEADS = D_INNER // HEADDIM
GROUPS = 8
HPG = HEADS // GROUPS
STATE = 128
GW = D_INNER // GROUPS
XBC = D_INNER + 2 * GROUPS * STATE
FF_DENSE = 5632
N_EXPERTS = 8
FF_EXPERT = 7168
ALPHA = (2.0 * DEPTH) ** 0.25
LN_EPS = 1e-5
RMS_EPS = 1e-5

CH = 256
PADF = CH - N_META
LP = PADF + N_META + SEQ
NCH = LP // CH
TP = BATCH * LP

C0, B0, V0 = 0, D, 2 * D
Z0 = 3 * D
X0 = Z0 + D_INNER
G0 = X0 + XBC
NP = G0 + 2 * D

LANES = 128
SUBLANES = 8
BF16_SUBLANES = 16

TM_MOE = 512
N_MOE_TILES = (2 * BATCH * (SEQ + N_META)) // TM_MOE + N_EXPERTS
P_MOE = N_MOE_TILES * TM_MOE
TF_MOE = 1024
TM_FFN = 512
TF_FFN = 512
TOK_TILE = CH

SUB = 128
LOG2E = 1.4426950408889634
EXP2_CLAMP = 100.0

MIB = 1024 * 1024


def _cp(sem, vmem_mib):
    return pltpu.CompilerParams(dimension_semantics=sem, vmem_limit_bytes=int(vmem_mib * MIB))


def _sigmoid(x):
    return 1.0 / (1.0 + jnp.exp(-x))


def _silu(x):
    return x * _sigmoid(x)


def _softplus(x):
    return jnp.maximum(x, 0.0) + jnp.log1p(jnp.exp(-jnp.abs(x)))


def _layer_norm(x, g, b):
    mu = jnp.mean(x, axis=-1, keepdims=True)
    xc = x - mu
    var = jnp.mean(xc * xc, axis=-1, keepdims=True)
    return xc * lax.rsqrt(var + LN_EPS) * g + b


def _valid_rows(tile_idx, tm):
    row = lax.broadcasted_iota(jnp.int32, (tm, 1), 0)
    valid = None
    for s in range(tm // CH):
        chunk = tile_idx * (tm // CH) + s
        hi = jnp.where(chunk % NCH == 0, s * CH + PADF, s * CH)
        ok = jnp.logical_or(row < s * CH, row >= hi)
        valid = ok if valid is None else jnp.logical_and(valid, ok)
    return valid


def _ln_in_kernel(x_ref, meta_ref, g_ref, b_ref, hf_ref, hb_ref):
    j = pl.program_id(1)

    @pl.when(j == 0)
    def _():
        hf_ref[...] = jnp.zeros_like(hf_ref)
        hb_ref[...] = jnp.zeros_like(hb_ref)
        y = _layer_norm(meta_ref[...], g_ref[...], b_ref[...])
        hf_ref[PADF:, :] = y
        hb_ref[PADF:, :] = y.astype(BF16)

    @pl.when(j > 0)
    def _():
        y = _layer_norm(x_ref[0], g_ref[...], b_ref[...])
        hf_ref[...] = y
        hb_ref[...] = y.astype(BF16)


def _ln_in(x, meta, g, b):
    return pl.pallas_call(
        _ln_in_kernel,
        grid=(BATCH, NCH),
        in_specs=[
            pl.BlockSpec((1, CH, D), lambda bi, j: (bi, jnp.maximum(j - 1, 0), 0)),
            pl.BlockSpec((N_META, D), lambda bi, j: (0, 0)),
            pl.BlockSpec((1, D), lambda bi, j: (0, 0)),
            pl.BlockSpec((1, D), lambda bi, j: (0, 0)),
        ],
        out_specs=[
            pl.BlockSpec((CH, D), lambda bi, j: (bi * NCH + j, 0)),
            pl.BlockSpec((CH, D), lambda bi, j: (bi * NCH + j, 0)),
        ],
        out_shape=[jax.ShapeDtypeStruct((TP, D), F32), jax.ShapeDtypeStruct((TP, D), BF16)],
        compiler_params=_cp(("arbitrary", "arbitrary"), 32),
        name="ln_in",
    )(x, meta, g, b)


def _mm_kernel(a_ref, w_ref, o_ref):
    o_ref[...] = jnp.dot(a_ref[...], w_ref[...], preferred_element_type=F32).astype(o_ref.dtype)


def _in_proj(h_bf, w_main):
    tm, tn = 1024, 2048
    return pl.pallas_call(
        _mm_kernel,
        grid=(NP // tn, TP // tm),
        in_specs=[
            pl.BlockSpec((tm, D), lambda j, i: (i, 0)),
            pl.BlockSpec((D, tn), lambda j, i: (0, j)),
        ],
        out_specs=pl.BlockSpec((tm, tn), lambda j, i: (i, j)),
        out_shape=jax.ShapeDtypeStruct((TP, NP), BF16),
        compiler_params=_cp(("arbitrary", "arbitrary"), 48),
        name="in_proj",
    )(h_bf, w_main)


def _dt_kernel(a_ref, w_ref, bias_ref, alog_ref, dt_ref, cum_ref):
    i = pl.program_id(0)
    raw = jnp.dot(a_ref[...], w_ref[...], preferred_element_type=F32) + bias_ref[...]
    dt = _softplus(raw)
    a_neg = -jnp.exp(alog_ref[...])
    row = lax.broadcasted_iota(jnp.int32, (CH, 1), 0)
    first = jnp.where(i % NCH == 0, PADF, 0)
    da = jnp.where(row >= first, dt * a_neg, 0.0)
    r = lax.broadcasted_iota(jnp.int32, (CH, CH), 0)
    c = lax.broadcasted_iota(jnp.int32, (CH, CH), 1)
    tril = (r >= c).astype(F32)
    cum = jnp.dot(tril, da, precision=lax.Precision.HIGHEST, preferred_element_type=F32)
    dt_ref[...] = dt
    cum_ref[...] = cum


def _dt_proj(h_bf, w_dt, dt_bias, a_log):
    return pl.pallas_call(
        _dt_kernel,
        grid=(TP // CH,),
        in_specs=[
            pl.BlockSpec((CH, D), lambda i: (i, 0)),
            pl.BlockSpec((D, LANES), lambda i: (0, 0)),
            pl.BlockSpec((1, LANES), lambda i: (0, 0)),
            pl.BlockSpec((1, LANES), lambda i: (0, 0)),
        ],
        out_specs=[
            pl.BlockSpec((CH, LANES), lambda i: (i, 0)),
            pl.BlockSpec((CH, LANES), lambda i: (i, 0)),
        ],
        out_shape=[jax.ShapeDtypeStruct((TP, LANES), F32), jax.ShapeDtypeStruct((TP, LANES), F32)],
        compiler_params=_cp(("arbitrary",), 16),
        name="dt_proj",
    )(h_bf, w_dt, dt_bias, a_log)


def _causal_conv(x, halo, w, k):
    acc = w[k - 1:k, :] * x
    for s in range(1, k):
        acc = acc + w[k - 1 - s:k - s, :] * pltpu.roll(x, s, 0)
    x8 = x[0:SUBLANES, :]
    h8 = halo[BF16_SUBLANES - SUBLANES:, :]
    row = lax.broadcasted_iota(jnp.int32, (SUBLANES, 1), 0)
    top = w[k - 1:k, :] * x8
    for s in range(1, k):
        shifted = jnp.where(row < s, pltpu.roll(h8, s, 0), pltpu.roll(x8, s, 0))
        top = top + w[k - 1 - s:k - s, :] * shifted
    return jnp.concatenate([top, acc[SUBLANES:, :]], axis=0)


def _mixa_kernel(c_ref, b_ref, v_ref, ch_ref, vh_ref, g_ref, cw_ref, wo_ref, bg_ref, o_ref):
    u = c_ref[...].astype(F32) * v_ref[...].astype(F32)
    hu = ch_ref[...].astype(F32) * vh_ref[...].astype(F32)
    conv = _causal_conv(u, hu, cw_ref[...], 3)
    lhs = (b_ref[...].astype(F32) * conv).astype(BF16)
    ya = jnp.dot(lhs, wo_ref[...], preferred_element_type=F32)
    ga = _sigmoid(g_ref[...].astype(F32) + bg_ref[...])
    o_ref[...] = (ga * ya).astype(BF16)


def _mixer_a(proj, conv_w, w_out_a, bgate_a):
    tm = 512
    hb = tm // BF16_SUBLANES
    halo = lambda col: (lambda i: (jnp.maximum(i * hb - 1, 0), col))
    return pl.pallas_call(
        _mixa_kernel,
        grid=(TP // tm,),
        in_specs=[
            pl.BlockSpec((tm, D), lambda i: (i, C0 // D)),
            pl.BlockSpec((tm, D), lambda i: (i, B0 // D)),
            pl.BlockSpec((tm, D), lambda i: (i, V0 // D)),
            pl.BlockSpec((BF16_SUBLANES, D), halo(C0 // D)),
            pl.BlockSpec((BF16_SUBLANES, D), halo(V0 // D)),
            pl.BlockSpec((tm, D), lambda i: (i, G0 // D)),
            pl.BlockSpec((3, D), lambda i: (0, 0)),
            pl.BlockSpec((D, D), lambda i: (0, 0), pipeline_mode=pl.Buffered(1)),
            pl.BlockSpec((1, D), lambda i: (0, 0)),
        ],
        out_specs=pl.BlockSpec((tm, D), lambda i: (i, 0)),
        out_shape=jax.ShapeDtypeStruct((TP, D), BF16),
        compiler_params=_cp(("arbitrary",), 56),
        name="mixer_a",
    )(proj, proj, proj, proj, proj, proj, conv_w, w_out_a, bgate_a)


def _ssd_kernel(xs_ref, xh_ref, bm_ref, bh_ref, cm_ref, chh_ref, z_ref, dt_ref, cum_ref,
                wx_ref, wb_ref, wc_ref, bx_ref, bb_ref, bc_ref, dsk_ref, nw_ref,
                o_ref, st_ref):
    g = pl.program_id(1)
    c = pl.program_id(2)

    @pl.when(c == 0)
    def _():
        st_ref[...] = jnp.zeros_like(st_ref)

    row = lax.broadcasted_iota(jnp.int32, (CH, 1), 0)
    first = jnp.where(c == 0, PADF, 0)

    def conv_silu(x_ref, h_ref, w_ref, b_ref):
        y = _causal_conv(x_ref[...].astype(F32), h_ref[...].astype(F32), w_ref[...], 4)
        return _silu(y + b_ref[...])

    xs = conv_silu(xs_ref, xh_ref, wx_ref, bx_ref)
    xs = jnp.where(row >= first, xs, 0.0)
    bm = conv_silu(bm_ref, bh_ref, wb_ref, bb_ref)
    cm = conv_silu(cm_ref, chh_ref, wc_ref, bc_ref)

    shift = (LANES - HPG * g) % LANES
    dtg = pltpu.roll(dt_ref[...], shift, 1)
    cumg = pltpu.roll(cum_ref[...], shift, 1) * LOG2E

    bm_bf = bm.astype(BF16)
    cm_bf = cm.astype(BF16)
    xs_bf = xs.astype(BF16)
    r2 = lax.broadcasted_iota(jnp.int32, (SUB, SUB), 0)
    c2 = lax.broadcasted_iota(jnp.int32, (SUB, SUB), 1)
    causal = r2 >= c2
    lane = lax.broadcasted_iota(jnp.int32, (1, LANES), 1)
    lo = lane < HEADDIM

    st = st_ref[...]
    y_rows = []
    for q in range(CH // SUB):
        rs = slice(q * SUB, (q + 1) * SUB)
        cq = cumg[rs, :]
        if q > 0:
            cq = cq - cumg[q * SUB - 1:q * SUB, :]
        dq = dtg[rs, :]
        last = cq[SUB - 1:SUB, :]
        ecum = jnp.exp2(cq)
        wts = jnp.exp2(last - cq) * dq
        elast = jnp.exp2(last)
        src_t = (cq - jnp.log2(dq)).T

        cb = lax.dot_general(cm_bf[rs, :], bm_bf[rs, :], (((1,), (1,)), ((), ())),
                             preferred_element_type=F32)
        cbm = jnp.where(causal, cb, 0.0)
        y_off = jnp.dot(cm_bf[rs, :], st.astype(BF16), preferred_element_type=F32)

        def head_mat(j):
            seg = cq[:, j:j + 1] - src_t[j:j + 1, :]
            return (cbm * jnp.exp2(jnp.minimum(seg, EXP2_CLAMP))).astype(BF16)

        ys, xws, els = [], [], []
        for p in range(GW // LANES):
            ja, jb = 2 * p, 2 * p + 1
            sl = slice(p * LANES, (p + 1) * LANES)
            xp = xs_bf[rs, sl]
            ya = jnp.dot(head_mat(ja), xp, preferred_element_type=F32)
            yb = jnp.dot(head_mat(jb), xp, preferred_element_type=F32)
            e_p = jnp.where(lo, ecum[:, ja:ja + 1], ecum[:, jb:jb + 1])
            ys.append(jnp.where(lo, ya, yb) + e_p * y_off[:, sl])
            w_p = jnp.where(lo, wts[:, ja:ja + 1], wts[:, jb:jb + 1])
            xws.append((xs[rs, sl] * w_p).astype(BF16))
            els.append(jnp.where(lo, elast[:, ja:ja + 1], elast[:, jb:jb + 1]))
        y_rows.append(jnp.concatenate(ys, axis=1))
        xw = jnp.concatenate(xws, axis=1)
        el = jnp.concatenate(els, axis=1)
        bm_t = bm[rs, :].T.astype(BF16)
        st = st * el + jnp.dot(bm_t, xw, preferred_element_type=F32)
    st_ref[...] = st
    y = jnp.concatenate(y_rows, axis=0)

    y = y + xs * dsk_ref[...]
    yg = y * _silu(z_ref[...].astype(F32))
    ms = jnp.mean(yg * yg, axis=-1, keepdims=True)
    o_ref[...] = (yg * lax.rsqrt(ms + RMS_EPS) * nw_ref[...]).astype(BF16)


def _ssd(proj, dt, cum, conv_w, conv_b, dskip, norm_w):
    hb = CH // BF16_SUBLANES

    def rows(col_fn):
        return lambda b, g, c: (b * NCH + c, col_fn(g))

    def halo(col_fn):
        return lambda b, g, c: (jnp.maximum((b * NCH + c) * hb - 1, 0), col_fn(g))

    xcol = lambda g: X0 // GW + g
    bcol = lambda g: (X0 + D_INNER) // STATE + g
    ccol = lambda g: (X0 + D_INNER + GROUPS * STATE) // STATE + g
    zcol = lambda g: Z0 // GW + g
    wbcol = lambda g: D_INNER // STATE + g
    wccol = lambda g: (D_INNER + GROUPS * STATE) // STATE + g
    par = lambda col_fn: (lambda b, g, c: (0, col_fn(g)))
    return pl.pallas_call(
        _ssd_kernel,
        grid=(BATCH, GROUPS, NCH),
        in_specs=[
            pl.BlockSpec((CH, GW), rows(xcol)),
            pl.BlockSpec((BF16_SUBLANES, GW), halo(xcol)),
            pl.BlockSpec((CH, STATE), rows(bcol)),
            pl.BlockSpec((BF16_SUBLANES, STATE), halo(bcol)),
            pl.BlockSpec((CH, STATE), rows(ccol)),
            pl.BlockSpec((BF16_SUBLANES, STATE), halo(ccol)),
            pl.BlockSpec((CH, GW), rows(zcol)),
            pl.BlockSpec((CH, LANES), lambda b, g, c: (b * NCH + c, 0)),
            pl.BlockSpec((CH, LANES), lambda b, g, c: (b * NCH + c, 0)),
            pl.BlockSpec((4, GW), par(lambda g: g)),
            pl.BlockSpec((4, STATE), par(wbcol)),
            pl.BlockSpec((4, STATE), par(wccol)),
            pl.BlockSpec((1, GW), par(lambda g: g)),
            pl.BlockSpec((1, STATE), par(wbcol)),
            pl.BlockSpec((1, STATE), par(wccol)),
            pl.BlockSpec((1, GW), par(lambda g: g)),
            pl.BlockSpec((1, GW), par(lambda g: g)),
        ],
        out_specs=pl.BlockSpec((CH, GW), lambda b, g, c: (b * NCH + c, g)),
        out_shape=jax.ShapeDtypeStruct((TP, D_INNER), BF16),
        scratch_shapes=[pltpu.VMEM((STATE, GW), F32)],
        compiler_params=_cp(("arbitrary", "arbitrary", "arbitrary"), 32),
        name="ssd",
    )(proj, proj, proj, proj, proj, proj, proj, dt, cum,
      conv_w, conv_w, conv_w, conv_b, conv_b, conv_b, dskip, norm_w)


def _outb_kernel(y_ref, w_ref, g_ref, bg_ref, ya_ref, o_ref):
    yb = jnp.dot(y_ref[...], w_ref[...], preferred_element_type=F32)
    gb = _sigmoid(g_ref[...].astype(F32) + bg_ref[...])
    o_ref[...] = (gb * yb + ya_ref[...].astype(F32)).astype(BF16)


def _out_b(yn, w_out_b, proj, bgate_b, ga_ya):
    tm = 512
    return pl.pallas_call(
        _outb_kernel,
        grid=(TP // tm,),
        in_specs=[
            pl.BlockSpec((tm, D_INNER), lambda i: (i, 0)),
            pl.BlockSpec((D_INNER, D), lambda i: (0, 0), pipeline_mode=pl.Buffered(1)),
            pl.BlockSpec((tm, D), lambda i: (i, G0 // D + 1)),
            pl.BlockSpec((1, D), lambda i: (0, 0)),
            pl.BlockSpec((tm, D), lambda i: (i, 0)),
        ],
        out_specs=pl.BlockSpec((tm, D), lambda i: (i, 0)),
        out_shape=jax.ShapeDtypeStruct((TP, D), BF16),
        compiler_params=_cp(("arbitrary",), 56),
        name="out_b",
    )(yn, w_out_b, proj, bgate_b, ga_ya)


def _wo_kernel(m_ref, w_ref, h_ref, g_ref, b_ref, hf_ref, hb_ref):
    mix = jnp.dot(m_ref[...], w_ref[...], preferred_element_type=F32)
    y = _layer_norm(ALPHA * h_ref[...] + mix, g_ref[...], b_ref[...])
    hf_ref[...] = y
    hb_ref[...] = y.astype(BF16)


def _wo_ln(m, w_o, h, g, b):
    tm = 512
    return pl.pallas_call(
        _wo_kernel,
        grid=(TP // tm,),
        in_specs=[
            pl.BlockSpec((tm, D), lambda i: (i, 0)),
            pl.BlockSpec((D, D), lambda i: (0, 0), pipeline_mode=pl.Buffered(1)),
            pl.BlockSpec((tm, D), lambda i: (i, 0)),
            pl.BlockSpec((1, D), lambda i: (0, 0)),
            pl.BlockSpec((1, D), lambda i: (0, 0)),
        ],
        out_specs=[
            pl.BlockSpec((tm, D), lambda i: (i, 0)),
            pl.BlockSpec((tm, D), lambda i: (i, 0)),
        ],
        out_shape=[jax.ShapeDtypeStruct((TP, D), F32), jax.ShapeDtypeStruct((TP, D), BF16)],
        compiler_params=_cp(("arbitrary",), 56),
        name="wo_ln",
    )(m, w_o, h, g, b)


def _swiglu_step(x_bf, wg_ref, wu_ref, wd_ref):
    gate = jnp.dot(x_bf, wg_ref[...], preferred_element_type=F32)
    up = jnp.dot(x_bf, wu_ref[...], preferred_element_type=F32)
    mid = (_silu(gate) * up).astype(BF16)
    return jnp.dot(mid, wd_ref[...], preferred_element_type=F32)


def _ffn_dense_kernel(x_ref, wg_ref, wu_ref, wd_ref, h_ref, g_ref, b_ref, hf_ref, hb_ref, acc_ref):
    i = pl.program_id(0)
    f = pl.program_id(1)

    @pl.when(f == 0)
    def _():
        acc_ref[...] = jnp.zeros_like(acc_ref)

    acc_ref[...] += _swiglu_step(x_ref[...], wg_ref, wu_ref, wd_ref)

    @pl.when(f == pl.num_programs(1) - 1)
    def _():
        y = _layer_norm(ALPHA * h_ref[...] + acc_ref[...], g_ref[...], b_ref[...])
        y = jnp.where(_valid_rows(i, TM_FFN), y, 0.0)
        hf_ref[...] = y
        hb_ref[...] = y.astype(BF16)


def _ffn_dense(h_bf, h_f, wg, wu, wd, g, b):
    tm, tf = TM_FFN, TF_FFN
    return pl.pallas_call(
        _ffn_dense_kernel,
        grid=(TP // tm, FF_DENSE // tf),
        in_specs=[
            pl.BlockSpec((tm, D), lambda i, f: (i, 0)),
            pl.BlockSpec((D, tf), lambda i, f: (0, f)),
            pl.BlockSpec((D, tf), lambda i, f: (0, f)),
            pl.BlockSpec((tf, D), lambda i, f: (f, 0)),
            pl.BlockSpec((tm, D), lambda i, f: (i, 0)),
            pl.BlockSpec((1, D), lambda i, f: (0, 0)),
            pl.BlockSpec((1, D), lambda i, f: (0, 0)),
        ],
        out_specs=[
            pl.BlockSpec((tm, D), lambda i, f: (i, 0)),
            pl.BlockSpec((tm, D), lambda i, f: (i, 0)),
        ],
        out_shape=[jax.ShapeDtypeStruct((TP, D), F32), jax.ShapeDtypeStruct((TP, D), BF16)],
        scratch_shapes=[pltpu.VMEM((tm, D), F32)],
        compiler_params=_cp(("arbitrary", "arbitrary"), 56),
        name="ffn_dense",
    )(h_bf, wg, wu, wd, h_f, g, b)


def _router_kernel(h_ref, r_ref, meta_ref, cnt_ref, carry_ref):
    i = pl.program_id(0)
    tm = TOK_TILE

    @pl.when(i == 0)
    def _():
        carry_ref[...] = jnp.zeros_like(carry_ref)

    logits = jnp.dot(h_ref[...], r_ref[...], precision=lax.Precision.HIGHEST,
                     preferred_element_type=F32)
    lane = lax.broadcasted_iota(jnp.int32, (tm, LANES), 1).astype(F32)
    neg = jnp.float32(-jnp.inf)
    logits = jnp.where(lane < N_EXPERTS, logits, neg)
    m1 = jnp.max(logits, axis=-1, keepdims=True)
    i1 = jnp.min(jnp.where(logits == m1, lane, float(LANES)), axis=-1, keepdims=True)
    rest = jnp.where(lane == i1, neg, logits)
    m2 = jnp.max(rest, axis=-1, keepdims=True)
    i2 = jnp.min(jnp.where(rest == m2, lane, float(LANES)), axis=-1, keepdims=True)
    e = jnp.exp(m2 - m1)
    w1 = 1.0 / (1.0 + e)
    w2 = e / (1.0 + e)

    valid = _valid_rows(i, tm)
    sel = jnp.logical_and(jnp.logical_or(lane == i1, lane == i2), valid)
    onehot = sel.astype(F32)
    r = lax.broadcasted_iota(jnp.int32, (tm, tm), 0)
    c = lax.broadcasted_iota(jnp.int32, (tm, tm), 1)
    strict = (r > c).astype(BF16)
    before = jnp.dot(strict, onehot.astype(BF16), preferred_element_type=F32) + carry_ref[0:1, :]
    rank1 = jnp.sum(jnp.where(lane == i1, before, 0.0), axis=-1, keepdims=True)
    rank2 = jnp.sum(jnp.where(lane == i2, before, 0.0), axis=-1, keepdims=True)
    total = carry_ref[0:1, :] + jnp.sum(onehot, axis=0, keepdims=True)
    carry_ref[...] = jnp.broadcast_to(total, carry_ref.shape)
    cnt_ref[...] = jnp.broadcast_to(total, cnt_ref.shape)

    meta = jnp.where(lane == 0, i1, 0.0)
    meta = jnp.where(lane == 1, i2, meta)
    meta = jnp.where(lane == 2, w1, meta)
    meta = jnp.where(lane == 3, w2, meta)
    meta = jnp.where(lane == 4, rank1, meta)
    meta = jnp.where(lane == 5, rank2, meta)
    meta_ref[...] = meta


def _router(h_f, router_w):
    tm = TOK_TILE
    return pl.pallas_call(
        _router_kernel,
        grid=(TP // tm,),
        in_specs=[
            pl.BlockSpec((tm, D), lambda i: (i, 0)),
            pl.BlockSpec((D, LANES), lambda i: (0, 0)),
        ],
        out_specs=[
            pl.BlockSpec((tm, LANES), lambda i: (i, 0)),
            pl.BlockSpec((8, LANES), lambda i: (0, 0)),
        ],
        out_shape=[jax.ShapeDtypeStruct((TP, LANES), F32), jax.ShapeDtypeStruct((8, LANES), F32)],
        scratch_shapes=[pltpu.VMEM((8, LANES), F32)],
        compiler_params=_cp(("arbitrary",), 32),
        name="router",
    )(h_f, router_w)


def _gather_kernel(nu_ref, src_ref, h_ref, o_ref, buf, sem):
    i = pl.program_id(0)
    used = i < nu_ref[0]

    def copy(t):
        return pltpu.make_async_copy(h_ref.at[pl.ds(src_ref[0, 0, t], 1)], buf.at[pl.ds(t, 1)], sem)

    def start(t, carry):
        copy(t).start()
        return carry

    def wait(t, carry):
        copy(t).wait()
        return carry

    @pl.when(used)
    def _():
        lax.fori_loop(0, TM_MOE, start, 0, unroll=8)
        lax.fori_loop(0, TM_MOE, wait, 0, unroll=8)
        o_ref[...] = buf[...].astype(BF16)

    @pl.when(jnp.logical_not(used))
    def _():
        o_ref[...] = jnp.zeros_like(o_ref)


def _gather_rows(n_used, src, h_f):
    grid_spec = pltpu.PrefetchScalarGridSpec(
        num_scalar_prefetch=1,
        grid=(N_MOE_TILES,),
        in_specs=[
            pl.BlockSpec((1, 1, TM_MOE), lambda i, nu: (i, 0, 0), memory_space=pltpu.SMEM),
            pl.BlockSpec(memory_space=pl.ANY),
        ],
        out_specs=pl.BlockSpec((TM_MOE, D), lambda i, nu: (i, 0)),
        scratch_shapes=[pltpu.VMEM((TM_MOE, D), F32), pltpu.SemaphoreType.DMA(())],
    )
    return pl.pallas_call(
        _gather_kernel,
        grid_spec=grid_spec,
        out_shape=jax.ShapeDtypeStruct((P_MOE, D), BF16),
        compiler_params=_cp(("arbitrary",), 32),
        name="moe_gather",
    )(n_used, src, h_f)


def _ffn_moe_kernel(te_ref, nu_ref, x_ref, wg_ref, wu_ref, wd_ref, o_ref, acc_ref):
    del te_ref
    i = pl.program_id(0)
    f = pl.program_id(1)
    used = i < nu_ref[0]

    @pl.when(jnp.logical_and(used, f == 0))
    def _():
        acc_ref[...] = jnp.zeros_like(acc_ref)

    @pl.when(used)
    def _():
        acc_ref[...] += _swiglu_step(x_ref[...], wg_ref, wu_ref, wd_ref)

    @pl.when(jnp.logical_and(used, f == pl.num_programs(1) - 1))
    def _():
        o_ref[...] = acc_ref[...]

    @pl.when(jnp.logical_and(jnp.logical_not(used), f == 0))
    def _():
        o_ref[...] = jnp.zeros_like(o_ref)


def _ffn_moe(tile_e, n_used, xs, wg, wu, wd):
    tm, tf = TM_MOE, TF_MOE
    nf = FF_EXPERT // tf

    def fblk(i, f, nu):
        return jnp.where(i < nu[0], f, nf - 1)

    grid_spec = pltpu.PrefetchScalarGridSpec(
        num_scalar_prefetch=2,
        grid=(N_MOE_TILES, nf),
        in_specs=[
            pl.BlockSpec((tm, D), lambda i, f, te, nu: (i, 0)),
            pl.BlockSpec((None, D, tf), lambda i, f, te, nu: (te[i], 0, fblk(i, f, nu))),
            pl.BlockSpec((None, D, tf), lambda i, f, te, nu: (te[i], 0, fblk(i, f, nu))),
            pl.BlockSpec((None, tf, D), lambda i, f, te, nu: (te[i], fblk(i, f, nu), 0)),
        ],
        out_specs=pl.BlockSpec((tm, D), lambda i, f, te, nu: (i, 0)),
        scratch_shapes=[pltpu.VMEM((tm, D), F32)],
    )
    return pl.pallas_call(
        _ffn_moe_kernel,
        grid_spec=grid_spec,
        out_shape=jax.ShapeDtypeStruct((P_MOE, D), F32),
        compiler_params=_cp(("arbitrary", "arbitrary"), 58),
        name="ffn_moe",
    )(tile_e, n_used, xs, wg, wu, wd)


def _combine_kernel(pos_ref, y_ref, meta_ref, h_ref, g_ref, b_ref, hf_ref, hb_ref, buf1, buf2, sem):
    i = pl.program_id(0)
    tm = TOK_TILE

    def copy(t, k):
        src = pos_ref[0, 0, 2 * t + k]
        buf = buf1 if k == 0 else buf2
        return pltpu.make_async_copy(y_ref.at[pl.ds(src, 1)], buf.at[pl.ds(t, 1)], sem)

    def start(t, carry):
        copy(t, 0).start()
        copy(t, 1).start()
        return carry

    def wait(t, carry):
        copy(t, 0).wait()
        copy(t, 1).wait()
        return carry

    lax.fori_loop(0, tm, start, 0, unroll=8)
    lax.fori_loop(0, tm, wait, 0, unroll=8)

    meta = meta_ref[...]
    w1 = meta[:, 2:3]
    w2 = meta[:, 3:4]
    f = w1 * buf1[...] + w2 * buf2[...]
    y = _layer_norm(ALPHA * h_ref[...] + f, g_ref[...], b_ref[...])
    y = jnp.where(_valid_rows(i, tm), y, 0.0)
    hf_ref[...] = y
    hb_ref[...] = y.astype(BF16)


def _combine(pos, y, meta, h_f, g, b):
    tm = TOK_TILE
    return pl.pallas_call(
        _combine_kernel,
        grid=(TP // tm,),
        in_specs=[
            pl.BlockSpec((1, 1, 2 * tm), lambda i: (i, 0, 0), memory_space=pltpu.SMEM),
            pl.BlockSpec(memory_space=pl.ANY),
            pl.BlockSpec((tm, LANES), lambda i: (i, 0)),
            pl.BlockSpec((tm, D), lambda i: (i, 0)),
            pl.BlockSpec((1, D), lambda i: (0, 0)),
            pl.BlockSpec((1, D), lambda i: (0, 0)),
        ],
        out_specs=[
            pl.BlockSpec((tm, D), lambda i: (i, 0)),
            pl.BlockSpec((tm, D), lambda i: (i, 0)),
        ],
        out_shape=[jax.ShapeDtypeStruct((TP, D), F32), jax.ShapeDtypeStruct((TP, D), BF16)],
        scratch_shapes=[pltpu.VMEM((tm, D), F32), pltpu.VMEM((tm, D), F32), pltpu.SemaphoreType.DMA(())],
        compiler_params=_cp(("arbitrary",), 32),
        name="moe_combine",
    )(pos, y, meta, h_f, g, b)


def _pad_lanes(v, width=LANES):
    return jnp.pad(v, ((0, 0), (0, width - v.shape[1])))


def _token_mixer(h_f, h_bf, w_in, b_gate, conv_a_w, w_out_a, ssd_conv_w, ssd_conv_b, dt_bias, a_log,
                 d_skip, ssd_norm_w, w_out_b, w_o, ln_g, ln_b):
    dt0 = G0
    w_main = jnp.concatenate([w_in[:, :dt0], w_in[:, dt0 + HEADS:]], axis=1).astype(BF16)
    w_dt = _pad_lanes(w_in[:, dt0:dt0 + HEADS]).astype(BF16)
    proj = _in_proj(h_bf, w_main)
    dt, cum = _dt_proj(h_bf, w_dt, _pad_lanes(dt_bias[None, :]), _pad_lanes(a_log[None, :]))
    ga_ya = _mixer_a(proj, conv_a_w, w_out_a.astype(BF16), b_gate[None, :D])
    dskip = jnp.repeat(d_skip, HEADDIM)[None, :]
    yn = _ssd(proj, dt, cum, ssd_conv_w, ssd_conv_b[None, :], dskip, ssd_norm_w[None, :])
    m = _out_b(yn, w_out_b.astype(BF16), proj, b_gate[None, D:], ga_ya)
    return _wo_ln(m, w_o.astype(BF16), h_f, ln_g[None, :], ln_b[None, :])


def _moe_layer(h_f, router_w, wg, wu, wd, ln_g, ln_b):
    meta, cnt = _router(h_f, _pad_lanes(router_w))
    i1 = meta[:, 0].astype(jnp.int32)
    i2 = meta[:, 1].astype(jnp.int32)
    counts = cnt[0, :N_EXPERTS].astype(jnp.int32)
    padded = ((counts + TM_MOE - 1) // TM_MOE) * TM_MOE
    ends = jnp.cumsum(padded)
    starts = ends - padded
    pos = jnp.stack([starts[i1] + meta[:, 4].astype(jnp.int32),
                     starts[i2] + meta[:, 5].astype(jnp.int32)], axis=1)
    pos = jnp.clip(pos, 0, P_MOE - 1)
    tok = jnp.arange(TP, dtype=jnp.int32)
    valid = (tok % LP) >= PADF
    dst = jnp.where(valid[:, None], pos, P_MOE).reshape(-1)
    src = jnp.zeros((P_MOE,), jnp.int32).at[dst].set(jnp.repeat(tok, 2), mode="drop")
    src = src.reshape(N_MOE_TILES, 1, TM_MOE)
    pos = pos.reshape(TP // TOK_TILE, 1, 2 * TOK_TILE)
    n_used = (ends[-1] // TM_MOE).reshape(1).astype(jnp.int32)
    tile_start = jnp.arange(N_MOE_TILES, dtype=jnp.int32) * TM_MOE
    tile_e = jnp.sum((tile_start[:, None] >= ends[None, :]).astype(jnp.int32), axis=1)
    last_e = jnp.minimum(tile_e[jnp.maximum(n_used[0] - 1, 0)], N_EXPERTS - 1)
    tile_e = jnp.where(tile_start < ends[-1], jnp.minimum(tile_e, N_EXPERTS - 1), last_e)
    xs = _gather_rows(n_used, src, h_f)
    y = _ffn_moe(tile_e, n_used, xs, wg.astype(BF16), wu.astype(BF16), wd.astype(BF16))
    return _combine(pos, y, meta, h_f, ln_g[None, :], ln_b[None, :])


def kernel(x, meta_tokens, ln_in_g, ln_in_b, w_in, b_gate, conv_a_w, w_out_a, ssd_conv_w, ssd_conv_b,
           dt_bias, a_log, d_skip, ssd_norm_w, w_out_b, w_o, ln1_g, ln1_b, ffn_w_gate, ffn_w_up,
           ffn_w_down, router, moe_w_gate, moe_w_up, moe_w_down, ln2_g, ln2_b):
    h_f, h_bf = _ln_in(x, meta_tokens.astype(x.dtype), ln_in_g[None, :], ln_in_b[None, :])
    for i in range(DEPTH):
        h_f, h_bf = _token_mixer(h_f, h_bf, w_in[i], b_gate[i], conv_a_w[i], w_out_a[i], ssd_conv_w[i],
                                 ssd_conv_b[i], dt_bias[i], a_log[i], d_skip[i], ssd_norm_w[i],
                                 w_out_b[i], w_o[i], ln1_g[i], ln1_b[i])
        j = i // 2
        if i % 2 == 0:
            h_f, h_bf = _ffn_dense(h_bf, h_f, ffn_w_gate[j].astype(BF16), ffn_w_up[j].astype(BF16),
                                   ffn_w_down[j].astype(BF16), ln2_g[i][None, :], ln2_b[i][None, :])
        else:
            h_f, h_bf = _moe_layer(h_f, router[j], moe_w_gate[j], moe_w_up[j], moe_w_down[j],
                                   ln2_g[i], ln2_b[i])
    return h_f.reshape(BATCH, LP, D)[:, PADF + N_META:, :]
```

```python
import functools

import jax
import jax.numpy as jnp
from jax import lax
from jax.experimental import pallas as pl
from jax.experimental.pallas import tpu as pltpu

F32 = jnp.float32
BF16 = jnp.bfloat16

D = 2048
BATCH = 4
SEQ = 4096
N_META = 16
DEPTH = 2
D_INNER = 2 * D
HEADDIM = 64
HEADS = D_INNER // HEADDIM
GROUPS = 8
HPG = HEADS // GROUPS
STATE = 128
GW = D_INNER // GROUPS
XBC = D_INNER + 2 * GROUPS * STATE
FF_DENSE = 5632
N_EXPERTS = 8
FF_EXPERT = 7168
ALPHA = (2.0 * DEPTH) ** 0.25
LN_EPS = 1e-5
RMS_EPS = 1e-5
SSD_CONV_K = 4

CH = 256
PADF = CH - N_META
LP = PADF + N_META + SEQ
NCH = LP // CH
TP = BATCH * LP

C0, B0, V0 = 0, D, 2 * D
Z0 = 3 * D
X0 = Z0 + D_INNER
G0 = X0 + XBC
NP = G0 + 2 * D

LANES = 128
SUBLANES = 8
BF16_SUBLANES = 16

TM_MOE = 512
N_MOE_TILES = (2 * BATCH * (SEQ + N_META)) // TM_MOE + N_EXPERTS
P_MOE = N_MOE_TILES * TM_MOE
TF_MOE = 1024
TM_FFN = 512
TF_FFN = 512
TOK_TILE = CH

SUB = 128
LOG2E = 1.4426950408889634
EXP2_CLAMP = 100.0

MIB = 1024 * 1024


def _cp(sem, vmem_mib):
    return pltpu.CompilerParams(dimension_semantics=sem, vmem_limit_bytes=int(vmem_mib * MIB))


def _sigmoid(x):
    return 0.5 * jnp.tanh(0.5 * x) + 0.5


def _silu(x):
    return x * _sigmoid(x)


def _softplus(x):
    return jnp.maximum(x, 0.0) + jnp.log1p(jnp.exp(-jnp.abs(x)))


def _layer_norm(x, g, b):
    mu = jnp.mean(x, axis=-1, keepdims=True)
    xc = x - mu
    var = jnp.mean(xc * xc, axis=-1, keepdims=True)
    return xc * lax.rsqrt(var + LN_EPS) * g + b


def _valid_rows(tile_idx, tm):
    row = lax.broadcasted_iota(jnp.int32, (tm, 1), 0)
    valid = None
    for s in range(tm // CH):
        chunk = tile_idx * (tm // CH) + s
        hi = jnp.where(chunk % NCH == 0, s * CH + PADF, s * CH)
        ok = jnp.logical_or(row < s * CH, row >= hi)
        valid = ok if valid is None else jnp.logical_and(valid, ok)
    return valid


def _ln_in_kernel(x_ref, meta_ref, g_ref, b_ref, hf_ref, hb_ref):
    j = pl.program_id(1)

    @pl.when(j == 0)
    def _():
        hf_ref[...] = jnp.zeros_like(hf_ref)
        hb_ref[...] = jnp.zeros_like(hb_ref)
        y = _layer_norm(meta_ref[...], g_ref[...], b_ref[...])
        hf_ref[PADF:, :] = y
        hb_ref[PADF:, :] = y.astype(BF16)

    @pl.when(j > 0)
    def _():
        y = _layer_norm(x_ref[0], g_ref[...], b_ref[...])
        hf_ref[...] = y
        hb_ref[...] = y.astype(BF16)


def _ln_in(x, meta, g, b):
    return pl.pallas_call(
        _ln_in_kernel,
        grid=(BATCH, NCH),
        in_specs=[
            pl.BlockSpec((1, CH, D), lambda bi, j: (bi, jnp.maximum(j - 1, 0), 0)),
            pl.BlockSpec((N_META, D), lambda bi, j: (0, 0)),
            pl.BlockSpec((1, D), lambda bi, j: (0, 0)),
            pl.BlockSpec((1, D), lambda bi, j: (0, 0)),
        ],
        out_specs=[
            pl.BlockSpec((CH, D), lambda bi, j: (bi * NCH + j, 0)),
            pl.BlockSpec((CH, D), lambda bi, j: (bi * NCH + j, 0)),
        ],
        out_shape=[jax.ShapeDtypeStruct((TP, D), F32), jax.ShapeDtypeStruct((TP, D), BF16)],
        compiler_params=_cp(("arbitrary", "arbitrary"), 32),
        name="ln_in",
    )(x, meta, g, b)


def _mm_kernel(a_ref, w_ref, o_ref):
    o_ref[...] = jnp.dot(a_ref[...], w_ref[...], preferred_element_type=F32).astype(o_ref.dtype)


def _in_proj(h_bf, w_main):
    tm, tn = 1024, 2048
    return pl.pallas_call(
        _mm_kernel,
        grid=(NP // tn, TP // tm),
        in_specs=[
            pl.BlockSpec((tm, D), lambda j, i: (i, 0)),
            pl.BlockSpec((D, tn), lambda j, i: (0, j)),
        ],
        out_specs=pl.BlockSpec((tm, tn), lambda j, i: (i, j)),
        out_shape=jax.ShapeDtypeStruct((TP, NP), BF16),
        compiler_params=_cp(("arbitrary", "arbitrary"), 48),
        name="in_proj",
    )(h_bf, w_main)


def _dt_kernel(a_ref, w_ref, bias_ref, alog_ref, dt_ref, cum_ref):
    i = pl.program_id(0)
    raw = jnp.dot(a_ref[...], w_ref[...], preferred_element_type=F32) + bias_ref[...]
    dt = _softplus(raw)
    a_neg = -jnp.exp(alog_ref[...])
    row = lax.broadcasted_iota(jnp.int32, (CH, 1), 0)
    first = jnp.where(i % NCH == 0, PADF, 0)
    da = jnp.where(row >= first, dt * a_neg, 0.0)
    r = lax.broadcasted_iota(jnp.int32, (CH, CH), 0)
    c = lax.broadcasted_iota(jnp.int32, (CH, CH), 1)
    tril = (r >= c).astype(F32)
    cum = jnp.dot(tril, da, precision=lax.Precision.HIGHEST, preferred_element_type=F32)
    dt_ref[...] = dt
    cum_ref[...] = cum


def _dt_proj(h_bf, w_dt, dt_bias, a_log):
    return pl.pallas_call(
        _dt_kernel,
        grid=(TP // CH,),
        in_specs=[
            pl.BlockSpec((CH, D), lambda i: (i, 0)),
            pl.BlockSpec((D, LANES), lambda i: (0, 0)),
            pl.BlockSpec((1, LANES), lambda i: (0, 0)),
            pl.BlockSpec((1, LANES), lambda i: (0, 0)),
        ],
        out_specs=[
            pl.BlockSpec((CH, LANES), lambda i: (i, 0)),
            pl.BlockSpec((CH, LANES), lambda i: (i, 0)),
        ],
        out_shape=[jax.ShapeDtypeStruct((TP, LANES), F32), jax.ShapeDtypeStruct((TP, LANES), F32)],
        compiler_params=_cp(("arbitrary",), 16),
        name="dt_proj",
    )(h_bf, w_dt, dt_bias, a_log)


def _causal_conv(x, halo, w, k):
    acc = w[k - 1:k, :] * x
    for s in range(1, k):
        acc = acc + w[k - 1 - s:k - s, :] * pltpu.roll(x, s, 0)
    x8 = x[0:SUBLANES, :]
    h8 = halo[BF16_SUBLANES - SUBLANES:, :]
    row = lax.broadcasted_iota(jnp.int32, (SUBLANES, 1), 0)
    top = w[k - 1:k, :] * x8
    for s in range(1, k):
        shifted = jnp.where(row < s, pltpu.roll(h8, s, 0), pltpu.roll(x8, s, 0))
        top = top + w[k - 1 - s:k - s, :] * shifted
    return jnp.concatenate([top, acc[SUBLANES:, :]], axis=0)


def _mixa_kernel(c_ref, b_ref, v_ref, ch_ref, vh_ref, g_ref, cw_ref, wo_ref, bg_ref, o_ref):
    u = c_ref[...].astype(F32) * v_ref[...].astype(F32)
    hu = ch_ref[...].astype(F32) * vh_ref[...].astype(F32)
    conv = _causal_conv(u, hu, cw_ref[...], 3)
    lhs = (b_ref[...].astype(F32) * conv).astype(BF16)
    ya = jnp.dot(lhs, wo_ref[...], preferred_element_type=F32)
    ga = _sigmoid(g_ref[...].astype(F32) + bg_ref[...])
    o_ref[...] = (ga * ya).astype(BF16)


def _mixer_a(proj, conv_w, w_out_a, bgate_a):
    tm = 512
    hb = tm // BF16_SUBLANES
    halo = lambda col: (lambda i: (jnp.maximum(i * hb - 1, 0), col))
    return pl.pallas_call(
        _mixa_kernel,
        grid=(TP // tm,),
        in_specs=[
            pl.BlockSpec((tm, D), lambda i: (i, C0 // D)),
            pl.BlockSpec((tm, D), lambda i: (i, B0 // D)),
            pl.BlockSpec((tm, D), lambda i: (i, V0 // D)),
            pl.BlockSpec((BF16_SUBLANES, D), halo(C0 // D)),
            pl.BlockSpec((BF16_SUBLANES, D), halo(V0 // D)),
            pl.BlockSpec((tm, D), lambda i: (i, G0 // D)),
            pl.BlockSpec((3, D), lambda i: (0, 0)),
            pl.BlockSpec((D, D), lambda i: (0, 0), pipeline_mode=pl.Buffered(1)),
            pl.BlockSpec((1, D), lambda i: (0, 0)),
        ],
        out_specs=pl.BlockSpec((tm, D), lambda i: (i, 0)),
        out_shape=jax.ShapeDtypeStruct((TP, D), BF16),
        compiler_params=_cp(("arbitrary",), 56),
        name="mixer_a",
    )(proj, proj, proj, proj, proj, proj, conv_w, w_out_a, bgate_a)


def _ssd_kernel(xs_ref, bm_ref, cm_ref, z_ref, dt_ref, cum_ref,
                wx_ref, wb_ref, wc_ref, bx_ref, bb_ref, bc_ref, dsk_ref, nw_ref,
                o_ref, st_ref):
    g = pl.program_id(1)
    st_ref[...] = jnp.zeros_like(st_ref)

    rr = lax.broadcasted_iota(jnp.int32, (CH, CH), 0)
    cc = lax.broadcasted_iota(jnp.int32, (CH, CH), 1)
    shift_mats = [(rr - cc == s).astype(BF16) for s in range(1, SSD_CONV_K)]
    w_all = jnp.concatenate([wx_ref[...], wb_ref[...], wc_ref[...]], axis=1)
    b_all = jnp.concatenate([bx_ref[...], bb_ref[...], bc_ref[...]], axis=1)
    taps = [w_all[k:k + 1, :] for k in range(SSD_CONV_K)]
    lane_shift = (LANES - HPG * g) % LANES
    row = lax.broadcasted_iota(jnp.int32, (CH, 1), 0)
    row8 = lax.broadcasted_iota(jnp.int32, (SUBLANES, 1), 0)
    r2 = lax.broadcasted_iota(jnp.int32, (SUB, SUB), 0)
    c2 = lax.broadcasted_iota(jnp.int32, (SUB, SUB), 1)
    causal = r2 >= c2
    lane = lax.broadcasted_iota(jnp.int32, (1, LANES), 1)
    lo = lane < HEADDIM
    dskip = dsk_ref[...]
    norm_w = nw_ref[...]

    def chunk(c, carry):
        r0 = pl.multiple_of(c * CH, CH)
        h0 = pl.multiple_of(jnp.maximum(r0 - BF16_SUBLANES, 0), BF16_SUBLANES)
        rows = pl.ds(r0, CH)
        above = pl.ds(h0, BF16_SUBLANES)
        raw = jnp.concatenate([xs_ref[rows, :], bm_ref[rows, :], cm_ref[rows, :]], axis=1)
        halo = jnp.concatenate([xs_ref[above, :], bm_ref[above, :], cm_ref[above, :]], axis=1)
        h8 = halo[BF16_SUBLANES - SUBLANES:, :].astype(F32)

        acc = taps[SSD_CONV_K - 1] * raw.astype(F32) + b_all
        fix = jnp.zeros((SUBLANES, raw.shape[1]), F32)
        for s in range(1, SSD_CONV_K):
            tap = taps[SSD_CONV_K - 1 - s]
            acc = acc + tap * jnp.dot(shift_mats[s - 1], raw, preferred_element_type=F32)
            fix = fix + jnp.where(row8 < s, tap * pltpu.roll(h8, s, 0), 0.0)
        act = _silu(jnp.concatenate([acc[0:SUBLANES, :] + fix, acc[SUBLANES:, :]], axis=0))
        first = jnp.where(c == 0, PADF, 0)
        xs = jnp.where(row >= first, act[:, :GW], 0.0)
        bm = act[:, GW:GW + STATE]
        cm = act[:, GW + STATE:]

        dtg = pltpu.roll(dt_ref[rows, :], lane_shift, 1)
        cumg = pltpu.roll(cum_ref[rows, :], lane_shift, 1) * LOG2E
        bm_bf = bm.astype(BF16)
        cm_bf = cm.astype(BF16)
        xs_bf = xs.astype(BF16)

        st = st_ref[...]
        y_rows = []
        for q in range(CH // SUB):
            rs = slice(q * SUB, (q + 1) * SUB)
            cq = cumg[rs, :]
            if q > 0:
                cq = cq - cumg[q * SUB - 1:q * SUB, :]
            dq = dtg[rs, :]
            last = cq[SUB - 1:SUB, :]
            ecum = jnp.exp2(cq)
            wts = jnp.exp2(last - cq) * dq
            elast = jnp.exp2(last)
            src_t = (cq - jnp.log2(dq)).T

            cb = lax.dot_general(cm_bf[rs, :], bm_bf[rs, :], (((1,), (1,)), ((), ())),
                                 preferred_element_type=F32)
            cbm = jnp.where(causal, cb, 0.0)
            y_off = jnp.dot(cm_bf[rs, :], st.astype(BF16), preferred_element_type=F32)

            def head_mat(j):
                seg = cq[:, j:j + 1] - src_t[j:j + 1, :]
                return (cbm * jnp.exp2(jnp.minimum(seg, EXP2_CLAMP))).astype(BF16)

            ys, xws, els = [], [], []
            for p in range(GW // LANES):
                ja, jb = 2 * p, 2 * p + 1
                sl = slice(p * LANES, (p + 1) * LANES)
                xp = xs_bf[rs, sl]
                ya = jnp.dot(head_mat(ja), xp, preferred_element_type=F32)
                yb = jnp.dot(head_mat(jb), xp, preferred_element_type=F32)
                e_p = jnp.where(lo, ecum[:, ja:ja + 1], ecum[:, jb:jb + 1])
                ys.append(jnp.where(lo, ya, yb) + e_p * y_off[:, sl])
                w_p = jnp.where(lo, wts[:, ja:ja + 1], wts[:, jb:jb + 1])
                xws.append((xs[rs, sl] * w_p).astype(BF16))
                els.append(jnp.where(lo, elast[:, ja:ja + 1], elast[:, jb:jb + 1]))
            y_rows.append(jnp.concatenate(ys, axis=1))
            xw = jnp.concatenate(xws, axis=1)
            el = jnp.concatenate(els, axis=1)
            bm_t = bm[rs, :].T.astype(BF16)
            st = st * el + jnp.dot(bm_t, xw, preferred_element_type=F32)
        st_ref[...] = st
        y = jnp.concatenate(y_rows, axis=0)

        y = y + xs * dskip
        yg = y * _silu(z_ref[rows, :].astype(F32))
        ms = jnp.mean(yg * yg, axis=-1, keepdims=True)
        o_ref[rows, :] = (yg * lax.rsqrt(ms + RMS_EPS) * norm_w).astype(BF16)
        return carry

    lax.fori_loop(0, NCH, chunk, 0)


def _ssd(proj, dt, cum, conv_w, conv_b, dskip, norm_w):
    xcol = lambda g: X0 // GW + g
    bcol = lambda g: (X0 + D_INNER) // STATE + g
    ccol = lambda g: (X0 + D_INNER + GROUPS * STATE) // STATE + g
    zcol = lambda g: Z0 // GW + g
    wbcol = lambda g: D_INNER // STATE + g
    wccol = lambda g: (D_INNER + GROUPS * STATE) // STATE + g
    rows = lambda col_fn: (lambda b, g: (b, col_fn(g)))
    par = lambda col_fn: (lambda b, g: (0, col_fn(g)))
    return pl.pallas_call(
        _ssd_kernel,
        grid=(BATCH, GROUPS),
        in_specs=[
            pl.BlockSpec((LP, GW), rows(xcol)),
            pl.BlockSpec((LP, STATE), rows(bcol)),
            pl.BlockSpec((LP, STATE), rows(ccol)),
            pl.BlockSpec((LP, GW), rows(zcol)),
            pl.BlockSpec((LP, LANES), lambda b, g: (b, 0)),
            pl.BlockSpec((LP, LANES), lambda b, g: (b, 0)),
            pl.BlockSpec((SSD_CONV_K, GW), par(lambda g: g)),
            pl.BlockSpec((SSD_CONV_K, STATE), par(wbcol)),
            pl.BlockSpec((SSD_CONV_K, STATE), par(wccol)),
            pl.BlockSpec((1, GW), par(lambda g: g)),
            pl.BlockSpec((1, STATE), par(wbcol)),
            pl.BlockSpec((1, STATE), par(wccol)),
            pl.BlockSpec((1, GW), par(lambda g: g)),
            pl.BlockSpec((1, GW), par(lambda g: g)),
        ],
        out_specs=pl.BlockSpec((LP, GW), lambda b, g: (b, g)),
        out_shape=jax.ShapeDtypeStruct((TP, D_INNER), BF16),
        scratch_shapes=[pltpu.VMEM((STATE, GW), F32)],
        compiler_params=_cp(("arbitrary", "arbitrary"), 56),
        name="ssd",
    )(proj, proj, proj, proj, dt, cum,
      conv_w, conv_w, conv_w, conv_b, conv_b, conv_b, dskip, norm_w)


def _outb_kernel(y_ref, w_ref, g_ref, bg_ref, ya_ref, o_ref):
    yb = jnp.dot(y_ref[...], w_ref[...], preferred_element_type=F32)
    gb = _sigmoid(g_ref[...].astype(F32) + bg_ref[...])
    o_ref[...] = (gb * yb + ya_ref[...].astype(F32)).astype(BF16)


def _out_b(yn, w_out_b, proj, bgate_b, ga_ya):
    tm = 512
    return pl.pallas_call(
        _outb_kernel,
        grid=(TP // tm,),
        in_specs=[
            pl.BlockSpec((tm, D_INNER), lambda i: (i, 0)),
            pl.BlockSpec((D_INNER, D), lambda i: (0, 0), pipeline_mode=pl.Buffered(1)),
            pl.BlockSpec((tm, D), lambda i: (i, G0 // D + 1)),
            pl.BlockSpec((1, D), lambda i: (0, 0)),
            pl.BlockSpec((tm, D), lambda i: (i, 0)),
        ],
        out_specs=pl.BlockSpec((tm, D), lambda i: (i, 0)),
        out_shape=jax.ShapeDtypeStruct((TP, D), BF16),
        compiler_params=_cp(("arbitrary",), 56),
        name="out_b",
    )(yn, w_out_b, proj, bgate_b, ga_ya)


def _wo_kernel(m_ref, w_ref, h_ref, g_ref, b_ref, hf_ref, hb_ref):
    mix = jnp.dot(m_ref[...], w_ref[...], preferred_element_type=F32)
    y = _layer_norm(ALPHA * h_ref[...] + mix, g_ref[...], b_ref[...])
    hf_ref[...] = y
    hb_ref[...] = y.astype(BF16)


def _wo_ln(m, w_o, h, g, b):
    tm = 512
    return pl.pallas_call(
        _wo_kernel,
        grid=(TP // tm,),
        in_specs=[
            pl.BlockSpec((tm, D), lambda i: (i, 0)),
            pl.BlockSpec((D, D), lambda i: (0, 0), pipeline_mode=pl.Buffered(1)),
            pl.BlockSpec((tm, D), lambda i: (i, 0)),
            pl.BlockSpec((1, D), lambda i: (0, 0)),
            pl.BlockSpec((1, D), lambda i: (0, 0)),
        ],
        out_specs=[
            pl.BlockSpec((tm, D), lambda i: (i, 0)),
            pl.BlockSpec((tm, D), lambda i: (i, 0)),
        ],
        out_shape=[jax.ShapeDtypeStruct((TP, D), F32), jax.ShapeDtypeStruct((TP, D), BF16)],
        compiler_params=_cp(("arbitrary",), 56),
        name="wo_ln",
    )(m, w_o, h, g, b)


def _swiglu_step(x_bf, wg_ref, wu_ref, wd_ref):
    gate = jnp.dot(x_bf, wg_ref[...], preferred_element_type=F32)
    up = jnp.dot(x_bf, wu_ref[...], preferred_element_type=F32)
    mid = (_silu(gate) * up).astype(BF16)
    return jnp.dot(mid, wd_ref[...], preferred_element_type=F32)


def _ffn_dense_kernel(x_ref, wg_ref, wu_ref, wd_ref, h_ref, g_ref, b_ref, hf_ref, hb_ref, acc_ref):
    i = pl.program_id(0)
    f = pl.program_id(1)

    @pl.when(f == 0)
    def _():
        acc_ref[...] = jnp.zeros_like(acc_ref)

    acc_ref[...] += _swiglu_step(x_ref[...], wg_ref, wu_ref, wd_ref)

    @pl.when(f == pl.num_programs(1) - 1)
    def _():
        y = _layer_norm(ALPHA * h_ref[...] + acc_ref[...], g_ref[...], b_ref[...])
        y = jnp.where(_valid_rows(i, TM_FFN), y, 0.0)
        hf_ref[...] = y
        hb_ref[...] = y.astype(BF16)


def _ffn_dense(h_bf, h_f, wg, wu, wd, g, b):
    tm, tf = TM_FFN, TF_FFN
    return pl.pallas_call(
        _ffn_dense_kernel,
        grid=(TP // tm, FF_DENSE // tf),
        in_specs=[
            pl.BlockSpec((tm, D), lambda i, f: (i, 0)),
            pl.BlockSpec((D, tf), lambda i, f: (0, f)),
            pl.BlockSpec((D, tf), lambda i, f: (0, f)),
            pl.BlockSpec((tf, D), lambda i, f: (f, 0)),
            pl.BlockSpec((tm, D), lambda i, f: (i, 0)),
            pl.BlockSpec((1, D), lambda i, f: (0, 0)),
            pl.BlockSpec((1, D), lambda i, f: (0, 0)),
        ],
        out_specs=[
            pl.BlockSpec((tm, D), lambda i, f: (i, 0)),
            pl.BlockSpec((tm, D), lambda i, f: (i, 0)),
        ],
        out_shape=[jax.ShapeDtypeStruct((TP, D), F32), jax.ShapeDtypeStruct((TP, D), BF16)],
        scratch_shapes=[pltpu.VMEM((tm, D), F32)],
        compiler_params=_cp(("arbitrary", "arbitrary"), 56),
        name="ffn_dense",
    )(h_bf, wg, wu, wd, h_f, g, b)


def _router_kernel(h_ref, r_ref, meta_ref, cnt_ref, carry_ref):
    i = pl.program_id(0)
    tm = TOK_TILE

    @pl.when(i == 0)
    def _():
        carry_ref[...] = jnp.zeros_like(carry_ref)

    logits = jnp.dot(h_ref[...], r_ref[...], precision=lax.Precision.HIGHEST,
                     preferred_element_type=F32)
    lane = lax.broadcasted_iota(jnp.int32, (tm, LANES), 1).astype(F32)
    neg = jnp.float32(-jnp.inf)
    logits = jnp.where(lane < N_EXPERTS, logits, neg)
    m1 = jnp.max(logits, axis=-1, keepdims=True)
    i1 = jnp.min(jnp.where(logits == m1, lane, float(LANES)), axis=-1, keepdims=True)
    rest = jnp.where(lane == i1, neg, logits)
    m2 = jnp.max(rest, axis=-1, keepdims=True)
    i2 = jnp.min(jnp.where(rest == m2, lane, float(LANES)), axis=-1, keepdims=True)
    e = jnp.exp(m2 - m1)
    w1 = 1.0 / (1.0 + e)
    w2 = e / (1.0 + e)

    valid = _valid_rows(i, tm)
    sel = jnp.logical_and(jnp.logical_or(lane == i1, lane == i2), valid)
    onehot = sel.astype(F32)
    r = lax.broadcasted_iota(jnp.int32, (tm, tm), 0)
    c = lax.broadcasted_iota(jnp.int32, (tm, tm), 1)
    strict = (r > c).astype(BF16)
    before = jnp.dot(strict, onehot.astype(BF16), preferred_element_type=F32) + carry_ref[0:1, :]
    rank1 = jnp.sum(jnp.where(lane == i1, before, 0.0), axis=-1, keepdims=True)
    rank2 = jnp.sum(jnp.where(lane == i2, before, 0.0), axis=-1, keepdims=True)
    total = carry_ref[0:1, :] + jnp.sum(onehot, axis=0, keepdims=True)
    carry_ref[...] = jnp.broadcast_to(total, carry_ref.shape)
    cnt_ref[...] = jnp.broadcast_to(total, cnt_ref.shape)

    meta = jnp.where(lane == 0, i1, 0.0)
    meta = jnp.where(lane == 1, i2, meta)
    meta = jnp.where(lane == 2, w1, meta)
    meta = jnp.where(lane == 3, w2, meta)
    meta = jnp.where(lane == 4, rank1, meta)
    meta = jnp.where(lane == 5, rank2, meta)
    meta_ref[...] = meta


def _router(h_f, router_w):
    tm = TOK_TILE
    return pl.pallas_call(
        _router_kernel,
        grid=(TP // tm,),
        in_specs=[
            pl.BlockSpec((tm, D), lambda i: (i, 0)),
            pl.BlockSpec((D, LANES), lambda i: (0, 0)),
        ],
        out_specs=[
            pl.BlockSpec((tm, LANES), lambda i: (i, 0)),
            pl.BlockSpec((8, LANES), lambda i: (0, 0)),
        ],
        out_shape=[jax.ShapeDtypeStruct((TP, LANES), F32), jax.ShapeDtypeStruct((8, LANES), F32)],
        scratch_shapes=[pltpu.VMEM((8, LANES), F32)],
        compiler_params=_cp(("arbitrary",), 32),
        name="router",
    )(h_f, router_w)


def _gather_kernel(nu_ref, src_ref, h_ref, o_ref, buf, sem):
    i = pl.program_id(0)
    used = i < nu_ref[0]

    def copy(t):
        return pltpu.make_async_copy(h_ref.at[pl.ds(src_ref[0, 0, t], 1)], buf.at[pl.ds(t, 1)], sem)

    def start(t, carry):
        copy(t).start()
        return carry

    def wait(t, carry):
        copy(t).wait()
        return carry

    @pl.when(used)
    def _():
        lax.fori_loop(0, TM_MOE, start, 0, unroll=8)
        lax.fori_loop(0, TM_MOE, wait, 0, unroll=8)
        o_ref[...] = buf[...].astype(BF16)

    @pl.when(jnp.logical_not(used))
    def _():
        o_ref[...] = jnp.zeros_like(o_ref)


def _gather_rows(n_used, src, h_f):
    grid_spec = pltpu.PrefetchScalarGridSpec(
        num_scalar_prefetch=1,
        grid=(N_MOE_TILES,),
        in_specs=[
            pl.BlockSpec((1, 1, TM_MOE), lambda i, nu: (i, 0, 0), memory_space=pltpu.SMEM),
            pl.BlockSpec(memory_space=pl.ANY),
        ],
        out_specs=pl.BlockSpec((TM_MOE, D), lambda i, nu: (i, 0)),
        scratch_shapes=[pltpu.VMEM((TM_MOE, D), F32), pltpu.SemaphoreType.DMA(())],
    )
    return pl.pallas_call(
        _gather_kernel,
        grid_spec=grid_spec,
        out_shape=jax.ShapeDtypeStruct((P_MOE, D), BF16),
        compiler_params=_cp(("arbitrary",), 32),
        name="moe_gather",
    )(n_used, src, h_f)


def _ffn_moe_kernel(te_ref, nu_ref, x_ref, wg_ref, wu_ref, wd_ref, o_ref, acc_ref):
    del te_ref
    i = pl.program_id(0)
    f = pl.program_id(1)
    used = i < nu_ref[0]

    @pl.when(jnp.logical_and(used, f == 0))
    def _():
        acc_ref[...] = jnp.zeros_like(acc_ref)

    @pl.when(used)
    def _():
        acc_ref[...] += _swiglu_step(x_ref[...], wg_ref, wu_ref, wd_ref)

    @pl.when(jnp.logical_and(used, f == pl.num_programs(1) - 1))
    def _():
        o_ref[...] = acc_ref[...]

    @pl.when(jnp.logical_and(jnp.logical_not(used), f == 0))
    def _():
        o_ref[...] = jnp.zeros_like(o_ref)


def _ffn_moe(tile_e, n_used, xs, wg, wu, wd):
    tm, tf = TM_MOE, TF_MOE
    nf = FF_EXPERT // tf

    def fblk(i, f, nu):
        return jnp.where(i < nu[0], f, nf - 1)

    grid_spec = pltpu.PrefetchScalarGridSpec(
        num_scalar_prefetch=2,
        grid=(N_MOE_TILES, nf),
        in_specs=[
            pl.BlockSpec((tm, D), lambda i, f, te, nu: (i, 0)),
            pl.BlockSpec((None, D, tf), lambda i, f, te, nu: (te[i], 0, fblk(i, f, nu))),
            pl.BlockSpec((None, D, tf), lambda i, f, te, nu: (te[i], 0, fblk(i, f, nu))),
            pl.BlockSpec((None, tf, D), lambda i, f, te, nu: (te[i], fblk(i, f, nu), 0)),
        ],
        out_specs=pl.BlockSpec((tm, D), lambda i, f, te, nu: (i, 0)),
        scratch_shapes=[pltpu.VMEM((tm, D), F32)],
    )
    return pl.pallas_call(
        _ffn_moe_kernel,
        grid_spec=grid_spec,
        out_shape=jax.ShapeDtypeStruct((P_MOE, D), F32),
        compiler_params=_cp(("arbitrary", "arbitrary"), 58),
        name="ffn_moe",
    )(tile_e, n_used, xs, wg, wu, wd)


def _combine_kernel(final, pos_ref, y_ref, meta_ref, h_ref, g_ref, b_ref, *rest):
    if final:
        out_ref, buf1, buf2, sem = rest
    else:
        hf_ref, hb_ref, buf1, buf2, sem = rest
    i = pl.program_id(0)
    tm = TOK_TILE

    def copy(t, k):
        src = pos_ref[0, 0, 2 * t + k]
        buf = buf1 if k == 0 else buf2
        return pltpu.make_async_copy(y_ref.at[pl.ds(src, 1)], buf.at[pl.ds(t, 1)], sem)

    def start(t, carry):
        copy(t, 0).start()
        copy(t, 1).start()
        return carry

    def wait(t, carry):
        copy(t, 0).wait()
        copy(t, 1).wait()
        return carry

    lax.fori_loop(0, tm, start, 0, unroll=8)
    lax.fori_loop(0, tm, wait, 0, unroll=8)

    meta = meta_ref[...]
    w1 = meta[:, 2:3]
    w2 = meta[:, 3:4]
    f = w1 * buf1[...] + w2 * buf2[...]
    y = _layer_norm(ALPHA * h_ref[...] + f, g_ref[...], b_ref[...])
    if final:
        out_ref[0] = y
    else:
        y = jnp.where(_valid_rows(i, tm), y, 0.0)
        hf_ref[...] = y
        hb_ref[...] = y.astype(BF16)


def _combine(pos, y, meta, h_f, g, b, final):
    tm = TOK_TILE
    if final:
        out_specs = pl.BlockSpec((1, tm, D), lambda i: (i // NCH, jnp.maximum(i % NCH - 1, 0), 0))
        out_shape = jax.ShapeDtypeStruct((BATCH, SEQ, D), F32)
    else:
        out_specs = [pl.BlockSpec((tm, D), lambda i: (i, 0)), pl.BlockSpec((tm, D), lambda i: (i, 0))]
        out_shape = [jax.ShapeDtypeStruct((TP, D), F32), jax.ShapeDtypeStruct((TP, D), BF16)]
    return pl.pallas_call(
        functools.partial(_combine_kernel, final),
        grid=(TP // tm,),
        in_specs=[
            pl.BlockSpec((1, 1, 2 * tm), lambda i: (i, 0, 0), memory_space=pltpu.SMEM),
            pl.BlockSpec(memory_space=pl.ANY),
            pl.BlockSpec((tm, LANES), lambda i: (i, 0)),
            pl.BlockSpec((tm, D), lambda i: (i, 0)),
            pl.BlockSpec((1, D), lambda i: (0, 0)),
            pl.BlockSpec((1, D), lambda i: (0, 0)),
        ],
        out_specs=out_specs,
        out_shape=out_shape,
        scratch_shapes=[pltpu.VMEM((tm, D), F32), pltpu.VMEM((tm, D), F32), pltpu.SemaphoreType.DMA(())],
        compiler_params=_cp(("arbitrary",), 32),
        name="moe_combine",
    )(pos, y, meta, h_f, g, b)


def _pad_lanes(v, width=LANES):
    return jnp.pad(v, ((0, 0), (0, width - v.shape[1])))


def _token_mixer(h_f, h_bf, w_in, b_gate, conv_a_w, w_out_a, ssd_conv_w, ssd_conv_b, dt_bias, a_log,
                 d_skip, ssd_norm_w, w_out_b, w_o, ln_g, ln_b):
    dt0 = G0
    w_main = jnp.concatenate([w_in[:, :dt0], w_in[:, dt0 + HEADS:]], axis=1).astype(BF16)
    w_dt = _pad_lanes(w_in[:, dt0:dt0 + HEADS]).astype(BF16)
    proj = _in_proj(h_bf, w_main)
    dt, cum = _dt_proj(h_bf, w_dt, _pad_lanes(dt_bias[None, :]), _pad_lanes(a_log[None, :]))
    ga_ya = _mixer_a(proj, conv_a_w, w_out_a.astype(BF16), b_gate[None, :D])
    dskip = jnp.repeat(d_skip, HEADDIM)[None, :]
    yn = _ssd(proj, dt, cum, ssd_conv_w, ssd_conv_b[None, :], dskip, ssd_norm_w[None, :])
    m = _out_b(yn, w_out_b.astype(BF16), proj, b_gate[None, D:], ga_ya)
    return _wo_ln(m, w_o.astype(BF16), h_f, ln_g[None, :], ln_b[None, :])


def _moe_layer(h_f, router_w, wg, wu, wd, ln_g, ln_b, final):
    meta, cnt = _router(h_f, _pad_lanes(router_w))
    i1 = meta[:, 0].astype(jnp.int32)
    i2 = meta[:, 1].astype(jnp.int32)
    counts = cnt[0, :N_EXPERTS].astype(jnp.int32)
    padded = ((counts + TM_MOE - 1) // TM_MOE) * TM_MOE
    ends = jnp.cumsum(padded)
    starts = ends - padded
    pos = jnp.stack([starts[i1] + meta[:, 4].astype(jnp.int32),
                     starts[i2] + meta[:, 5].astype(jnp.int32)], axis=1)
    pos = jnp.clip(pos, 0, P_MOE - 1)
    tok = jnp.arange(TP, dtype=jnp.int32)
    valid = (tok % LP) >= PADF
    dst = jnp.where(valid[:, None], pos, P_MOE).reshape(-1)
    src = jnp.zeros((P_MOE,), jnp.int32).at[dst].set(jnp.repeat(tok, 2), mode="drop")
    src = src.reshape(N_MOE_TILES, 1, TM_MOE)
    pos = pos.reshape(TP // TOK_TILE, 1, 2 * TOK_TILE)
    n_used = (ends[-1] // TM_MOE).reshape(1).astype(jnp.int32)
    tile_start = jnp.arange(N_MOE_TILES, dtype=jnp.int32) * TM_MOE
    tile_e = jnp.sum((tile_start[:, None] >= ends[None, :]).astype(jnp.int32), axis=1)
    last_e = jnp.minimum(tile_e[jnp.maximum(n_used[0] - 1, 0)], N_EXPERTS - 1)
    tile_e = jnp.where(tile_start < ends[-1], jnp.minimum(tile_e, N_EXPERTS - 1), last_e)
    xs = _gather_rows(n_used, src, h_f)
    y = _ffn_moe(tile_e, n_used, xs, wg.astype(BF16), wu.astype(BF16), wd.astype(BF16))
    return _combine(pos, y, meta, h_f, ln_g[None, :], ln_b[None, :], final)


def kernel(x, meta_tokens, ln_in_g, ln_in_b, w_in, b_gate, conv_a_w, w_out_a, ssd_conv_w, ssd_conv_b,
           dt_bias, a_log, d_skip, ssd_norm_w, w_out_b, w_o, ln1_g, ln1_b, ffn_w_gate, ffn_w_up,
           ffn_w_down, router, moe_w_gate, moe_w_up, moe_w_down, ln2_g, ln2_b):
    h_f, h_bf = _ln_in(x, meta_tokens.astype(x.dtype), ln_in_g[None, :], ln_in_b[None, :])
    for i in range(DEPTH):
        h_f, h_bf = _token_mixer(h_f, h_bf, w_in[i], b_gate[i], conv_a_w[i], w_out_a[i], ssd_conv_w[i],
                                 ssd_conv_b[i], dt_bias[i], a_log[i], d_skip[i], ssd_norm_w[i],
                                 w_out_b[i], w_o[i], ln1_g[i], ln1_b[i])
        j = i // 2
        if i % 2 == 0:
            h_f, h_bf = _ffn_dense(h_bf, h_f, ffn_w_gate[j].astype(BF16), ffn_w_up[j].astype(BF16),
                                   ffn_w_down[j].astype(BF16), ln2_g[i][None, :], ln2_b[i][None, :])
        else:
            final = i == DEPTH - 1
            res = _moe_layer(h_f, router[j], moe_w_gate[j], moe_w_up[j], moe_w_down[j],
                             ln2_g[i], ln2_b[i], final)
            if final:
                return res
            h_f, h_bf = res
    return h_f.reshape(BATCH, LP, D)[:, PADF + N_META:, :]
```

```python
import functools

import jax
import jax.numpy as jnp
from jax import lax
from jax.experimental import pallas as pl
from jax.experimental.pallas import tpu as pltpu

F32 = jnp.float32
BF16 = jnp.bfloat16

D = 2048
BATCH = 4
SEQ = 4096
N_META = 16
DEPTH = 2
D_INNER = 2 * D
HEADDIM = 64
HEADS = D_INNER // HEADDIM
GROUPS = 8
HPG = HEADS // GROUPS
STATE = 128
GW = D_INNER // GROUPS
XBC = D_INNER + 2 * GROUPS * STATE
FF_DENSE = 5632
N_EXPERTS = 8
FF_EXPERT = 7168
ALPHA = (2.0 * DEPTH) ** 0.25
LN_EPS = 1e-5
RMS_EPS = 1e-5
SSD_CONV_K = 4

CH = 256
PADF = CH - N_META
LP = PADF + N_META + SEQ
NCH = LP // CH
TP = BATCH * LP

C0, B0, V0 = 0, D, 2 * D
Z0 = 3 * D
X0 = Z0 + D_INNER
G0 = X0 + XBC
NP = G0 + 2 * D

LANES = 128
SUBLANES = 8
BF16_SUBLANES = 16

TM_MOE = 512
N_MOE_TILES = (2 * BATCH * (SEQ + N_META)) // TM_MOE + N_EXPERTS
P_MOE = N_MOE_TILES * TM_MOE
TF_MOE = 1024
TM_FFN = 512
TF_FFN = 512
FFN_DENSE_STEPS = (TP // TM_FFN) * (FF_DENSE // TF_FFN)
TM_INPROJ = 1024
TN_INPROJ = 2048
IN_PROJ_STEPS = (NP // TN_INPROJ) * (TP // TM_INPROJ)
TOK_TILE = CH

SUB = 128
LOG2E = 1.4426950408889634
EXP2_CLAMP = 100.0

MIB = 1024 * 1024


def _cp(sem, vmem_mib):
    return pltpu.CompilerParams(dimension_semantics=sem, vmem_limit_bytes=int(vmem_mib * MIB))


def _sigmoid(x):
    return 0.5 * jnp.tanh(0.5 * x) + 0.5


def _silu(x):
    h = 0.5 * x
    return h + h * jnp.tanh(h)


def _softplus(x):
    return jnp.maximum(x, 0.0) + jnp.log1p(jnp.exp(-jnp.abs(x)))


def _layer_norm(x, g, b):
    mu = jnp.mean(x, axis=-1, keepdims=True)
    xc = x - mu
    var = jnp.mean(xc * xc, axis=-1, keepdims=True)
    return xc * lax.rsqrt(var + LN_EPS) * g + b


def _valid_rows(tile_idx, tm):
    row = lax.broadcasted_iota(jnp.int32, (tm, 1), 0)
    valid = None
    for s in range(tm // CH):
        chunk = tile_idx * (tm // CH) + s
        hi = jnp.where(chunk % NCH == 0, s * CH + PADF, s * CH)
        ok = jnp.logical_or(row < s * CH, row >= hi)
        valid = ok if valid is None else jnp.logical_and(valid, ok)
    return valid


def _ln_in_kernel(x_ref, meta_ref, g_ref, b_ref, hf_ref, hb_ref):
    j = pl.program_id(1)

    @pl.when(j == 0)
    def _():
        hf_ref[...] = jnp.zeros_like(hf_ref)
        hb_ref[...] = jnp.zeros_like(hb_ref)
        y = _layer_norm(meta_ref[...], g_ref[...], b_ref[...])
        hf_ref[PADF:, :] = y
        hb_ref[PADF:, :] = y.astype(BF16)

    @pl.when(j > 0)
    def _():
        y = _layer_norm(x_ref[0], g_ref[...], b_ref[...])
        hf_ref[...] = y
        hb_ref[...] = y.astype(BF16)


def _ln_in(x, meta, g, b):
    return pl.pallas_call(
        _ln_in_kernel,
        grid=(BATCH, NCH),
        in_specs=[
            pl.BlockSpec((1, CH, D), lambda bi, j: (bi, jnp.maximum(j - 1, 0), 0)),
            pl.BlockSpec((N_META, D), lambda bi, j: (0, 0)),
            pl.BlockSpec((1, D), lambda bi, j: (0, 0)),
            pl.BlockSpec((1, D), lambda bi, j: (0, 0)),
        ],
        out_specs=[
            pl.BlockSpec((CH, D), lambda bi, j: (bi * NCH + j, 0)),
            pl.BlockSpec((CH, D), lambda bi, j: (bi * NCH + j, 0)),
        ],
        out_shape=[jax.ShapeDtypeStruct((TP, D), F32), jax.ShapeDtypeStruct((TP, D), BF16)],
        compiler_params=_cp(("arbitrary", "arbitrary"), 32),
        name="ln_in",
    )(x, meta, g, b)


def _side_cast_specs(side, linear_step):
    src, rb = side
    nblk = src.shape[0] // rb
    imap = lambda *ids: (jnp.minimum(linear_step(*ids), nblk - 1), 0)
    spec = pl.BlockSpec((rb, src.shape[1]), imap)
    return nblk, spec, jax.ShapeDtypeStruct(src.shape, BF16)


def _side_block_rows(rows, steps):
    for nblk in range(min(steps, rows // BF16_SUBLANES), 0, -1):
        if rows % nblk == 0 and (rows // nblk) % BF16_SUBLANES == 0:
            return rows // nblk
    return rows


def _side_cast(nblk, step, src_ref, dst_ref):
    @pl.when(step < nblk)
    def _():
        dst_ref[...] = src_ref[...].astype(BF16)


def _mm_kernel(nblk, a_ref, w_ref, *rest):
    if nblk is None:
        (o_ref,) = rest
    else:
        c_ref, o_ref, co_ref = rest
        _side_cast(nblk, pl.program_id(0) * pl.num_programs(1) + pl.program_id(1), c_ref, co_ref)
    o_ref[...] = jnp.dot(a_ref[...], w_ref[...], preferred_element_type=F32).astype(o_ref.dtype)


def _in_proj(h_bf, w_main, side=None):
    tm, tn = TM_INPROJ, TN_INPROJ
    grid = (NP // tn, TP // tm)
    in_specs = [
        pl.BlockSpec((tm, D), lambda j, i: (i, 0)),
        pl.BlockSpec((D, tn), lambda j, i: (0, j)),
    ]
    out_specs = [pl.BlockSpec((tm, tn), lambda j, i: (i, j))]
    out_shape = [jax.ShapeDtypeStruct((TP, NP), BF16)]
    args = [h_bf, w_main]
    nblk = None
    if side is not None:
        nblk, spec, shape = _side_cast_specs(side, lambda j, i: j * grid[1] + i)
        in_specs.append(spec)
        out_specs.append(spec)
        out_shape.append(shape)
        args.append(side[0])
    res = pl.pallas_call(
        functools.partial(_mm_kernel, nblk),
        grid=grid,
        in_specs=in_specs,
        out_specs=out_specs,
        out_shape=out_shape,
        compiler_params=_cp(("arbitrary", "arbitrary"), 56),
        name="in_proj",
    )(*args)
    return res if side is not None else (res[0], None)


def _dt_kernel(a_ref, w_ref, bias_ref, alog_ref, dt_ref, cum_ref):
    i = pl.program_id(0)
    raw = jnp.dot(a_ref[...], w_ref[...], preferred_element_type=F32) + bias_ref[...]
    dt = _softplus(raw)
    a_neg = -jnp.exp(alog_ref[...])
    row = lax.broadcasted_iota(jnp.int32, (CH, 1), 0)
    first = jnp.where(i % NCH == 0, PADF, 0)
    da = jnp.where(row >= first, dt * a_neg, 0.0)
    r = lax.broadcasted_iota(jnp.int32, (CH, CH), 0)
    c = lax.broadcasted_iota(jnp.int32, (CH, CH), 1)
    tril = (r >= c).astype(F32)
    cum = jnp.dot(tril, da, precision=lax.Precision.HIGHEST, preferred_element_type=F32)
    dt_ref[...] = dt
    cum_ref[...] = cum


def _dt_proj(h_bf, w_dt, dt_bias, a_log):
    return pl.pallas_call(
        _dt_kernel,
        grid=(TP // CH,),
        in_specs=[
            pl.BlockSpec((CH, D), lambda i: (i, 0)),
            pl.BlockSpec((D, LANES), lambda i: (0, 0)),
            pl.BlockSpec((1, LANES), lambda i: (0, 0)),
            pl.BlockSpec((1, LANES), lambda i: (0, 0)),
        ],
        out_specs=[
            pl.BlockSpec((CH, LANES), lambda i: (i, 0)),
            pl.BlockSpec((CH, LANES), lambda i: (i, 0)),
        ],
        out_shape=[jax.ShapeDtypeStruct((TP, LANES), F32), jax.ShapeDtypeStruct((TP, LANES), F32)],
        compiler_params=_cp(("arbitrary",), 16),
        name="dt_proj",
    )(h_bf, w_dt, dt_bias, a_log)


def _causal_conv(x, halo, w, k):
    acc = w[k - 1:k, :] * x
    for s in range(1, k):
        acc = acc + w[k - 1 - s:k - s, :] * pltpu.roll(x, s, 0)
    x8 = x[0:SUBLANES, :]
    h8 = halo[BF16_SUBLANES - SUBLANES:, :]
    row = lax.broadcasted_iota(jnp.int32, (SUBLANES, 1), 0)
    top = w[k - 1:k, :] * x8
    for s in range(1, k):
        shifted = jnp.where(row < s, pltpu.roll(h8, s, 0), pltpu.roll(x8, s, 0))
        top = top + w[k - 1 - s:k - s, :] * shifted
    return jnp.concatenate([top, acc[SUBLANES:, :]], axis=0)


def _mixa_kernel(c_ref, b_ref, v_ref, ch_ref, vh_ref, g_ref, cw_ref, wo_ref, bg_ref, o_ref):
    u = c_ref[...].astype(F32) * v_ref[...].astype(F32)
    hu = ch_ref[...].astype(F32) * vh_ref[...].astype(F32)
    conv = _causal_conv(u, hu, cw_ref[...], 3)
    lhs = (b_ref[...].astype(F32) * conv).astype(BF16)
    ya = jnp.dot(lhs, wo_ref[...], preferred_element_type=F32)
    ga = _sigmoid(g_ref[...].astype(F32) + bg_ref[...])
    o_ref[...] = (ga * ya).astype(BF16)


def _mixer_a(proj, conv_w, w_out_a, bgate_a):
    tm = 512
    hb = tm // BF16_SUBLANES
    halo = lambda col: (lambda i: (jnp.maximum(i * hb - 1, 0), col))
    return pl.pallas_call(
        _mixa_kernel,
        grid=(TP // tm,),
        in_specs=[
            pl.BlockSpec((tm, D), lambda i: (i, C0 // D)),
            pl.BlockSpec((tm, D), lambda i: (i, B0 // D)),
            pl.BlockSpec((tm, D), lambda i: (i, V0 // D)),
            pl.BlockSpec((BF16_SUBLANES, D), halo(C0 // D)),
            pl.BlockSpec((BF16_SUBLANES, D), halo(V0 // D)),
            pl.BlockSpec((tm, D), lambda i: (i, G0 // D)),
            pl.BlockSpec((3, D), lambda i: (0, 0)),
            pl.BlockSpec((D, D), lambda i: (0, 0), pipeline_mode=pl.Buffered(1)),
            pl.BlockSpec((1, D), lambda i: (0, 0)),
        ],
        out_specs=pl.BlockSpec((tm, D), lambda i: (i, 0)),
        out_shape=jax.ShapeDtypeStruct((TP, D), BF16),
        compiler_params=_cp(("arbitrary",), 56),
        name="mixer_a",
    )(proj, proj, proj, proj, proj, proj, conv_w, w_out_a, bgate_a)


def _ssd_kernel(xs_ref, bm_ref, cm_ref, z_ref, dt_ref, cum_ref,
                wx_ref, wb_ref, wc_ref, bx_ref, bb_ref, bc_ref, dsk_ref, nw_ref,
                o_ref, st_ref):
    g = pl.program_id(1)
    st_ref[...] = jnp.zeros_like(st_ref)

    rr = lax.broadcasted_iota(jnp.int32, (CH, CH), 0)
    cc = lax.broadcasted_iota(jnp.int32, (CH, CH), 1)
    shift_mats = [(rr - cc == s).astype(BF16) for s in range(1, SSD_CONV_K)]
    w_all = jnp.concatenate([wx_ref[...], wb_ref[...], wc_ref[...]], axis=1)
    b_all = jnp.concatenate([bx_ref[...], bb_ref[...], bc_ref[...]], axis=1)
    taps = [w_all[k:k + 1, :] for k in range(SSD_CONV_K)]
    lane_shift = (LANES - HPG * g) % LANES
    row = lax.broadcasted_iota(jnp.int32, (CH, 1), 0)
    row8 = lax.broadcasted_iota(jnp.int32, (SUBLANES, 1), 0)
    r2 = lax.broadcasted_iota(jnp.int32, (SUB, SUB), 0)
    c2 = lax.broadcasted_iota(jnp.int32, (SUB, SUB), 1)
    causal = r2 >= c2
    lane = lax.broadcasted_iota(jnp.int32, (1, LANES), 1)
    lo = lane < HEADDIM
    dskip = dsk_ref[...]
    norm_w = nw_ref[...]

    def chunk(c, carry):
        r0 = pl.multiple_of(c * CH, CH)
        h0 = pl.multiple_of(jnp.maximum(r0 - BF16_SUBLANES, 0), BF16_SUBLANES)
        rows = pl.ds(r0, CH)
        above = pl.ds(h0, BF16_SUBLANES)
        raw = jnp.concatenate([xs_ref[rows, :], bm_ref[rows, :], cm_ref[rows, :]], axis=1)
        halo = jnp.concatenate([xs_ref[above, :], bm_ref[above, :], cm_ref[above, :]], axis=1)
        h8 = halo[BF16_SUBLANES - SUBLANES:, :].astype(F32)

        acc = taps[SSD_CONV_K - 1] * raw.astype(F32) + b_all
        fix = jnp.zeros((SUBLANES, raw.shape[1]), F32)
        for s in range(1, SSD_CONV_K):
            tap = taps[SSD_CONV_K - 1 - s]
            acc = acc + tap * jnp.dot(shift_mats[s - 1], raw, preferred_element_type=F32)
            fix = fix + jnp.where(row8 < s, tap * pltpu.roll(h8, s, 0), 0.0)
        act = _silu(jnp.concatenate([acc[0:SUBLANES, :] + fix, acc[SUBLANES:, :]], axis=0))
        first = jnp.where(c == 0, PADF, 0)
        xs = jnp.where(row >= first, act[:, :GW], 0.0)
        bm = act[:, GW:GW + STATE]
        cm = act[:, GW + STATE:]

        dtg = pltpu.roll(dt_ref[rows, :], lane_shift, 1)
        cumg = pltpu.roll(cum_ref[rows, :], lane_shift, 1) * LOG2E
        bm_bf = bm.astype(BF16)
        cm_bf = cm.astype(BF16)
        xs_bf = xs.astype(BF16)

        st = st_ref[...]
        y_rows = []
        for q in range(CH // SUB):
            rs = slice(q * SUB, (q + 1) * SUB)
            cq = cumg[rs, :]
            if q > 0:
                cq = cq - cumg[q * SUB - 1:q * SUB, :]
            dq = dtg[rs, :]
            last = cq[SUB - 1:SUB, :]
            ecum = jnp.exp2(cq)
            wts = jnp.exp2(last - cq) * dq
            elast = jnp.exp2(last)
            src_t = (cq - jnp.log2(dq)).T

            cb = lax.dot_general(cm_bf[rs, :], bm_bf[rs, :], (((1,), (1,)), ((), ())),
                                 preferred_element_type=F32)
            cbm = jnp.where(causal, cb, 0.0)
            y_off = jnp.dot(cm_bf[rs, :], st.astype(BF16), preferred_element_type=F32)

            def head_mat(j):
                seg = cq[:, j:j + 1] - src_t[j:j + 1, :]
                return (cbm * jnp.exp2(jnp.minimum(seg, EXP2_CLAMP))).astype(BF16)

            ys, xws, els = [], [], []
            for p in range(GW // LANES):
                ja, jb = 2 * p, 2 * p + 1
                sl = slice(p * LANES, (p + 1) * LANES)
                xp = xs_bf[rs, sl]
                ya = jnp.dot(head_mat(ja), xp, preferred_element_type=F32)
                yb = jnp.dot(head_mat(jb), xp, preferred_element_type=F32)
                e_p = jnp.where(lo, ecum[:, ja:ja + 1], ecum[:, jb:jb + 1])
                ys.append(jnp.where(lo, ya, yb) + e_p * y_off[:, sl])
                w_p = jnp.where(lo, wts[:, ja:ja + 1], wts[:, jb:jb + 1])
                xws.append((xs[rs, sl] * w_p).astype(BF16))
                els.append(jnp.where(lo, elast[:, ja:ja + 1], elast[:, jb:jb + 1]))
            y_rows.append(jnp.concatenate(ys, axis=1))
            xw = jnp.concatenate(xws, axis=1)
            el = jnp.concatenate(els, axis=1)
            bm_t = bm[rs, :].T.astype(BF16)
            st = st * el + jnp.dot(bm_t, xw, preferred_element_type=F32)
        st_ref[...] = st
        y = jnp.concatenate(y_rows, axis=0)

        y = y + xs * dskip
        yg = y * _silu(z_ref[rows, :].astype(F32))
        ms = jnp.mean(yg * yg, axis=-1, keepdims=True)
        o_ref[rows, :] = (yg * lax.rsqrt(ms + RMS_EPS) * norm_w).astype(BF16)
        return carry

    lax.fori_loop(0, NCH, chunk, 0)


def _ssd(proj, dt, cum, conv_w, conv_b, dskip, norm_w):
    xcol = lambda g: X0 // GW + g
    bcol = lambda g: (X0 + D_INNER) // STATE + g
    ccol = lambda g: (X0 + D_INNER + GROUPS * STATE) // STATE + g
    zcol = lambda g: Z0 // GW + g
    wbcol = lambda g: D_INNER // STATE + g
    wccol = lambda g: (D_INNER + GROUPS * STATE) // STATE + g
    rows = lambda col_fn: (lambda b, g: (b, col_fn(g)))
    par = lambda col_fn: (lambda b, g: (0, col_fn(g)))
    return pl.pallas_call(
        _ssd_kernel,
        grid=(BATCH, GROUPS),
        in_specs=[
            pl.BlockSpec((LP, GW), rows(xcol)),
            pl.BlockSpec((LP, STATE), rows(bcol)),
            pl.BlockSpec((LP, STATE), rows(ccol)),
            pl.BlockSpec((LP, GW), rows(zcol)),
            pl.BlockSpec((LP, LANES), lambda b, g: (b, 0)),
            pl.BlockSpec((LP, LANES), lambda b, g: (b, 0)),
            pl.BlockSpec((SSD_CONV_K, GW), par(lambda g: g)),
            pl.BlockSpec((SSD_CONV_K, STATE), par(wbcol)),
            pl.BlockSpec((SSD_CONV_K, STATE), par(wccol)),
            pl.BlockSpec((1, GW), par(lambda g: g)),
            pl.BlockSpec((1, STATE), par(wbcol)),
            pl.BlockSpec((1, STATE), par(wccol)),
            pl.BlockSpec((1, GW), par(lambda g: g)),
            pl.BlockSpec((1, GW), par(lambda g: g)),
        ],
        out_specs=pl.BlockSpec((LP, GW), lambda b, g: (b, g)),
        out_shape=jax.ShapeDtypeStruct((TP, D_INNER), BF16),
        scratch_shapes=[pltpu.VMEM((STATE, GW), F32)],
        compiler_params=_cp(("arbitrary", "arbitrary"), 56),
        name="ssd",
    )(proj, proj, proj, proj, dt, cum,
      conv_w, conv_w, conv_w, conv_b, conv_b, conv_b, dskip, norm_w)


def _outb_kernel(y_ref, w_ref, g_ref, bg_ref, ya_ref, o_ref):
    yb = jnp.dot(y_ref[...], w_ref[...], preferred_element_type=F32)
    gb = _sigmoid(g_ref[...].astype(F32) + bg_ref[...])
    o_ref[...] = (gb * yb + ya_ref[...].astype(F32)).astype(BF16)


def _out_b(yn, w_out_b, proj, bgate_b, ga_ya):
    tm = 512
    return pl.pallas_call(
        _outb_kernel,
        grid=(TP // tm,),
        in_specs=[
            pl.BlockSpec((tm, D_INNER), lambda i: (i, 0)),
            pl.BlockSpec((D_INNER, D), lambda i: (0, 0), pipeline_mode=pl.Buffered(1)),
            pl.BlockSpec((tm, D), lambda i: (i, G0 // D + 1)),
            pl.BlockSpec((1, D), lambda i: (0, 0)),
            pl.BlockSpec((tm, D), lambda i: (i, 0)),
        ],
        out_specs=pl.BlockSpec((tm, D), lambda i: (i, 0)),
        out_shape=jax.ShapeDtypeStruct((TP, D), BF16),
        compiler_params=_cp(("arbitrary",), 56),
        name="out_b",
    )(yn, w_out_b, proj, bgate_b, ga_ya)


def _wo_kernel(m_ref, w_ref, h_ref, g_ref, b_ref, hf_ref, hb_ref):
    mix = jnp.dot(m_ref[...], w_ref[...], preferred_element_type=F32)
    y = _layer_norm(ALPHA * h_ref[...] + mix, g_ref[...], b_ref[...])
    hf_ref[...] = y
    hb_ref[...] = y.astype(BF16)


def _wo_ln(m, w_o, h, g, b):
    tm = 512
    return pl.pallas_call(
        _wo_kernel,
        grid=(TP // tm,),
        in_specs=[
            pl.BlockSpec((tm, D), lambda i: (i, 0)),
            pl.BlockSpec((D, D), lambda i: (0, 0), pipeline_mode=pl.Buffered(1)),
            pl.BlockSpec((tm, D), lambda i: (i, 0)),
            pl.BlockSpec((1, D), lambda i: (0, 0)),
            pl.BlockSpec((1, D), lambda i: (0, 0)),
        ],
        out_specs=[
            pl.BlockSpec((tm, D), lambda i: (i, 0)),
            pl.BlockSpec((tm, D), lambda i: (i, 0)),
        ],
        out_shape=[jax.ShapeDtypeStruct((TP, D), F32), jax.ShapeDtypeStruct((TP, D), BF16)],
        compiler_params=_cp(("arbitrary",), 56),
        name="wo_ln",
    )(m, w_o, h, g, b)


def _swiglu_step(x_bf, wg_ref, wu_ref, wd_ref):
    gate = jnp.dot(x_bf, wg_ref[...], preferred_element_type=F32)
    up = jnp.dot(x_bf, wu_ref[...], preferred_element_type=F32)
    mid = (_silu(gate) * up).astype(BF16)
    return jnp.dot(mid, wd_ref[...], preferred_element_type=F32)


def _ffn_dense_kernel(nblk, x_ref, wg_ref, wu_ref, wd_ref, h_ref, g_ref, b_ref, *rest):
    i = pl.program_id(0)
    f = pl.program_id(1)
    if nblk is None:
        hf_ref, hb_ref, acc_ref = rest
    else:
        c_ref, hf_ref, hb_ref, co_ref, acc_ref = rest
        _side_cast(nblk, i * pl.num_programs(1) + f, c_ref, co_ref)

    @pl.when(f == 0)
    def _():
        acc_ref[...] = jnp.zeros_like(acc_ref)

    acc_ref[...] += _swiglu_step(x_ref[...], wg_ref, wu_ref, wd_ref)

    @pl.when(f == pl.num_programs(1) - 1)
    def _():
        y = _layer_norm(ALPHA * h_ref[...] + acc_ref[...], g_ref[...], b_ref[...])
        y = jnp.where(_valid_rows(i, TM_FFN), y, 0.0)
        hf_ref[...] = y
        hb_ref[...] = y.astype(BF16)


def _ffn_dense(h_bf, h_f, wg, wu, wd, g, b, side=None):
    tm, tf = TM_FFN, TF_FFN
    grid = (TP // tm, FF_DENSE // tf)
    in_specs = [
        pl.BlockSpec((tm, D), lambda i, f: (i, 0)),
        pl.BlockSpec((D, tf), lambda i, f: (0, f)),
        pl.BlockSpec((D, tf), lambda i, f: (0, f)),
        pl.BlockSpec((tf, D), lambda i, f: (f, 0)),
        pl.BlockSpec((tm, D), lambda i, f: (i, 0)),
        pl.BlockSpec((1, D), lambda i, f: (0, 0)),
        pl.BlockSpec((1, D), lambda i, f: (0, 0)),
    ]
    out_specs = [pl.BlockSpec((tm, D), lambda i, f: (i, 0)), pl.BlockSpec((tm, D), lambda i, f: (i, 0))]
    out_shape = [jax.ShapeDtypeStruct((TP, D), F32), jax.ShapeDtypeStruct((TP, D), BF16)]
    args = [h_bf, wg, wu, wd, h_f, g, b]
    nblk = None
    if side is not None:
        nblk, spec, shape = _side_cast_specs(side, lambda i, f: i * grid[1] + f)
        in_specs.append(spec)
        out_specs.append(spec)
        out_shape.append(shape)
        args.append(side[0])
    res = pl.pallas_call(
        functools.partial(_ffn_dense_kernel, nblk),
        grid=grid,
        in_specs=in_specs,
        out_specs=out_specs,
        out_shape=out_shape,
        scratch_shapes=[pltpu.VMEM((tm, D), F32)],
        compiler_params=_cp(("arbitrary", "arbitrary"), 58),
        name="ffn_dense",
    )(*args)
    return res if side is not None else (res[0], res[1], None)


def _router_kernel(h_ref, r_ref, meta_ref, cnt_ref, carry_ref):
    i = pl.program_id(0)
    tm = TOK_TILE

    @pl.when(i == 0)
    def _():
        carry_ref[...] = jnp.zeros_like(carry_ref)

    logits = jnp.dot(h_ref[...], r_ref[...], precision=lax.Precision.HIGHEST,
                     preferred_element_type=F32)
    lane = lax.broadcasted_iota(jnp.int32, (tm, LANES), 1).astype(F32)
    neg = jnp.float32(-jnp.inf)
    logits = jnp.where(lane < N_EXPERTS, logits, neg)
    m1 = jnp.max(logits, axis=-1, keepdims=True)
    i1 = jnp.min(jnp.where(logits == m1, lane, float(LANES)), axis=-1, keepdims=True)
    rest = jnp.where(lane == i1, neg, logits)
    m2 = jnp.max(rest, axis=-1, keepdims=True)
    i2 = jnp.min(jnp.where(rest == m2, lane, float(LANES)), axis=-1, keepdims=True)
    e = jnp.exp(m2 - m1)
    w1 = 1.0 / (1.0 + e)
    w2 = e / (1.0 + e)

    valid = _valid_rows(i, tm)
    sel = jnp.logical_and(jnp.logical_or(lane == i1, lane == i2), valid)
    onehot = sel.astype(F32)
    r = lax.broadcasted_iota(jnp.int32, (tm, tm), 0)
    c = lax.broadcasted_iota(jnp.int32, (tm, tm), 1)
    strict = (r > c).astype(BF16)
    before = jnp.dot(strict, onehot.astype(BF16), preferred_element_type=F32) + carry_ref[0:1, :]
    rank1 = jnp.sum(jnp.where(lane == i1, before, 0.0), axis=-1, keepdims=True)
    rank2 = jnp.sum(jnp.where(lane == i2, before, 0.0), axis=-1, keepdims=True)
    total = carry_ref[0:1, :] + jnp.sum(onehot, axis=0, keepdims=True)
    carry_ref[...] = jnp.broadcast_to(total, carry_ref.shape)
    cnt_ref[...] = jnp.broadcast_to(total, cnt_ref.shape)

    meta = jnp.where(lane == 0, i1, 0.0)
    meta = jnp.where(lane == 1, i2, meta)
    meta = jnp.where(lane == 2, w1, meta)
    meta = jnp.where(lane == 3, w2, meta)
    meta = jnp.where(lane == 4, rank1, meta)
    meta = jnp.where(lane == 5, rank2, meta)
    meta_ref[...] = meta


def _router(h_f, router_w):
    tm = TOK_TILE
    return pl.pallas_call(
        _router_kernel,
        grid=(TP // tm,),
        in_specs=[
            pl.BlockSpec((tm, D), lambda i: (i, 0)),
            pl.BlockSpec((D, LANES), lambda i: (0, 0)),
        ],
        out_specs=[
            pl.BlockSpec((tm, LANES), lambda i: (i, 0)),
            pl.BlockSpec((8, LANES), lambda i: (0, 0)),
        ],
        out_shape=[jax.ShapeDtypeStruct((TP, LANES), F32), jax.ShapeDtypeStruct((8, LANES), F32)],
        scratch_shapes=[pltpu.VMEM((8, LANES), F32)],
        compiler_params=_cp(("arbitrary",), 32),
        name="router",
    )(h_f, router_w)


def _gather_kernel(nu_ref, src_ref, h_ref, o_ref, buf, sem):
    i = pl.program_id(0)
    used = i < nu_ref[0]

    def copy(t):
        return pltpu.make_async_copy(h_ref.at[pl.ds(src_ref[0, 0, t], 1)], buf.at[pl.ds(t, 1)], sem)

    def start(t, carry):
        copy(t).start()
        return carry

    def wait(t, carry):
        copy(t).wait()
        return carry

    @pl.when(used)
    def _():
        lax.fori_loop(0, TM_MOE, start, 0, unroll=8)
        lax.fori_loop(0, TM_MOE, wait, 0, unroll=8)
        o_ref[...] = buf[...].astype(BF16)

    @pl.when(jnp.logical_not(used))
    def _():
        o_ref[...] = jnp.zeros_like(o_ref)


def _gather_rows(n_used, src, h_f):
    grid_spec = pltpu.PrefetchScalarGridSpec(
        num_scalar_prefetch=1,
        grid=(N_MOE_TILES,),
        in_specs=[
            pl.BlockSpec((1, 1, TM_MOE), lambda i, nu: (i, 0, 0), memory_space=pltpu.SMEM),
            pl.BlockSpec(memory_space=pl.ANY),
        ],
        out_specs=pl.BlockSpec((TM_MOE, D), lambda i, nu: (i, 0)),
        scratch_shapes=[pltpu.VMEM((TM_MOE, D), F32), pltpu.SemaphoreType.DMA(())],
    )
    return pl.pallas_call(
        _gather_kernel,
        grid_spec=grid_spec,
        out_shape=jax.ShapeDtypeStruct((P_MOE, D), BF16),
        compiler_params=_cp(("arbitrary",), 32),
        name="moe_gather",
    )(n_used, src, h_f)


def _ffn_moe_kernel(te_ref, nu_ref, x_ref, wg_ref, wu_ref, wd_ref, o_ref, acc_ref):
    del te_ref
    i = pl.program_id(0)
    f = pl.program_id(1)
    used = i < nu_ref[0]

    @pl.when(jnp.logical_and(used, f == 0))
    def _():
        acc_ref[...] = jnp.zeros_like(acc_ref)

    @pl.when(used)
    def _():
        acc_ref[...] += _swiglu_step(x_ref[...], wg_ref, wu_ref, wd_ref)

    @pl.when(jnp.logical_and(used, f == pl.num_programs(1) - 1))
    def _():
        o_ref[...] = acc_ref[...]

    @pl.when(jnp.logical_and(jnp.logical_not(used), f == 0))
    def _():
        o_ref[...] = jnp.zeros_like(o_ref)


def _ffn_moe(tile_e, n_used, xs, wg, wu, wd):
    tm, tf = TM_MOE, TF_MOE
    nf = FF_EXPERT // tf

    def fblk(i, f, nu):
        return jnp.where(i < nu[0], f, nf - 1)

    grid_spec = pltpu.PrefetchScalarGridSpec(
        num_scalar_prefetch=2,
        grid=(N_MOE_TILES, nf),
        in_specs=[
            pl.BlockSpec((tm, D), lambda i, f, te, nu: (i, 0)),
            pl.BlockSpec((None, D, tf), lambda i, f, te, nu: (te[i], 0, fblk(i, f, nu))),
            pl.BlockSpec((None, D, tf), lambda i, f, te, nu: (te[i], 0, fblk(i, f, nu))),
            pl.BlockSpec((None, tf, D), lambda i, f, te, nu: (te[i], fblk(i, f, nu), 0)),
        ],
        out_specs=pl.BlockSpec((tm, D), lambda i, f, te, nu: (i, 0)),
        scratch_shapes=[pltpu.VMEM((tm, D), F32)],
    )
    return pl.pallas_call(
        _ffn_moe_kernel,
        grid_spec=grid_spec,
        out_shape=jax.ShapeDtypeStruct((P_MOE, D), F32),
        compiler_params=_cp(("arbitrary", "arbitrary"), 58),
        name="ffn_moe",
    )(tile_e, n_used, xs, wg, wu, wd)


def _combine_kernel(final, pos_ref, y_ref, meta_ref, h_ref, g_ref, b_ref, *rest):
    if final:
        out_ref, buf1, buf2, sem = rest
    else:
        hf_ref, hb_ref, buf1, buf2, sem = rest
    i = pl.program_id(0)
    tm = TOK_TILE

    def copy(t, k):
        src = pos_ref[0, 0, 2 * t + k]
        buf = buf1 if k == 0 else buf2
        return pltpu.make_async_copy(y_ref.at[pl.ds(src, 1)], buf.at[pl.ds(t, 1)], sem)

    def start(t, carry):
        copy(t, 0).start()
        copy(t, 1).start()
        return carry

    def wait(t, carry):
        copy(t, 0).wait()
        copy(t, 1).wait()
        return carry

    lax.fori_loop(0, tm, start, 0, unroll=8)
    lax.fori_loop(0, tm, wait, 0, unroll=8)

    meta = meta_ref[...]
    w1 = meta[:, 2:3]
    w2 = meta[:, 3:4]
    f = w1 * buf1[...] + w2 * buf2[...]
    y = _layer_norm(ALPHA * h_ref[...] + f, g_ref[...], b_ref[...])
    if final:
        out_ref[0] = y
    else:
        y = jnp.where(_valid_rows(i, tm), y, 0.0)
        hf_ref[...] = y
        hb_ref[...] = y.astype(BF16)


def _combine(pos, y, meta, h_f, g, b, final):
    tm = TOK_TILE
    if final:
        out_specs = pl.BlockSpec((1, tm, D), lambda i: (i // NCH, jnp.maximum(i % NCH - 1, 0), 0))
        out_shape = jax.ShapeDtypeStruct((BATCH, SEQ, D), F32)
    else:
        out_specs = [pl.BlockSpec((tm, D), lambda i: (i, 0)), pl.BlockSpec((tm, D), lambda i: (i, 0))]
        out_shape = [jax.ShapeDtypeStruct((TP, D), F32), jax.ShapeDtypeStruct((TP, D), BF16)]
    return pl.pallas_call(
        functools.partial(_combine_kernel, final),
        grid=(TP // tm,),
        in_specs=[
            pl.BlockSpec((1, 1, 2 * tm), lambda i: (i, 0, 0), memory_space=pltpu.SMEM),
            pl.BlockSpec(memory_space=pl.ANY),
            pl.BlockSpec((tm, LANES), lambda i: (i, 0)),
            pl.BlockSpec((tm, D), lambda i: (i, 0)),
            pl.BlockSpec((1, D), lambda i: (0, 0)),
            pl.BlockSpec((1, D), lambda i: (0, 0)),
        ],
        out_specs=out_specs,
        out_shape=out_shape,
        scratch_shapes=[pltpu.VMEM((tm, D), F32), pltpu.VMEM((tm, D), F32), pltpu.SemaphoreType.DMA(())],
        compiler_params=_cp(("arbitrary",), 32),
        name="moe_combine",
    )(pos, y, meta, h_f, g, b)


def _pad_lanes(v, width=LANES):
    return jnp.pad(v, ((0, 0), (0, width - v.shape[1])))


def _token_mixer(h_f, h_bf, w_in, b_gate, conv_a_w, w_out_a, ssd_conv_w, ssd_conv_b, dt_bias, a_log,
                 d_skip, ssd_norm_w, w_out_b, w_o, ln_g, ln_b, side=None):
    dt0 = G0
    w_main = jnp.concatenate([w_in[:, :dt0], w_in[:, dt0 + HEADS:]], axis=1).astype(BF16)
    w_dt = _pad_lanes(w_in[:, dt0:dt0 + HEADS]).astype(BF16)
    proj, side_out = _in_proj(h_bf, w_main, side)
    dt, cum = _dt_proj(h_bf, w_dt, _pad_lanes(dt_bias[None, :]), _pad_lanes(a_log[None, :]))
    ga_ya = _mixer_a(proj, conv_a_w, w_out_a.astype(BF16), b_gate[None, :D])
    dskip = jnp.repeat(d_skip, HEADDIM)[None, :]
    yn = _ssd(proj, dt, cum, ssd_conv_w, ssd_conv_b[None, :], dskip, ssd_norm_w[None, :])
    m = _out_b(yn, w_out_b.astype(BF16), proj, b_gate[None, D:], ga_ya)
    h_f, h_bf = _wo_ln(m, w_o.astype(BF16), h_f, ln_g[None, :], ln_b[None, :])
    return h_f, h_bf, side_out


def _moe_layer(h_f, router_w, wg, wu, wd, ln_g, ln_b, final):
    meta, cnt = _router(h_f, _pad_lanes(router_w))
    i1 = meta[:, 0].astype(jnp.int32)
    i2 = meta[:, 1].astype(jnp.int32)
    counts = cnt[0, :N_EXPERTS].astype(jnp.int32)
    padded = ((counts + TM_MOE - 1) // TM_MOE) * TM_MOE
    ends = jnp.cumsum(padded)
    starts = ends - padded
    pos = jnp.stack([starts[i1] + meta[:, 4].astype(jnp.int32),
                     starts[i2] + meta[:, 5].astype(jnp.int32)], axis=1)
    pos = jnp.clip(pos, 0, P_MOE - 1)
    tok = jnp.arange(TP, dtype=jnp.int32)
    valid = (tok % LP) >= PADF
    dst = jnp.where(valid[:, None], pos, P_MOE).reshape(-1)
    src = jnp.zeros((P_MOE,), jnp.int32).at[dst].set(jnp.repeat(tok, 2), mode="drop")
    src = src.reshape(N_MOE_TILES, 1, TM_MOE)
    pos = pos.reshape(TP // TOK_TILE, 1, 2 * TOK_TILE)
    n_used = (ends[-1] // TM_MOE).reshape(1).astype(jnp.int32)
    tile_start = jnp.arange(N_MOE_TILES, dtype=jnp.int32) * TM_MOE
    tile_e = jnp.sum((tile_start[:, None] >= ends[None, :]).astype(jnp.int32), axis=1)
    last_e = jnp.minimum(tile_e[jnp.maximum(n_used[0] - 1, 0)], N_EXPERTS - 1)
    tile_e = jnp.where(tile_start < ends[-1], jnp.minimum(tile_e, N_EXPERTS - 1), last_e)
    xs = _gather_rows(n_used, src, h_f)
    y = _ffn_moe(tile_e, n_used, xs, wg, wu, wd)
    return _combine(pos, y, meta, h_f, ln_g[None, :], ln_b[None, :], final)


def kernel(x, meta_tokens, ln_in_g, ln_in_b, w_in, b_gate, conv_a_w, w_out_a, ssd_conv_w, ssd_conv_b,
           dt_bias, a_log, d_skip, ssd_norm_w, w_out_b, w_o, ln1_g, ln1_b, ffn_w_gate, ffn_w_up,
           ffn_w_down, router, moe_w_gate, moe_w_up, moe_w_down, ln2_g, ln2_b):
    h_f, h_bf = _ln_in(x, meta_tokens.astype(x.dtype), ln_in_g[None, :], ln_in_b[None, :])

    pending = [((i, k), w[i // 2]) for i in range(DEPTH) if i % 2 == 1
               for k, w in enumerate((moe_w_gate, moe_w_up, moe_w_down))]
    cast = {}

    def next_side(layer, steps):
        while pending and pending[0][0][0] < layer:
            pending.pop(0)
        if not pending:
            return None, None
        key, w = pending.pop(0)
        w2d = w.reshape(-1, w.shape[-1])
        return key, (w2d, _side_block_rows(w2d.shape[0], steps))

    for i in range(DEPTH):
        key, side = next_side(i, IN_PROJ_STEPS)
        h_f, h_bf, side_out = _token_mixer(h_f, h_bf, w_in[i], b_gate[i], conv_a_w[i], w_out_a[i],
                                           ssd_conv_w[i], ssd_conv_b[i], dt_bias[i], a_log[i], d_skip[i],
                                           ssd_norm_w[i], w_out_b[i], w_o[i], ln1_g[i], ln1_b[i], side)
        if key is not None:
            cast[key] = side_out
        j = i // 2
        if i % 2 == 0:
            key, side = next_side(i + 1, FFN_DENSE_STEPS)
            h_f, h_bf, side_out = _ffn_dense(h_bf, h_f, ffn_w_gate[j].astype(BF16), ffn_w_up[j].astype(BF16),
                                             ffn_w_down[j].astype(BF16), ln2_g[i][None, :], ln2_b[i][None, :],
                                             side)
            if key is not None:
                cast[key] = side_out
        else:
            final = i == DEPTH - 1
            wts = [cast[(i, k)].reshape(w[j].shape) if (i, k) in cast else w[j].astype(BF16)
                   for k, w in enumerate((moe_w_gate, moe_w_up, moe_w_down))]
            res = _moe_layer(h_f, router[j], wts[0], wts[1], wts[2], ln2_g[i], ln2_b[i], final)
            if final:
                return res
            h_f, h_bf = res
    return h_f.reshape(BATCH, LP, D)[:, PADF + N_META:, :]
```

```python
import functools

import jax
import jax.numpy as jnp
from jax import lax
from jax.experimental import pallas as pl
from jax.experimental.pallas import tpu as pltpu

F32 = jnp.float32
BF16 = jnp.bfloat16

D = 2048
BATCH = 4
SEQ = 4096
N_META = 16
DEPTH = 2
D_INNER = 2 * D
HEADDIM = 64
HEADS = D_INNER // HEADDIM
GROUPS = 8
HPG = HEADS // GROUPS
STATE = 128
GW = D_INNER // GROUPS
XBC = D_INNER + 2 * GROUPS * STATE
FF_DENSE = 5632
N_EXPERTS = 8
FF_EXPERT = 7168
ALPHA = (2.0 * DEPTH) ** 0.25
LN_EPS = 1e-5
RMS_EPS = 1e-5
SSD_CONV_K = 4

CH = 256
PADF = CH - N_META
LP = PADF + N_META + SEQ
NCH = LP // CH
TP = BATCH * LP

C0, B0, V0 = 0, D, 2 * D
Z0 = 3 * D
X0 = Z0 + D_INNER
G0 = X0 + XBC
NP = G0 + 2 * D

LANES = 128
SUBLANES = 8
BF16_SUBLANES = 16

TM_MOE = 512
N_MOE_TILES = (2 * BATCH * (SEQ + N_META)) // TM_MOE + N_EXPERTS
P_MOE = N_MOE_TILES * TM_MOE
TF_MOE = 1024
TM_FFN = 256
TN_FFN = 2816
FFN_DENSE_STEPS = (TP // TM_FFN) * (FF_DENSE // TN_FFN)
TM_FFN_DOWN = 256
ROW_CHUNK = 128
TM_INPROJ = 1024
TN_INPROJ = 2048
IN_PROJ_STEPS = (NP // TN_INPROJ) * (TP // TM_INPROJ)
TOK_TILE = CH

SUB = 128
LOG2E = 1.4426950408889634
EXP2_CLAMP = 100.0

MIB = 1024 * 1024


def _cp(sem, vmem_mib):
    return pltpu.CompilerParams(dimension_semantics=sem, vmem_limit_bytes=int(vmem_mib * MIB))


def _sigmoid(x):
    return 0.5 * jnp.tanh(0.5 * x) + 0.5


def _silu(x):
    h = 0.5 * x
    return h + h * jnp.tanh(h)


def _softplus(x):
    return jnp.maximum(x, 0.0) + jnp.log1p(jnp.exp(-jnp.abs(x)))


def _layer_norm(x, g, b):
    mu = jnp.mean(x, axis=-1, keepdims=True)
    xc = x - mu
    var = jnp.mean(xc * xc, axis=-1, keepdims=True)
    return xc * lax.rsqrt(var + LN_EPS) * g + b


def _row_chunks(rows, size=ROW_CHUNK):
    return [slice(r, r + size) for r in range(0, rows, size)]


def _valid_rows(tile_idx, tm):
    row = lax.broadcasted_iota(jnp.int32, (tm, 1), 0)
    valid = None
    for s in range(tm // CH):
        chunk = tile_idx * (tm // CH) + s
        hi = jnp.where(chunk % NCH == 0, s * CH + PADF, s * CH)
        ok = jnp.logical_or(row < s * CH, row >= hi)
        valid = ok if valid is None else jnp.logical_and(valid, ok)
    return valid


def _ln_in_kernel(x_ref, meta_ref, g_ref, b_ref, hf_ref, hb_ref):
    j = pl.program_id(1)

    @pl.when(j == 0)
    def _():
        hf_ref[...] = jnp.zeros_like(hf_ref)
        hb_ref[...] = jnp.zeros_like(hb_ref)
        y = _layer_norm(meta_ref[...], g_ref[...], b_ref[...])
        hf_ref[PADF:, :] = y
        hb_ref[PADF:, :] = y.astype(BF16)

    @pl.when(j > 0)
    def _():
        y = _layer_norm(x_ref[0], g_ref[...], b_ref[...])
        hf_ref[...] = y
        hb_ref[...] = y.astype(BF16)


def _ln_in(x, meta, g, b):
    return pl.pallas_call(
        _ln_in_kernel,
        grid=(BATCH, NCH),
        in_specs=[
            pl.BlockSpec((1, CH, D), lambda bi, j: (bi, jnp.maximum(j - 1, 0), 0)),
            pl.BlockSpec((N_META, D), lambda bi, j: (0, 0)),
            pl.BlockSpec((1, D), lambda bi, j: (0, 0)),
            pl.BlockSpec((1, D), lambda bi, j: (0, 0)),
        ],
        out_specs=[
            pl.BlockSpec((CH, D), lambda bi, j: (bi * NCH + j, 0)),
            pl.BlockSpec((CH, D), lambda bi, j: (bi * NCH + j, 0)),
        ],
        out_shape=[jax.ShapeDtypeStruct((TP, D), F32), jax.ShapeDtypeStruct((TP, D), BF16)],
        compiler_params=_cp(("arbitrary", "arbitrary"), 32),
        name="ln_in",
    )(x, meta, g, b)


def _side_cast_specs(side, linear_step):
    src, rb = side
    nblk = src.shape[0] // rb
    imap = lambda *ids: (jnp.minimum(linear_step(*ids), nblk - 1), 0)
    spec = pl.BlockSpec((rb, src.shape[1]), imap)
    return nblk, spec, jax.ShapeDtypeStruct(src.shape, BF16)


def _side_block_rows(rows, steps):
    for nblk in range(min(steps, rows // BF16_SUBLANES), 0, -1):
        if rows % nblk == 0 and (rows // nblk) % BF16_SUBLANES == 0:
            return rows // nblk
    return rows


def _side_cast(nblk, step, src_ref, dst_ref):
    @pl.when(step < nblk)
    def _():
        dst_ref[...] = src_ref[...].astype(BF16)


def _mm_kernel(nblk, a_ref, w_ref, *rest):
    if nblk is None:
        (o_ref,) = rest
    else:
        c_ref, o_ref, co_ref = rest
        _side_cast(nblk, pl.program_id(0) * pl.num_programs(1) + pl.program_id(1), c_ref, co_ref)
    o_ref[...] = jnp.dot(a_ref[...], w_ref[...], preferred_element_type=F32).astype(o_ref.dtype)


def _in_proj(h_bf, w_main, side=None):
    tm, tn = TM_INPROJ, TN_INPROJ
    grid = (NP // tn, TP // tm)
    in_specs = [
        pl.BlockSpec((tm, D), lambda j, i: (i, 0)),
        pl.BlockSpec((D, tn), lambda j, i: (0, j)),
    ]
    out_specs = [pl.BlockSpec((tm, tn), lambda j, i: (i, j))]
    out_shape = [jax.ShapeDtypeStruct((TP, NP), BF16)]
    args = [h_bf, w_main]
    nblk = None
    if side is not None:
        nblk, spec, shape = _side_cast_specs(side, lambda j, i: j * grid[1] + i)
        in_specs.append(spec)
        out_specs.append(spec)
        out_shape.append(shape)
        args.append(side[0])
    res = pl.pallas_call(
        functools.partial(_mm_kernel, nblk),
        grid=grid,
        in_specs=in_specs,
        out_specs=out_specs,
        out_shape=out_shape,
        compiler_params=_cp(("arbitrary", "arbitrary"), 56),
        name="in_proj",
    )(*args)
    return res if side is not None else (res[0], None)


def _dt_kernel(a_ref, w_ref, bias_ref, alog_ref, dt_ref, cum_ref):
    i = pl.program_id(0)
    raw = jnp.dot(a_ref[...], w_ref[...], preferred_element_type=F32) + bias_ref[...]
    dt = _softplus(raw)
    a_neg = -jnp.exp(alog_ref[...])
    row = lax.broadcasted_iota(jnp.int32, (CH, 1), 0)
    first = jnp.where(i % NCH == 0, PADF, 0)
    da = jnp.where(row >= first, dt * a_neg, 0.0)
    r = lax.broadcasted_iota(jnp.int32, (CH, CH), 0)
    c = lax.broadcasted_iota(jnp.int32, (CH, CH), 1)
    tril = (r >= c).astype(F32)
    cum = jnp.dot(tril, da, precision=lax.Precision.HIGHEST, preferred_element_type=F32)
    dt_ref[...] = dt
    cum_ref[...] = cum


def _dt_proj(h_bf, w_dt, dt_bias, a_log):
    return pl.pallas_call(
        _dt_kernel,
        grid=(TP // CH,),
        in_specs=[
            pl.BlockSpec((CH, D), lambda i: (i, 0)),
            pl.BlockSpec((D, LANES), lambda i: (0, 0)),
            pl.BlockSpec((1, LANES), lambda i: (0, 0)),
            pl.BlockSpec((1, LANES), lambda i: (0, 0)),
        ],
        out_specs=[
            pl.BlockSpec((CH, LANES), lambda i: (i, 0)),
            pl.BlockSpec((CH, LANES), lambda i: (i, 0)),
        ],
        out_shape=[jax.ShapeDtypeStruct((TP, LANES), F32), jax.ShapeDtypeStruct((TP, LANES), F32)],
        compiler_params=_cp(("arbitrary",), 16),
        name="dt_proj",
    )(h_bf, w_dt, dt_bias, a_log)


def _causal_conv(x, halo, w, k):
    acc = w[k - 1:k, :] * x
    for s in range(1, k):
        acc = acc + w[k - 1 - s:k - s, :] * pltpu.roll(x, s, 0)
    x8 = x[0:SUBLANES, :]
    h8 = halo[BF16_SUBLANES - SUBLANES:, :]
    row = lax.broadcasted_iota(jnp.int32, (SUBLANES, 1), 0)
    top = w[k - 1:k, :] * x8
    for s in range(1, k):
        shifted = jnp.where(row < s, pltpu.roll(h8, s, 0), pltpu.roll(x8, s, 0))
        top = top + w[k - 1 - s:k - s, :] * shifted
    return jnp.concatenate([top, acc[SUBLANES:, :]], axis=0)


def _mixa_kernel(c_ref, b_ref, v_ref, ch_ref, vh_ref, g_ref, cw_ref, wo_ref, bg_ref, o_ref):
    u = c_ref[...].astype(F32) * v_ref[...].astype(F32)
    hu = ch_ref[...].astype(F32) * vh_ref[...].astype(F32)
    conv = _causal_conv(u, hu, cw_ref[...], 3)
    lhs = (b_ref[...].astype(F32) * conv).astype(BF16)
    ya = jnp.dot(lhs, wo_ref[...], preferred_element_type=F32)
    ga = _sigmoid(g_ref[...].astype(F32) + bg_ref[...])
    o_ref[...] = (ga * ya).astype(BF16)


def _mixer_a(proj, conv_w, w_out_a, bgate_a):
    tm = 512
    hb = tm // BF16_SUBLANES
    halo = lambda col: (lambda i: (jnp.maximum(i * hb - 1, 0), col))
    return pl.pallas_call(
        _mixa_kernel,
        grid=(TP // tm,),
        in_specs=[
            pl.BlockSpec((tm, D), lambda i: (i, C0 // D)),
            pl.BlockSpec((tm, D), lambda i: (i, B0 // D)),
            pl.BlockSpec((tm, D), lambda i: (i, V0 // D)),
            pl.BlockSpec((BF16_SUBLANES, D), halo(C0 // D)),
            pl.BlockSpec((BF16_SUBLANES, D), halo(V0 // D)),
            pl.BlockSpec((tm, D), lambda i: (i, G0 // D)),
            pl.BlockSpec((3, D), lambda i: (0, 0)),
            pl.BlockSpec((D, D), lambda i: (0, 0), pipeline_mode=pl.Buffered(1)),
            pl.BlockSpec((1, D), lambda i: (0, 0)),
        ],
        out_specs=pl.BlockSpec((tm, D), lambda i: (i, 0)),
        out_shape=jax.ShapeDtypeStruct((TP, D), BF16),
        compiler_params=_cp(("arbitrary",), 56),
        name="mixer_a",
    )(proj, proj, proj, proj, proj, proj, conv_w, w_out_a, bgate_a)


def _ssd_kernel(xs_ref, bm_ref, cm_ref, z_ref, dt_ref, cum_ref,
                wx_ref, wb_ref, wc_ref, bx_ref, bb_ref, bc_ref, dsk_ref, nw_ref,
                o_ref, st_ref):
    g = pl.program_id(1)
    st_ref[...] = jnp.zeros_like(st_ref)

    rr = lax.broadcasted_iota(jnp.int32, (CH, CH), 0)
    cc = lax.broadcasted_iota(jnp.int32, (CH, CH), 1)
    shift_mats = [(rr - cc == s).astype(BF16) for s in range(1, SSD_CONV_K)]
    w_all = jnp.concatenate([wx_ref[...], wb_ref[...], wc_ref[...]], axis=1)
    b_all = jnp.concatenate([bx_ref[...], bb_ref[...], bc_ref[...]], axis=1)
    taps = [w_all[k:k + 1, :] for k in range(SSD_CONV_K)]
    lane_shift = (LANES - HPG * g) % LANES
    row = lax.broadcasted_iota(jnp.int32, (CH, 1), 0)
    row8 = lax.broadcasted_iota(jnp.int32, (SUBLANES, 1), 0)
    r2 = lax.broadcasted_iota(jnp.int32, (SUB, SUB), 0)
    c2 = lax.broadcasted_iota(jnp.int32, (SUB, SUB), 1)
    causal = r2 >= c2
    lane = lax.broadcasted_iota(jnp.int32, (1, LANES), 1)
    lo = lane < HEADDIM
    dskip = dsk_ref[...]
    norm_w = nw_ref[...]

    def chunk(c, carry):
        r0 = pl.multiple_of(c * CH, CH)
        h0 = pl.multiple_of(jnp.maximum(r0 - BF16_SUBLANES, 0), BF16_SUBLANES)
        rows = pl.ds(r0, CH)
        above = pl.ds(h0, BF16_SUBLANES)
        raw = jnp.concatenate([xs_ref[rows, :], bm_ref[rows, :], cm_ref[rows, :]], axis=1)
        halo = jnp.concatenate([xs_ref[above, :], bm_ref[above, :], cm_ref[above, :]], axis=1)
        h8 = halo[BF16_SUBLANES - SUBLANES:, :].astype(F32)

        acc = taps[SSD_CONV_K - 1] * raw.astype(F32) + b_all
        fix = jnp.zeros((SUBLANES, raw.shape[1]), F32)
        for s in range(1, SSD_CONV_K):
            tap = taps[SSD_CONV_K - 1 - s]
            acc = acc + tap * jnp.dot(shift_mats[s - 1], raw, preferred_element_type=F32)
            fix = fix + jnp.where(row8 < s, tap * pltpu.roll(h8, s, 0), 0.0)
        act = _silu(jnp.concatenate([acc[0:SUBLANES, :] + fix, acc[SUBLANES:, :]], axis=0))
        first = jnp.where(c == 0, PADF, 0)
        xs = jnp.where(row >= first, act[:, :GW], 0.0)
        bm = act[:, GW:GW + STATE]
        cm = act[:, GW + STATE:]

        dtg = pltpu.roll(dt_ref[rows, :], lane_shift, 1)
        cumg = pltpu.roll(cum_ref[rows, :], lane_shift, 1) * LOG2E
        bm_bf = bm.astype(BF16)
        cm_bf = cm.astype(BF16)
        xs_bf = xs.astype(BF16)

        st = st_ref[...]
        y_rows = []
        for q in range(CH // SUB):
            rs = slice(q * SUB, (q + 1) * SUB)
            cq = cumg[rs, :]
            if q > 0:
                cq = cq - cumg[q * SUB - 1:q * SUB, :]
            dq = dtg[rs, :]
            last = cq[SUB - 1:SUB, :]
            ecum = jnp.exp2(cq)
            wts = jnp.exp2(last - cq) * dq
            elast = jnp.exp2(last)
            src_t = (cq - jnp.log2(dq)).T

            cb = lax.dot_general(cm_bf[rs, :], bm_bf[rs, :], (((1,), (1,)), ((), ())),
                                 preferred_element_type=F32)
            cbm = jnp.where(causal, cb, 0.0)
            y_off = jnp.dot(cm_bf[rs, :], st.astype(BF16), preferred_element_type=F32)

            def head_mat(j):
                seg = cq[:, j:j + 1] - src_t[j:j + 1, :]
                return (cbm * jnp.exp2(jnp.minimum(seg, EXP2_CLAMP))).astype(BF16)

            ys, xws, els = [], [], []
            for p in range(GW // LANES):
                ja, jb = 2 * p, 2 * p + 1
                sl = slice(p * LANES, (p + 1) * LANES)
                xp = xs_bf[rs, sl]
                ya = jnp.dot(head_mat(ja), xp, preferred_element_type=F32)
                yb = jnp.dot(head_mat(jb), xp, preferred_element_type=F32)
                e_p = jnp.where(lo, ecum[:, ja:ja + 1], ecum[:, jb:jb + 1])
                ys.append(jnp.where(lo, ya, yb) + e_p * y_off[:, sl])
                w_p = jnp.where(lo, wts[:, ja:ja + 1], wts[:, jb:jb + 1])
                xws.append((xs[rs, sl] * w_p).astype(BF16))
                els.append(jnp.where(lo, elast[:, ja:ja + 1], elast[:, jb:jb + 1]))
            y_rows.append(jnp.concatenate(ys, axis=1))
            xw = jnp.concatenate(xws, axis=1)
            el = jnp.concatenate(els, axis=1)
            bm_t = bm[rs, :].T.astype(BF16)
            st = st * el + jnp.dot(bm_t, xw, preferred_element_type=F32)
        st_ref[...] = st
        y = jnp.concatenate(y_rows, axis=0)

        y = y + xs * dskip
        yg = y * _silu(z_ref[rows, :].astype(F32))
        ms = jnp.mean(yg * yg, axis=-1, keepdims=True)
        o_ref[rows, :] = (yg * lax.rsqrt(ms + RMS_EPS) * norm_w).astype(BF16)
        return carry

    lax.fori_loop(0, NCH, chunk, 0)


def _ssd(proj, dt, cum, conv_w, conv_b, dskip, norm_w):
    xcol = lambda g: X0 // GW + g
    bcol = lambda g: (X0 + D_INNER) // STATE + g
    ccol = lambda g: (X0 + D_INNER + GROUPS * STATE) // STATE + g
    zcol = lambda g: Z0 // GW + g
    wbcol = lambda g: D_INNER // STATE + g
    wccol = lambda g: (D_INNER + GROUPS * STATE) // STATE + g
    rows = lambda col_fn: (lambda b, g: (b, col_fn(g)))
    par = lambda col_fn: (lambda b, g: (0, col_fn(g)))
    return pl.pallas_call(
        _ssd_kernel,
        grid=(BATCH, GROUPS),
        in_specs=[
            pl.BlockSpec((LP, GW), rows(xcol)),
            pl.BlockSpec((LP, STATE), rows(bcol)),
            pl.BlockSpec((LP, STATE), rows(ccol)),
            pl.BlockSpec((LP, GW), rows(zcol)),
            pl.BlockSpec((LP, LANES), lambda b, g: (b, 0)),
            pl.BlockSpec((LP, LANES), lambda b, g: (b, 0)),
            pl.BlockSpec((SSD_CONV_K, GW), par(lambda g: g)),
            pl.BlockSpec((SSD_CONV_K, STATE), par(wbcol)),
            pl.BlockSpec((SSD_CONV_K, STATE), par(wccol)),
            pl.BlockSpec((1, GW), par(lambda g: g)),
            pl.BlockSpec((1, STATE), par(wbcol)),
            pl.BlockSpec((1, STATE), par(wccol)),
            pl.BlockSpec((1, GW), par(lambda g: g)),
            pl.BlockSpec((1, GW), par(lambda g: g)),
        ],
        out_specs=pl.BlockSpec((LP, GW), lambda b, g: (b, g)),
        out_shape=jax.ShapeDtypeStruct((TP, D_INNER), BF16),
        scratch_shapes=[pltpu.VMEM((STATE, GW), F32)],
        compiler_params=_cp(("arbitrary", "arbitrary"), 56),
        name="ssd",
    )(proj, proj, proj, proj, dt, cum,
      conv_w, conv_w, conv_w, conv_b, conv_b, conv_b, dskip, norm_w)


def _outb_kernel(y_ref, w_ref, g_ref, bg_ref, ya_ref, o_ref):
    for rs in _row_chunks(y_ref.shape[0]):
        yb = jnp.dot(y_ref[rs, :], w_ref[...], preferred_element_type=F32)
        gb = _sigmoid(g_ref[rs, :].astype(F32) + bg_ref[...])
        o_ref[rs, :] = (gb * yb + ya_ref[rs, :].astype(F32)).astype(BF16)


def _out_b(yn, w_out_b, proj, bgate_b, ga_ya):
    tm = 512
    return pl.pallas_call(
        _outb_kernel,
        grid=(TP // tm,),
        in_specs=[
            pl.BlockSpec((tm, D_INNER), lambda i: (i, 0)),
            pl.BlockSpec((D_INNER, D), lambda i: (0, 0), pipeline_mode=pl.Buffered(1)),
            pl.BlockSpec((tm, D), lambda i: (i, G0 // D + 1)),
            pl.BlockSpec((1, D), lambda i: (0, 0)),
            pl.BlockSpec((tm, D), lambda i: (i, 0)),
        ],
        out_specs=pl.BlockSpec((tm, D), lambda i: (i, 0)),
        out_shape=jax.ShapeDtypeStruct((TP, D), BF16),
        compiler_params=_cp(("arbitrary",), 56),
        name="out_b",
    )(yn, w_out_b, proj, bgate_b, ga_ya)


def _wo_kernel(m_ref, w_ref, h_ref, g_ref, b_ref, hf_ref, hb_ref):
    for rs in _row_chunks(m_ref.shape[0]):
        mix = jnp.dot(m_ref[rs, :], w_ref[...], preferred_element_type=F32)
        y = _layer_norm(ALPHA * h_ref[rs, :] + mix, g_ref[...], b_ref[...])
        hf_ref[rs, :] = y
        hb_ref[rs, :] = y.astype(BF16)


def _wo_ln(m, w_o, h, g, b):
    tm = 512
    return pl.pallas_call(
        _wo_kernel,
        grid=(TP // tm,),
        in_specs=[
            pl.BlockSpec((tm, D), lambda i: (i, 0)),
            pl.BlockSpec((D, D), lambda i: (0, 0), pipeline_mode=pl.Buffered(1)),
            pl.BlockSpec((tm, D), lambda i: (i, 0)),
            pl.BlockSpec((1, D), lambda i: (0, 0)),
            pl.BlockSpec((1, D), lambda i: (0, 0)),
        ],
        out_specs=[
            pl.BlockSpec((tm, D), lambda i: (i, 0)),
            pl.BlockSpec((tm, D), lambda i: (i, 0)),
        ],
        out_shape=[jax.ShapeDtypeStruct((TP, D), F32), jax.ShapeDtypeStruct((TP, D), BF16)],
        compiler_params=_cp(("arbitrary",), 56),
        name="wo_ln",
    )(m, w_o, h, g, b)


def _swiglu_step(x_bf, wg_ref, wu_ref, wd_ref):
    gate = jnp.dot(x_bf, wg_ref[...], preferred_element_type=F32)
    up = jnp.dot(x_bf, wu_ref[...], preferred_element_type=F32)
    mid = (_silu(gate) * up).astype(BF16)
    return jnp.dot(mid, wd_ref[...], preferred_element_type=F32)


def _ffn_gu_kernel(nblk, x_ref, wg_ref, wu_ref, *rest):
    if nblk is None:
        (o_ref,) = rest
    else:
        c_ref, o_ref, co_ref = rest
        _side_cast(nblk, pl.program_id(0) * pl.num_programs(1) + pl.program_id(1), c_ref, co_ref)
    for rs in _row_chunks(x_ref.shape[0]):
        x = x_ref[rs, :]
        gate = jnp.dot(x, wg_ref[...], preferred_element_type=F32)
        up = jnp.dot(x, wu_ref[...], preferred_element_type=F32)
        o_ref[rs, :] = (_silu(gate) * up).astype(BF16)


def _ffn_down_kernel(a_ref, w_ref, h_ref, g_ref, b_ref, hf_ref, hb_ref):
    valid = _valid_rows(pl.program_id(0), TM_FFN_DOWN)
    for rs in _row_chunks(TM_FFN_DOWN):
        f = jnp.dot(a_ref[rs, :], w_ref[...], preferred_element_type=F32)
        y = _layer_norm(ALPHA * h_ref[rs, :] + f, g_ref[...], b_ref[...])
        y = jnp.where(valid[rs, :], y, 0.0)
        hf_ref[rs, :] = y
        hb_ref[rs, :] = y.astype(BF16)


def _ffn_dense(h_bf, h_f, wg, wu, wd, g, b, side=None):
    tm, tn = TM_FFN, TN_FFN
    grid = (FF_DENSE // tn, TP // tm)
    in_specs = [
        pl.BlockSpec((tm, D), lambda j, i: (i, 0)),
        pl.BlockSpec((D, tn), lambda j, i: (0, j), pipeline_mode=pl.Buffered(1)),
        pl.BlockSpec((D, tn), lambda j, i: (0, j), pipeline_mode=pl.Buffered(1)),
    ]
    out_specs = [pl.BlockSpec((tm, tn), lambda j, i: (i, j))]
    out_shape = [jax.ShapeDtypeStruct((TP, FF_DENSE), BF16)]
    args = [h_bf, wg, wu]
    nblk = None
    if side is not None:
        nblk, spec, shape = _side_cast_specs(side, lambda j, i: j * grid[1] + i)
        in_specs.append(spec)
        out_specs.append(spec)
        out_shape.append(shape)
        args.append(side[0])
    res = pl.pallas_call(
        functools.partial(_ffn_gu_kernel, nblk),
        grid=grid,
        in_specs=in_specs,
        out_specs=out_specs,
        out_shape=out_shape,
        compiler_params=_cp(("arbitrary", "arbitrary"), 56),
        name="ffn_gate_up",
    )(*args)
    mid, side_out = (res[0], res[1]) if side is not None else (res[0], None)

    tmd = TM_FFN_DOWN
    h_f, h_bf = pl.pallas_call(
        _ffn_down_kernel,
        grid=(TP // tmd,),
        in_specs=[
            pl.BlockSpec((tmd, FF_DENSE), lambda i: (i, 0)),
            pl.BlockSpec((FF_DENSE, D), lambda i: (0, 0), pipeline_mode=pl.Buffered(1)),
            pl.BlockSpec((tmd, D), lambda i: (i, 0)),
            pl.BlockSpec((1, D), lambda i: (0, 0)),
            pl.BlockSpec((1, D), lambda i: (0, 0)),
        ],
        out_specs=[pl.BlockSpec((tmd, D), lambda i: (i, 0)), pl.BlockSpec((tmd, D), lambda i: (i, 0))],
        out_shape=[jax.ShapeDtypeStruct((TP, D), F32), jax.ShapeDtypeStruct((TP, D), BF16)],
        compiler_params=_cp(("arbitrary",), 56),
        name="ffn_down",
    )(mid, wd, h_f, g, b)
    return h_f, h_bf, side_out


def _router_kernel(h_ref, r_ref, meta_ref, cnt_ref, carry_ref):
    i = pl.program_id(0)
    tm = TOK_TILE

    @pl.when(i == 0)
    def _():
        carry_ref[...] = jnp.zeros_like(carry_ref)

    logits = jnp.dot(h_ref[...], r_ref[...], precision=lax.Precision.HIGHEST,
                     preferred_element_type=F32)
    lane = lax.broadcasted_iota(jnp.int32, (tm, LANES), 1).astype(F32)
    neg = jnp.float32(-jnp.inf)
    logits = jnp.where(lane < N_EXPERTS, logits, neg)
    m1 = jnp.max(logits, axis=-1, keepdims=True)
    i1 = jnp.min(jnp.where(logits == m1, lane, float(LANES)), axis=-1, keepdims=True)
    rest = jnp.where(lane == i1, neg, logits)
    m2 = jnp.max(rest, axis=-1, keepdims=True)
    i2 = jnp.min(jnp.where(rest == m2, lane, float(LANES)), axis=-1, keepdims=True)
    e = jnp.exp(m2 - m1)
    w1 = 1.0 / (1.0 + e)
    w2 = e / (1.0 + e)

    valid = _valid_rows(i, tm)
    sel = jnp.logical_and(jnp.logical_or(lane == i1, lane == i2), valid)
    onehot = sel.astype(F32)
    r = lax.broadcasted_iota(jnp.int32, (tm, tm), 0)
    c = lax.broadcasted_iota(jnp.int32, (tm, tm), 1)
    strict = (r > c).astype(BF16)
    before = jnp.dot(strict, onehot.astype(BF16), preferred_element_type=F32) + carry_ref[0:1, :]
    rank1 = jnp.sum(jnp.where(lane == i1, before, 0.0), axis=-1, keepdims=True)
    rank2 = jnp.sum(jnp.where(lane == i2, before, 0.0), axis=-1, keepdims=True)
    total = carry_ref[0:1, :] + jnp.sum(onehot, axis=0, keepdims=True)
    carry_ref[...] = jnp.broadcast_to(total, carry_ref.shape)
    cnt_ref[...] = jnp.broadcast_to(total, cnt_ref.shape)

    meta = jnp.where(lane == 0, i1, 0.0)
    meta = jnp.where(lane == 1, i2, meta)
    meta = jnp.where(lane == 2, w1, meta)
    meta = jnp.where(lane == 3, w2, meta)
    meta = jnp.where(lane == 4, rank1, meta)
    meta = jnp.where(lane == 5, rank2, meta)
    meta_ref[...] = meta


def _router(h_f, router_w):
    tm = TOK_TILE
    return pl.pallas_call(
        _router_kernel,
        grid=(TP // tm,),
        in_specs=[
            pl.BlockSpec((tm, D), lambda i: (i, 0)),
            pl.BlockSpec((D, LANES), lambda i: (0, 0)),
        ],
        out_specs=[
            pl.BlockSpec((tm, LANES), lambda i: (i, 0)),
            pl.BlockSpec((8, LANES), lambda i: (0, 0)),
        ],
        out_shape=[jax.ShapeDtypeStruct((TP, LANES), F32), jax.ShapeDtypeStruct((8, LANES), F32)],
        scratch_shapes=[pltpu.VMEM((8, LANES), F32)],
        compiler_params=_cp(("arbitrary",), 32),
        name="router",
    )(h_f, router_w)


def _gather_kernel(nu_ref, src_ref, h_ref, o_ref, buf, sem):
    i = pl.program_id(0)
    used = i < nu_ref[0]

    def copy(t):
        return pltpu.make_async_copy(h_ref.at[pl.ds(src_ref[0, 0, t], 1)], buf.at[pl.ds(t, 1)], sem)

    def start(t, carry):
        copy(t).start()
        return carry

    def wait(t, carry):
        copy(t).wait()
        return carry

    @pl.when(used)
    def _():
        lax.fori_loop(0, TM_MOE, start, 0, unroll=8)
        lax.fori_loop(0, TM_MOE, wait, 0, unroll=8)
        o_ref[...] = buf[...].astype(BF16)

    @pl.when(jnp.logical_not(used))
    def _():
        o_ref[...] = jnp.zeros_like(o_ref)


def _gather_rows(n_used, src, h_f):
    grid_spec = pltpu.PrefetchScalarGridSpec(
        num_scalar_prefetch=1,
        grid=(N_MOE_TILES,),
        in_specs=[
            pl.BlockSpec((1, 1, TM_MOE), lambda i, nu: (i, 0, 0), memory_space=pltpu.SMEM),
            pl.BlockSpec(memory_space=pl.ANY),
        ],
        out_specs=pl.BlockSpec((TM_MOE, D), lambda i, nu: (i, 0)),
        scratch_shapes=[pltpu.VMEM((TM_MOE, D), F32), pltpu.SemaphoreType.DMA(())],
    )
    return pl.pallas_call(
        _gather_kernel,
        grid_spec=grid_spec,
        out_shape=jax.ShapeDtypeStruct((P_MOE, D), BF16),
        compiler_params=_cp(("arbitrary",), 32),
        name="moe_gather",
    )(n_used, src, h_f)


def _ffn_moe_kernel(te_ref, nu_ref, x_ref, wg_ref, wu_ref, wd_ref, o_ref, acc_ref):
    del te_ref
    i = pl.program_id(0)
    f = pl.program_id(1)
    used = i < nu_ref[0]

    @pl.when(jnp.logical_and(used, f == 0))
    def _():
        acc_ref[...] = jnp.zeros_like(acc_ref)

    @pl.when(used)
    def _():
        acc_ref[...] += _swiglu_step(x_ref[...], wg_ref, wu_ref, wd_ref)

    @pl.when(jnp.logical_and(used, f == pl.num_programs(1) - 1))
    def _():
        o_ref[...] = acc_ref[...]

    @pl.when(jnp.logical_and(jnp.logical_not(used), f == 0))
    def _():
        o_ref[...] = jnp.zeros_like(o_ref)


def _ffn_moe(tile_e, n_used, xs, wg, wu, wd):
    tm, tf = TM_MOE, TF_MOE
    nf = FF_EXPERT // tf

    def fblk(i, f, nu):
        return jnp.where(i < nu[0], f, nf - 1)

    grid_spec = pltpu.PrefetchScalarGridSpec(
        num_scalar_prefetch=2,
        grid=(N_MOE_TILES, nf),
        in_specs=[
            pl.BlockSpec((tm, D), lambda i, f, te, nu: (i, 0)),
            pl.BlockSpec((None, D, tf), lambda i, f, te, nu: (te[i], 0, fblk(i, f, nu))),
            pl.BlockSpec((None, D, tf), lambda i, f, te, nu: (te[i], 0, fblk(i, f, nu))),
            pl.BlockSpec((None, tf, D), lambda i, f, te, nu: (te[i], fblk(i, f, nu), 0)),
        ],
        out_specs=pl.BlockSpec((tm, D), lambda i, f, te, nu: (i, 0)),
        scratch_shapes=[pltpu.VMEM((tm, D), F32)],
    )
    return pl.pallas_call(
        _ffn_moe_kernel,
        grid_spec=grid_spec,
        out_shape=jax.ShapeDtypeStruct((P_MOE, D), F32),
        compiler_params=_cp(("arbitrary", "arbitrary"), 58),
        name="ffn_moe",
    )(tile_e, n_used, xs, wg, wu, wd)


def _combine_kernel(final, pos_ref, y_ref, meta_ref, h_ref, g_ref, b_ref, *rest):
    if final:
        out_ref, buf1, buf2, sem = rest
    else:
        hf_ref, hb_ref, buf1, buf2, sem = rest
    i = pl.program_id(0)
    tm = TOK_TILE

    def copy(t, k):
        src = pos_ref[0, 0, 2 * t + k]
        buf = buf1 if k == 0 else buf2
        return pltpu.make_async_copy(y_ref.at[pl.ds(src, 1)], buf.at[pl.ds(t, 1)], sem)

    def start(t, carry):
        copy(t, 0).start()
        copy(t, 1).start()
        return carry

    def wait(t, carry):
        copy(t, 0).wait()
        copy(t, 1).wait()
        return carry

    lax.fori_loop(0, tm, start, 0, unroll=8)
    lax.fori_loop(0, tm, wait, 0, unroll=8)

    meta = meta_ref[...]
    w1 = meta[:, 2:3]
    w2 = meta[:, 3:4]
    f = w1 * buf1[...] + w2 * buf2[...]
    y = _layer_norm(ALPHA * h_ref[...] + f, g_ref[...], b_ref[...])
    if final:
        out_ref[0] = y
    else:
        y = jnp.where(_valid_rows(i, tm), y, 0.0)
        hf_ref[...] = y
        hb_ref[...] = y.astype(BF16)


def _combine(pos, y, meta, h_f, g, b, final):
    tm = TOK_TILE
    if final:
        out_specs = pl.BlockSpec((1, tm, D), lambda i: (i // NCH, jnp.maximum(i % NCH - 1, 0), 0))
        out_shape = jax.ShapeDtypeStruct((BATCH, SEQ, D), F32)
    else:
        out_specs = [pl.BlockSpec((tm, D), lambda i: (i, 0)), pl.BlockSpec((tm, D), lambda i: (i, 0))]
        out_shape = [jax.ShapeDtypeStruct((TP, D), F32), jax.ShapeDtypeStruct((TP, D), BF16)]
    return pl.pallas_call(
        functools.partial(_combine_kernel, final),
        grid=(TP // tm,),
        in_specs=[
            pl.BlockSpec((1, 1, 2 * tm), lambda i: (i, 0, 0), memory_space=pltpu.SMEM),
            pl.BlockSpec(memory_space=pl.ANY),
            pl.BlockSpec((tm, LANES), lambda i: (i, 0)),
            pl.BlockSpec((tm, D), lambda i: (i, 0)),
            pl.BlockSpec((1, D), lambda i: (0, 0)),
            pl.BlockSpec((1, D), lambda i: (0, 0)),
        ],
        out_specs=out_specs,
        out_shape=out_shape,
        scratch_shapes=[pltpu.VMEM((tm, D), F32), pltpu.VMEM((tm, D), F32), pltpu.SemaphoreType.DMA(())],
        compiler_params=_cp(("arbitrary",), 32),
        name="moe_combine",
    )(pos, y, meta, h_f, g, b)


def _pad_lanes(v, width=LANES):
    return jnp.pad(v, ((0, 0), (0, width - v.shape[1])))


def _token_mixer(h_f, h_bf, w_in, b_gate, conv_a_w, w_out_a, ssd_conv_w, ssd_conv_b, dt_bias, a_log,
                 d_skip, ssd_norm_w, w_out_b, w_o, ln_g, ln_b, side=None):
    dt0 = G0
    w_main = jnp.concatenate([w_in[:, :dt0], w_in[:, dt0 + HEADS:]], axis=1).astype(BF16)
    w_dt = _pad_lanes(w_in[:, dt0:dt0 + HEADS]).astype(BF16)
    proj, side_out = _in_proj(h_bf, w_main, side)
    dt, cum = _dt_proj(h_bf, w_dt, _pad_lanes(dt_bias[None, :]), _pad_lanes(a_log[None, :]))
    ga_ya = _mixer_a(proj, conv_a_w, w_out_a.astype(BF16), b_gate[None, :D])
    dskip = jnp.repeat(d_skip, HEADDIM)[None, :]
    yn = _ssd(proj, dt, cum, ssd_conv_w, ssd_conv_b[None, :], dskip, ssd_norm_w[None, :])
    m = _out_b(yn, w_out_b.astype(BF16), proj, b_gate[None, D:], ga_ya)
    h_f, h_bf = _wo_ln(m, w_o.astype(BF16), h_f, ln_g[None, :], ln_b[None, :])
    return h_f, h_bf, side_out


def _moe_layer(h_f, router_w, wg, wu, wd, ln_g, ln_b, final):
    meta, cnt = _router(h_f, _pad_lanes(router_w))
    i1 = meta[:, 0].astype(jnp.int32)
    i2 = meta[:, 1].astype(jnp.int32)
    counts = cnt[0, :N_EXPERTS].astype(jnp.int32)
    padded = ((counts + TM_MOE - 1) // TM_MOE) * TM_MOE
    ends = jnp.cumsum(padded)
    starts = ends - padded
    pos = jnp.stack([starts[i1] + meta[:, 4].astype(jnp.int32),
                     starts[i2] + meta[:, 5].astype(jnp.int32)], axis=1)
    pos = jnp.clip(pos, 0, P_MOE - 1)
    tok = jnp.arange(TP, dtype=jnp.int32)
    valid = (tok % LP) >= PADF
    dst = jnp.where(valid[:, None], pos, P_MOE).reshape(-1)
    src = jnp.zeros((P_MOE,), jnp.int32).at[dst].set(jnp.repeat(tok, 2), mode="drop")
    src = src.reshape(N_MOE_TILES, 1, TM_MOE)
    pos = pos.reshape(TP // TOK_TILE, 1, 2 * TOK_TILE)
    n_used = (ends[-1] // TM_MOE).reshape(1).astype(jnp.int32)
    tile_start = jnp.arange(N_MOE_TILES, dtype=jnp.int32) * TM_MOE
    tile_e = jnp.sum((tile_start[:, None] >= ends[None, :]).astype(jnp.int32), axis=1)
    last_e = jnp.minimum(tile_e[jnp.maximum(n_used[0] - 1, 0)], N_EXPERTS - 1)
    tile_e = jnp.where(tile_start < ends[-1], jnp.minimum(tile_e, N_EXPERTS - 1), last_e)
    xs = _gather_rows(n_used, src, h_f)
    y = _ffn_moe(tile_e, n_used, xs, wg, wu, wd)
    return _combine(pos, y, meta, h_f, ln_g[None, :], ln_b[None, :], final)


def kernel(x, meta_tokens, ln_in_g, ln_in_b, w_in, b_gate, conv_a_w, w_out_a, ssd_conv_w, ssd_conv_b,
           dt_bias, a_log, d_skip, ssd_norm_w, w_out_b, w_o, ln1_g, ln1_b, ffn_w_gate, ffn_w_up,
           ffn_w_down, router, moe_w_gate, moe_w_up, moe_w_down, ln2_g, ln2_b):
    h_f, h_bf = _ln_in(x, meta_tokens.astype(x.dtype), ln_in_g[None, :], ln_in_b[None, :])

    pending = [((i, k), w[i // 2]) for i in range(DEPTH) if i % 2 == 1
               for k, w in enumerate((moe_w_gate, moe_w_up, moe_w_down))]
    cast = {}

    def next_side(layer, steps):
        while pending and pending[0][0][0] < layer:
            pending.pop(0)
        if not pending:
            return None, None
        key, w = pending.pop(0)
        w2d = w.reshape(-1, w.shape[-1])
        return key, (w2d, _side_block_rows(w2d.shape[0], steps))

    for i in range(DEPTH):
        key, side = next_side(i, IN_PROJ_STEPS)
        h_f, h_bf, side_out = _token_mixer(h_f, h_bf, w_in[i], b_gate[i], conv_a_w[i], w_out_a[i],
                                           ssd_conv_w[i], ssd_conv_b[i], dt_bias[i], a_log[i], d_skip[i],
                                           ssd_norm_w[i], w_out_b[i], w_o[i], ln1_g[i], ln1_b[i], side)
        if key is not None:
            cast[key] = side_out
        j = i // 2
        if i % 2 == 0:
            key, side = next_side(i + 1, FFN_DENSE_STEPS)
            h_f, h_bf, side_out = _ffn_dense(h_bf, h_f, ffn_w_gate[j].astype(BF16), ffn_w_up[j].astype(BF16),
                                             ffn_w_down[j].astype(BF16), ln2_g[i][None, :], ln2_b[i][None, :],
                                             side)
            if key is not None:
                cast[key] = side_out
        else:
            final = i == DEPTH - 1
            wts = [cast[(i, k)].reshape(w[j].shape) if (i, k) in cast else w[j].astype(BF16)
                   for k, w in enumerate((moe_w_gate, moe_w_up, moe_w_down))]
            res = _moe_layer(h_f, router[j], wts[0], wts[1], wts[2], ln2_g[i], ln2_b[i], final)
            if final:
                return res
            h_f, h_bf = res
    return h_f.reshape(BATCH, LP, D)[:, PADF + N_META:, :]
```

```python
import functools

import jax
import jax.numpy as jnp
from jax import lax
from jax.experimental import pallas as pl
from jax.experimental.pallas import tpu as pltpu

F32 = jnp.float32
BF16 = jnp.bfloat16

D = 2048
BATCH = 4
SEQ = 4096
N_META = 16
DEPTH = 2
D_INNER = 2 * D
HEADDIM = 64
HEADS = D_INNER // HEADDIM
GROUPS = 8
HPG = HEADS // GROUPS
STATE = 128
GW = D_INNER // GROUPS
XBC = D_INNER + 2 * GROUPS * STATE
FF_DENSE = 5632
N_EXPERTS = 8
TOP_K = 2
FF_EXPERT = 7168
ALPHA = (2.0 * DEPTH) ** 0.25
LN_EPS = 1e-5
RMS_EPS = 1e-5
SSD_CONV_K = 4

CH = 256
PADF = CH - N_META
LP = PADF + N_META + SEQ
NCH = LP // CH
TP = BATCH * LP

C0, B0, V0 = 0, D, 2 * D
Z0 = 3 * D
X0 = Z0 + D_INNER
G0 = X0 + XBC
NP = G0 + 2 * D

LANES = 128
SUBLANES = 8
BF16_SUBLANES = 16

TM_MOE = 512
N_MOE_TILES = (2 * BATCH * (SEQ + N_META)) // TM_MOE + N_EXPERTS
P_MOE = N_MOE_TILES * TM_MOE
TF_MOE = 1024
TM_FFN = 256
TN_FFN = 2816
FFN_DENSE_STEPS = (TP // TM_FFN) * (FF_DENSE // TN_FFN)
TM_FFN_DOWN = 256
ROW_CHUNK = 128
TM_INPROJ = 1024
TN_INPROJ = 2048
IN_PROJ_STEPS = (NP // TN_INPROJ) * (TP // TM_INPROJ)
TOK_TILE = CH

SUB = 128
LOG2E = 1.4426950408889634
EXP2_CLAMP = 100.0

MIB = 1024 * 1024


def _cp(sem, vmem_mib):
    return pltpu.CompilerParams(dimension_semantics=sem, vmem_limit_bytes=int(vmem_mib * MIB))


def _sigmoid(x):
    return 0.5 * jnp.tanh(0.5 * x) + 0.5


def _silu(x):
    h = 0.5 * x
    return h + h * jnp.tanh(h)


def _softplus(x):
    return jnp.maximum(x, 0.0) + jnp.log1p(jnp.exp(-jnp.abs(x)))


def _layer_norm(x, g, b):
    mu = jnp.mean(x, axis=-1, keepdims=True)
    xc = x - mu
    var = jnp.mean(xc * xc, axis=-1, keepdims=True)
    return xc * lax.rsqrt(var + LN_EPS) * g + b


def _row_chunks(rows, size=ROW_CHUNK):
    return [slice(r, r + size) for r in range(0, rows, size)]


def _valid_rows(tile_idx, tm):
    row = lax.broadcasted_iota(jnp.int32, (tm, 1), 0)
    valid = None
    for s in range(tm // CH):
        chunk = tile_idx * (tm // CH) + s
        hi = jnp.where(chunk % NCH == 0, s * CH + PADF, s * CH)
        ok = jnp.logical_or(row < s * CH, row >= hi)
        valid = ok if valid is None else jnp.logical_and(valid, ok)
    return valid


def _ln_in_kernel(x_ref, meta_ref, g_ref, b_ref, hf_ref, hb_ref):
    j = pl.program_id(1)

    @pl.when(j == 0)
    def _():
        hf_ref[...] = jnp.zeros_like(hf_ref)
        hb_ref[...] = jnp.zeros_like(hb_ref)
        y = _layer_norm(meta_ref[...], g_ref[...], b_ref[...])
        hf_ref[PADF:, :] = y
        hb_ref[PADF:, :] = y.astype(BF16)

    @pl.when(j > 0)
    def _():
        y = _layer_norm(x_ref[0], g_ref[...], b_ref[...])
        hf_ref[...] = y
        hb_ref[...] = y.astype(BF16)


def _ln_in(x, meta, g, b):
    return pl.pallas_call(
        _ln_in_kernel,
        grid=(BATCH, NCH),
        in_specs=[
            pl.BlockSpec((1, CH, D), lambda bi, j: (bi, jnp.maximum(j - 1, 0), 0)),
            pl.BlockSpec((N_META, D), lambda bi, j: (0, 0)),
            pl.BlockSpec((1, D), lambda bi, j: (0, 0)),
            pl.BlockSpec((1, D), lambda bi, j: (0, 0)),
        ],
        out_specs=[
            pl.BlockSpec((CH, D), lambda bi, j: (bi * NCH + j, 0)),
            pl.BlockSpec((CH, D), lambda bi, j: (bi * NCH + j, 0)),
        ],
        out_shape=[jax.ShapeDtypeStruct((TP, D), F32), jax.ShapeDtypeStruct((TP, D), BF16)],
        compiler_params=_cp(("arbitrary", "arbitrary"), 32),
        name="ln_in",
    )(x, meta, g, b)


def _side_cast_specs(side, linear_step):
    src, rb = side
    nblk = src.shape[0] // rb
    imap = lambda *ids: (jnp.minimum(linear_step(*ids), nblk - 1), 0)
    spec = pl.BlockSpec((rb, src.shape[1]), imap)
    return nblk, spec, jax.ShapeDtypeStruct(src.shape, BF16)


def _side_block_rows(rows, steps):
    for nblk in range(min(steps, rows // BF16_SUBLANES), 0, -1):
        if rows % nblk == 0 and (rows // nblk) % BF16_SUBLANES == 0:
            return rows // nblk
    return rows


def _side_cast(nblk, step, src_ref, dst_ref):
    @pl.when(step < nblk)
    def _():
        dst_ref[...] = src_ref[...].astype(BF16)


def _mm_kernel(nblk, a_ref, w_ref, *rest):
    if nblk is None:
        (o_ref,) = rest
    else:
        c_ref, o_ref, co_ref = rest
        _side_cast(nblk, pl.program_id(0) * pl.num_programs(1) + pl.program_id(1), c_ref, co_ref)
    o_ref[...] = jnp.dot(a_ref[...], w_ref[...], preferred_element_type=F32).astype(o_ref.dtype)


def _in_proj(h_bf, w_main, side=None):
    tm, tn = TM_INPROJ, TN_INPROJ
    grid = (NP // tn, TP // tm)
    in_specs = [
        pl.BlockSpec((tm, D), lambda j, i: (i, 0)),
        pl.BlockSpec((D, tn), lambda j, i: (0, j)),
    ]
    out_specs = [pl.BlockSpec((tm, tn), lambda j, i: (i, j))]
    out_shape = [jax.ShapeDtypeStruct((TP, NP), BF16)]
    args = [h_bf, w_main]
    nblk = None
    if side is not None:
        nblk, spec, shape = _side_cast_specs(side, lambda j, i: j * grid[1] + i)
        in_specs.append(spec)
        out_specs.append(spec)
        out_shape.append(shape)
        args.append(side[0])
    res = pl.pallas_call(
        functools.partial(_mm_kernel, nblk),
        grid=grid,
        in_specs=in_specs,
        out_specs=out_specs,
        out_shape=out_shape,
        compiler_params=_cp(("arbitrary", "arbitrary"), 56),
        name="in_proj",
    )(*args)
    return res if side is not None else (res[0], None)


def _dt_kernel(a_ref, w_ref, bias_ref, alog_ref, dt_ref, cum_ref):
    i = pl.program_id(0)
    raw = jnp.dot(a_ref[...], w_ref[...], preferred_element_type=F32) + bias_ref[...]
    dt = _softplus(raw)
    a_neg = -jnp.exp(alog_ref[...])
    row = lax.broadcasted_iota(jnp.int32, (CH, 1), 0)
    first = jnp.where(i % NCH == 0, PADF, 0)
    da = jnp.where(row >= first, dt * a_neg, 0.0)
    r = lax.broadcasted_iota(jnp.int32, (CH, CH), 0)
    c = lax.broadcasted_iota(jnp.int32, (CH, CH), 1)
    tril = (r >= c).astype(F32)
    cum = jnp.dot(tril, da, precision=lax.Precision.HIGHEST, preferred_element_type=F32)
    dt_ref[...] = dt
    cum_ref[...] = cum


def _dt_proj(h_bf, w_dt, dt_bias, a_log):
    return pl.pallas_call(
        _dt_kernel,
        grid=(TP // CH,),
        in_specs=[
            pl.BlockSpec((CH, D), lambda i: (i, 0)),
            pl.BlockSpec((D, LANES), lambda i: (0, 0)),
            pl.BlockSpec((1, LANES), lambda i: (0, 0)),
            pl.BlockSpec((1, LANES), lambda i: (0, 0)),
        ],
        out_specs=[
            pl.BlockSpec((CH, LANES), lambda i: (i, 0)),
            pl.BlockSpec((CH, LANES), lambda i: (i, 0)),
        ],
        out_shape=[jax.ShapeDtypeStruct((TP, LANES), F32), jax.ShapeDtypeStruct((TP, LANES), F32)],
        compiler_params=_cp(("arbitrary",), 16),
        name="dt_proj",
    )(h_bf, w_dt, dt_bias, a_log)


def _causal_conv(x, halo, w, k):
    acc = w[k - 1:k, :] * x
    for s in range(1, k):
        acc = acc + w[k - 1 - s:k - s, :] * pltpu.roll(x, s, 0)
    x8 = x[0:SUBLANES, :]
    h8 = halo[BF16_SUBLANES - SUBLANES:, :]
    row = lax.broadcasted_iota(jnp.int32, (SUBLANES, 1), 0)
    top = w[k - 1:k, :] * x8
    for s in range(1, k):
        shifted = jnp.where(row < s, pltpu.roll(h8, s, 0), pltpu.roll(x8, s, 0))
        top = top + w[k - 1 - s:k - s, :] * shifted
    return jnp.concatenate([top, acc[SUBLANES:, :]], axis=0)


def _mixa_kernel(c_ref, b_ref, v_ref, ch_ref, vh_ref, g_ref, cw_ref, wo_ref, bg_ref, o_ref):
    u = c_ref[...].astype(F32) * v_ref[...].astype(F32)
    hu = ch_ref[...].astype(F32) * vh_ref[...].astype(F32)
    conv = _causal_conv(u, hu, cw_ref[...], 3)
    lhs = (b_ref[...].astype(F32) * conv).astype(BF16)
    ya = jnp.dot(lhs, wo_ref[...], preferred_element_type=F32)
    ga = _sigmoid(g_ref[...].astype(F32) + bg_ref[...])
    o_ref[...] = (ga * ya).astype(BF16)


def _mixer_a(proj, conv_w, w_out_a, bgate_a):
    tm = 512
    hb = tm // BF16_SUBLANES
    halo = lambda col: (lambda i: (jnp.maximum(i * hb - 1, 0), col))
    return pl.pallas_call(
        _mixa_kernel,
        grid=(TP // tm,),
        in_specs=[
            pl.BlockSpec((tm, D), lambda i: (i, C0 // D)),
            pl.BlockSpec((tm, D), lambda i: (i, B0 // D)),
            pl.BlockSpec((tm, D), lambda i: (i, V0 // D)),
            pl.BlockSpec((BF16_SUBLANES, D), halo(C0 // D)),
            pl.BlockSpec((BF16_SUBLANES, D), halo(V0 // D)),
            pl.BlockSpec((tm, D), lambda i: (i, G0 // D)),
            pl.BlockSpec((3, D), lambda i: (0, 0)),
            pl.BlockSpec((D, D), lambda i: (0, 0), pipeline_mode=pl.Buffered(1)),
            pl.BlockSpec((1, D), lambda i: (0, 0)),
        ],
        out_specs=pl.BlockSpec((tm, D), lambda i: (i, 0)),
        out_shape=jax.ShapeDtypeStruct((TP, D), BF16),
        compiler_params=_cp(("arbitrary",), 56),
        name="mixer_a",
    )(proj, proj, proj, proj, proj, proj, conv_w, w_out_a, bgate_a)


def _ssd_kernel(xs_ref, bm_ref, cm_ref, z_ref, dt_ref, cum_ref,
                wx_ref, wb_ref, wc_ref, bx_ref, bb_ref, bc_ref, dsk_ref, nw_ref,
                o_ref, st_ref):
    g = pl.program_id(1)
    st_ref[...] = jnp.zeros_like(st_ref)

    rr = lax.broadcasted_iota(jnp.int32, (CH, CH), 0)
    cc = lax.broadcasted_iota(jnp.int32, (CH, CH), 1)
    shift_mats = [(rr - cc == s).astype(BF16) for s in range(1, SSD_CONV_K)]
    w_all = jnp.concatenate([wx_ref[...], wb_ref[...], wc_ref[...]], axis=1)
    b_all = jnp.concatenate([bx_ref[...], bb_ref[...], bc_ref[...]], axis=1)
    taps = [w_all[k:k + 1, :] for k in range(SSD_CONV_K)]
    lane_shift = (LANES - HPG * g) % LANES
    row = lax.broadcasted_iota(jnp.int32, (CH, 1), 0)
    row8 = lax.broadcasted_iota(jnp.int32, (SUBLANES, 1), 0)
    r2 = lax.broadcasted_iota(jnp.int32, (SUB, SUB), 0)
    c2 = lax.broadcasted_iota(jnp.int32, (SUB, SUB), 1)
    causal = r2 >= c2
    lane = lax.broadcasted_iota(jnp.int32, (1, LANES), 1)
    lo = lane < HEADDIM
    dskip = dsk_ref[...]
    norm_w = nw_ref[...]

    def chunk(c, carry):
        r0 = pl.multiple_of(c * CH, CH)
        h0 = pl.multiple_of(jnp.maximum(r0 - BF16_SUBLANES, 0), BF16_SUBLANES)
        rows = pl.ds(r0, CH)
        above = pl.ds(h0, BF16_SUBLANES)
        raw = jnp.concatenate([xs_ref[rows, :], bm_ref[rows, :], cm_ref[rows, :]], axis=1)
        halo = jnp.concatenate([xs_ref[above, :], bm_ref[above, :], cm_ref[above, :]], axis=1)
        h8 = halo[BF16_SUBLANES - SUBLANES:, :].astype(F32)

        acc = taps[SSD_CONV_K - 1] * raw.astype(F32) + b_all
        fix = jnp.zeros((SUBLANES, raw.shape[1]), F32)
        for s in range(1, SSD_CONV_K):
            tap = taps[SSD_CONV_K - 1 - s]
            acc = acc + tap * jnp.dot(shift_mats[s - 1], raw, preferred_element_type=F32)
            fix = fix + jnp.where(row8 < s, tap * pltpu.roll(h8, s, 0), 0.0)
        act = _silu(jnp.concatenate([acc[0:SUBLANES, :] + fix, acc[SUBLANES:, :]], axis=0))
        first = jnp.where(c == 0, PADF, 0)
        xs = jnp.where(row >= first, act[:, :GW], 0.0)
        bm = act[:, GW:GW + STATE]
        cm = act[:, GW + STATE:]

        dtg = pltpu.roll(dt_ref[rows, :], lane_shift, 1)
        cumg = pltpu.roll(cum_ref[rows, :], lane_shift, 1) * LOG2E
        bm_bf = bm.astype(BF16)
        cm_bf = cm.astype(BF16)
        xs_bf = xs.astype(BF16)

        st = st_ref[...]
        y_rows = []
        for q in range(CH // SUB):
            rs = slice(q * SUB, (q + 1) * SUB)
            cq = cumg[rs, :]
            if q > 0:
                cq = cq - cumg[q * SUB - 1:q * SUB, :]
            dq = dtg[rs, :]
            last = cq[SUB - 1:SUB, :]
            ecum = jnp.exp2(cq)
            wts = jnp.exp2(last - cq) * dq
            elast = jnp.exp2(last)
            src_t = (cq - jnp.log2(dq)).T

            cb = lax.dot_general(cm_bf[rs, :], bm_bf[rs, :], (((1,), (1,)), ((), ())),
                                 preferred_element_type=F32)
            cbm = jnp.where(causal, cb, 0.0)
            y_off = jnp.dot(cm_bf[rs, :], st.astype(BF16), preferred_element_type=F32)

            def head_mat(j):
                seg = cq[:, j:j + 1] - src_t[j:j + 1, :]
                return (cbm * jnp.exp2(jnp.minimum(seg, EXP2_CLAMP))).astype(BF16)

            ys, xws, els = [], [], []
            for p in range(GW // LANES):
                ja, jb = 2 * p, 2 * p + 1
                sl = slice(p * LANES, (p + 1) * LANES)
                xp = xs_bf[rs, sl]
                ya = jnp.dot(head_mat(ja), xp, preferred_element_type=F32)
                yb = jnp.dot(head_mat(jb), xp, preferred_element_type=F32)
                e_p = jnp.where(lo, ecum[:, ja:ja + 1], ecum[:, jb:jb + 1])
                ys.append(jnp.where(lo, ya, yb) + e_p * y_off[:, sl])
                w_p = jnp.where(lo, wts[:, ja:ja + 1], wts[:, jb:jb + 1])
                xws.append((xs[rs, sl] * w_p).astype(BF16))
                els.append(jnp.where(lo, elast[:, ja:ja + 1], elast[:, jb:jb + 1]))
            y_rows.append(jnp.concatenate(ys, axis=1))
            xw = jnp.concatenate(xws, axis=1)
            el = jnp.concatenate(els, axis=1)
            bm_t = bm[rs, :].T.astype(BF16)
            st = st * el + jnp.dot(bm_t, xw, preferred_element_type=F32)
        st_ref[...] = st
        y = jnp.concatenate(y_rows, axis=0)

        y = y + xs * dskip
        yg = y * _silu(z_ref[rows, :].astype(F32))
        ms = jnp.mean(yg * yg, axis=-1, keepdims=True)
        o_ref[rows, :] = (yg * lax.rsqrt(ms + RMS_EPS) * norm_w).astype(BF16)
        return carry

    lax.fori_loop(0, NCH, chunk, 0)


def _ssd(proj, dt, cum, conv_w, conv_b, dskip, norm_w):
    xcol = lambda g: X0 // GW + g
    bcol = lambda g: (X0 + D_INNER) // STATE + g
    ccol = lambda g: (X0 + D_INNER + GROUPS * STATE) // STATE + g
    zcol = lambda g: Z0 // GW + g
    wbcol = lambda g: D_INNER // STATE + g
    wccol = lambda g: (D_INNER + GROUPS * STATE) // STATE + g
    rows = lambda col_fn: (lambda b, g: (b, col_fn(g)))
    par = lambda col_fn: (lambda b, g: (0, col_fn(g)))
    return pl.pallas_call(
        _ssd_kernel,
        grid=(BATCH, GROUPS),
        in_specs=[
            pl.BlockSpec((LP, GW), rows(xcol)),
            pl.BlockSpec((LP, STATE), rows(bcol)),
            pl.BlockSpec((LP, STATE), rows(ccol)),
            pl.BlockSpec((LP, GW), rows(zcol)),
            pl.BlockSpec((LP, LANES), lambda b, g: (b, 0)),
            pl.BlockSpec((LP, LANES), lambda b, g: (b, 0)),
            pl.BlockSpec((SSD_CONV_K, GW), par(lambda g: g)),
            pl.BlockSpec((SSD_CONV_K, STATE), par(wbcol)),
            pl.BlockSpec((SSD_CONV_K, STATE), par(wccol)),
            pl.BlockSpec((1, GW), par(lambda g: g)),
            pl.BlockSpec((1, STATE), par(wbcol)),
            pl.BlockSpec((1, STATE), par(wccol)),
            pl.BlockSpec((1, GW), par(lambda g: g)),
            pl.BlockSpec((1, GW), par(lambda g: g)),
        ],
        out_specs=pl.BlockSpec((LP, GW), lambda b, g: (b, g)),
        out_shape=jax.ShapeDtypeStruct((TP, D_INNER), BF16),
        scratch_shapes=[pltpu.VMEM((STATE, GW), F32)],
        compiler_params=_cp(("arbitrary", "arbitrary"), 56),
        name="ssd",
    )(proj, proj, proj, proj, dt, cum,
      conv_w, conv_w, conv_w, conv_b, conv_b, conv_b, dskip, norm_w)


def _outb_kernel(y_ref, w_ref, g_ref, bg_ref, ya_ref, o_ref):
    for rs in _row_chunks(y_ref.shape[0]):
        yb = jnp.dot(y_ref[rs, :], w_ref[...], preferred_element_type=F32)
        gb = _sigmoid(g_ref[rs, :].astype(F32) + bg_ref[...])
        o_ref[rs, :] = (gb * yb + ya_ref[rs, :].astype(F32)).astype(BF16)


def _out_b(yn, w_out_b, proj, bgate_b, ga_ya):
    tm = 512
    return pl.pallas_call(
        _outb_kernel,
        grid=(TP // tm,),
        in_specs=[
            pl.BlockSpec((tm, D_INNER), lambda i: (i, 0)),
            pl.BlockSpec((D_INNER, D), lambda i: (0, 0), pipeline_mode=pl.Buffered(1)),
            pl.BlockSpec((tm, D), lambda i: (i, G0 // D + 1)),
            pl.BlockSpec((1, D), lambda i: (0, 0)),
            pl.BlockSpec((tm, D), lambda i: (i, 0)),
        ],
        out_specs=pl.BlockSpec((tm, D), lambda i: (i, 0)),
        out_shape=jax.ShapeDtypeStruct((TP, D), BF16),
        compiler_params=_cp(("arbitrary",), 56),
        name="out_b",
    )(yn, w_out_b, proj, bgate_b, ga_ya)


def _wo_kernel(m_ref, w_ref, h_ref, g_ref, b_ref, hf_ref, hb_ref):
    for rs in _row_chunks(m_ref.shape[0]):
        mix = jnp.dot(m_ref[rs, :], w_ref[...], preferred_element_type=F32)
        y = _layer_norm(ALPHA * h_ref[rs, :] + mix, g_ref[...], b_ref[...])
        hf_ref[rs, :] = y
        hb_ref[rs, :] = y.astype(BF16)


def _wo_ln(m, w_o, h, g, b):
    tm = 512
    return pl.pallas_call(
        _wo_kernel,
        grid=(TP // tm,),
        in_specs=[
            pl.BlockSpec((tm, D), lambda i: (i, 0)),
            pl.BlockSpec((D, D), lambda i: (0, 0), pipeline_mode=pl.Buffered(1)),
            pl.BlockSpec((tm, D), lambda i: (i, 0)),
            pl.BlockSpec((1, D), lambda i: (0, 0)),
            pl.BlockSpec((1, D), lambda i: (0, 0)),
        ],
        out_specs=[
            pl.BlockSpec((tm, D), lambda i: (i, 0)),
            pl.BlockSpec((tm, D), lambda i: (i, 0)),
        ],
        out_shape=[jax.ShapeDtypeStruct((TP, D), F32), jax.ShapeDtypeStruct((TP, D), BF16)],
        compiler_params=_cp(("arbitrary",), 56),
        name="wo_ln",
    )(m, w_o, h, g, b)


def _swiglu_step(x_bf, wg_ref, wu_ref, wd_ref):
    gate = jnp.dot(x_bf, wg_ref[...], preferred_element_type=F32)
    up = jnp.dot(x_bf, wu_ref[...], preferred_element_type=F32)
    mid = (_silu(gate) * up).astype(BF16)
    return jnp.dot(mid, wd_ref[...], preferred_element_type=F32)


def _ffn_gu_kernel(nblk, x_ref, wg_ref, wu_ref, *rest):
    if nblk is None:
        (o_ref,) = rest
    else:
        c_ref, o_ref, co_ref = rest
        _side_cast(nblk, pl.program_id(0) * pl.num_programs(1) + pl.program_id(1), c_ref, co_ref)
    for rs in _row_chunks(x_ref.shape[0]):
        x = x_ref[rs, :]
        gate = jnp.dot(x, wg_ref[...], preferred_element_type=F32)
        up = jnp.dot(x, wu_ref[...], preferred_element_type=F32)
        o_ref[rs, :] = (_silu(gate) * up).astype(BF16)


def _ffn_down_kernel(a_ref, w_ref, h_ref, g_ref, b_ref, hf_ref, hb_ref):
    valid = _valid_rows(pl.program_id(0), TM_FFN_DOWN)
    for rs in _row_chunks(TM_FFN_DOWN):
        f = jnp.dot(a_ref[rs, :], w_ref[...], preferred_element_type=F32)
        y = _layer_norm(ALPHA * h_ref[rs, :] + f, g_ref[...], b_ref[...])
        y = jnp.where(valid[rs, :], y, 0.0)
        hf_ref[rs, :] = y
        hb_ref[rs, :] = y.astype(BF16)


def _ffn_dense(h_bf, h_f, wg, wu, wd, g, b, side=None):
    tm, tn = TM_FFN, TN_FFN
    grid = (FF_DENSE // tn, TP // tm)
    in_specs = [
        pl.BlockSpec((tm, D), lambda j, i: (i, 0)),
        pl.BlockSpec((D, tn), lambda j, i: (0, j), pipeline_mode=pl.Buffered(1)),
        pl.BlockSpec((D, tn), lambda j, i: (0, j), pipeline_mode=pl.Buffered(1)),
    ]
    out_specs = [pl.BlockSpec((tm, tn), lambda j, i: (i, j))]
    out_shape = [jax.ShapeDtypeStruct((TP, FF_DENSE), BF16)]
    args = [h_bf, wg, wu]
    nblk = None
    if side is not None:
        nblk, spec, shape = _side_cast_specs(side, lambda j, i: j * grid[1] + i)
        in_specs.append(spec)
        out_specs.append(spec)
        out_shape.append(shape)
        args.append(side[0])
    res = pl.pallas_call(
        functools.partial(_ffn_gu_kernel, nblk),
        grid=grid,
        in_specs=in_specs,
        out_specs=out_specs,
        out_shape=out_shape,
        compiler_params=_cp(("arbitrary", "arbitrary"), 56),
        name="ffn_gate_up",
    )(*args)
    mid, side_out = (res[0], res[1]) if side is not None else (res[0], None)

    tmd = TM_FFN_DOWN
    h_f, h_bf = pl.pallas_call(
        _ffn_down_kernel,
        grid=(TP // tmd,),
        in_specs=[
            pl.BlockSpec((tmd, FF_DENSE), lambda i: (i, 0)),
            pl.BlockSpec((FF_DENSE, D), lambda i: (0, 0), pipeline_mode=pl.Buffered(1)),
            pl.BlockSpec((tmd, D), lambda i: (i, 0)),
            pl.BlockSpec((1, D), lambda i: (0, 0)),
            pl.BlockSpec((1, D), lambda i: (0, 0)),
        ],
        out_specs=[pl.BlockSpec((tmd, D), lambda i: (i, 0)), pl.BlockSpec((tmd, D), lambda i: (i, 0))],
        out_shape=[jax.ShapeDtypeStruct((TP, D), F32), jax.ShapeDtypeStruct((TP, D), BF16)],
        compiler_params=_cp(("arbitrary",), 56),
        name="ffn_down",
    )(mid, wd, h_f, g, b)
    return h_f, h_bf, side_out


def _router_kernel(h_ref, r_ref, meta_ref, cnt_ref, carry_ref):
    i = pl.program_id(0)
    tm = TOK_TILE

    @pl.when(i == 0)
    def _():
        carry_ref[...] = jnp.zeros_like(carry_ref)

    logits = jnp.dot(h_ref[...], r_ref[...], precision=lax.Precision.HIGHEST,
                     preferred_element_type=F32)
    lane = lax.broadcasted_iota(jnp.int32, (tm, LANES), 1).astype(F32)
    neg = jnp.float32(-jnp.inf)
    logits = jnp.where(lane < N_EXPERTS, logits, neg)
    m1 = jnp.max(logits, axis=-1, keepdims=True)
    i1 = jnp.min(jnp.where(logits == m1, lane, float(LANES)), axis=-1, keepdims=True)
    rest = jnp.where(lane == i1, neg, logits)
    m2 = jnp.max(rest, axis=-1, keepdims=True)
    i2 = jnp.min(jnp.where(rest == m2, lane, float(LANES)), axis=-1, keepdims=True)
    e = jnp.exp(m2 - m1)
    w1 = 1.0 / (1.0 + e)
    w2 = e / (1.0 + e)

    valid = _valid_rows(i, tm)
    sel = jnp.logical_and(jnp.logical_or(lane == i1, lane == i2), valid)
    onehot = sel.astype(F32)
    r = lax.broadcasted_iota(jnp.int32, (tm, tm), 0)
    c = lax.broadcasted_iota(jnp.int32, (tm, tm), 1)
    strict = (r > c).astype(BF16)
    before = jnp.dot(strict, onehot.astype(BF16), preferred_element_type=F32) + carry_ref[0:1, :]
    rank1 = jnp.sum(jnp.where(lane == i1, before, 0.0), axis=-1, keepdims=True)
    rank2 = jnp.sum(jnp.where(lane == i2, before, 0.0), axis=-1, keepdims=True)
    total = carry_ref[0:1, :] + jnp.sum(onehot, axis=0, keepdims=True)
    carry_ref[...] = jnp.broadcast_to(total, carry_ref.shape)
    cnt_ref[...] = jnp.broadcast_to(total, cnt_ref.shape)

    meta = jnp.where(lane == 0, i1, 0.0)
    meta = jnp.where(lane == 1, i2, meta)
    meta = jnp.where(lane == 2, w1, meta)
    meta = jnp.where(lane == 3, w2, meta)
    meta = jnp.where(lane == 4, rank1, meta)
    meta = jnp.where(lane == 5, rank2, meta)
    meta_ref[...] = meta


def _router(h_f, router_w):
    tm = TOK_TILE
    return pl.pallas_call(
        _router_kernel,
        grid=(TP // tm,),
        in_specs=[
            pl.BlockSpec((tm, D), lambda i: (i, 0)),
            pl.BlockSpec((D, LANES), lambda i: (0, 0)),
        ],
        out_specs=[
            pl.BlockSpec((tm, LANES), lambda i: (i, 0)),
            pl.BlockSpec((8, LANES), lambda i: (0, 0)),
        ],
        out_shape=[jax.ShapeDtypeStruct((TP, LANES), F32), jax.ShapeDtypeStruct((8, LANES), F32)],
        scratch_shapes=[pltpu.VMEM((8, LANES), F32)],
        compiler_params=_cp(("arbitrary",), 32),
        name="router",
    )(h_f, router_w)


def _row_dma_ops(copy, count):
    def start(t, carry):
        copy(t).start()
        return carry

    def wait(t, carry):
        copy(t).wait()
        return carry

    return (lambda: lax.fori_loop(0, count, start, 0, unroll=8),
            lambda: lax.fori_loop(0, count, wait, 0, unroll=8))


def _gather_kernel(nu_ref, cur_ref, nxt_ref, h_ref, o_ref, buf, sem):
    i = pl.program_id(0)
    nu = nu_ref[0]
    slot = i % 2

    def copies(idx_ref, s):
        return _row_dma_ops(
            lambda t: pltpu.make_async_copy(h_ref.at[pl.ds(idx_ref[0, 0, t], 1)],
                                            buf.at[s, pl.ds(t, 1)], sem.at[s]), TM_MOE)

    @pl.when(jnp.logical_and(i == 0, nu > 0))
    def _():
        copies(cur_ref, 0)[0]()

    @pl.when(i + 1 < nu)
    def _():
        copies(nxt_ref, 1 - slot)[0]()

    @pl.when(i < nu)
    def _():
        copies(cur_ref, slot)[1]()
        o_ref[...] = buf[slot].astype(BF16)

    @pl.when(i >= nu)
    def _():
        o_ref[...] = jnp.zeros_like(o_ref)


def _gather_rows(n_used, src, h_f):
    last = N_MOE_TILES - 1
    grid_spec = pltpu.PrefetchScalarGridSpec(
        num_scalar_prefetch=1,
        grid=(N_MOE_TILES,),
        in_specs=[
            pl.BlockSpec((1, 1, TM_MOE), lambda i, nu: (i, 0, 0), memory_space=pltpu.SMEM),
            pl.BlockSpec((1, 1, TM_MOE), lambda i, nu: (jnp.minimum(i + 1, last), 0, 0),
                         memory_space=pltpu.SMEM),
            pl.BlockSpec(memory_space=pl.ANY),
        ],
        out_specs=pl.BlockSpec((TM_MOE, D), lambda i, nu: (i, 0)),
        scratch_shapes=[pltpu.VMEM((2, TM_MOE, D), F32), pltpu.SemaphoreType.DMA((2,))],
    )
    return pl.pallas_call(
        _gather_kernel,
        grid_spec=grid_spec,
        out_shape=jax.ShapeDtypeStruct((P_MOE, D), BF16),
        compiler_params=_cp(("arbitrary",), 32),
        name="moe_gather",
    )(n_used, src, src, h_f)


def _ffn_moe_kernel(te_ref, nu_ref, x_ref, wg_ref, wu_ref, wd_ref, o_ref):
    del te_ref
    i = pl.program_id(0)
    f = pl.program_id(1)

    @pl.when(f == 0)
    def _():
        o_ref[...] = jnp.zeros_like(o_ref)

    @pl.when(i < nu_ref[0])
    def _():
        for rs in _row_chunks(TM_MOE):
            o_ref[rs, :] += _swiglu_step(x_ref[rs, :], wg_ref, wu_ref, wd_ref)


def _ffn_moe(tile_e, n_used, xs, wg, wu, wd):
    tm, tf = TM_MOE, TF_MOE
    nf = FF_EXPERT // tf

    def fblk(i, f, nu):
        return jnp.where(i < nu[0], f, nf - 1)

    grid_spec = pltpu.PrefetchScalarGridSpec(
        num_scalar_prefetch=2,
        grid=(N_MOE_TILES, nf),
        in_specs=[
            pl.BlockSpec((tm, D), lambda i, f, te, nu: (i, 0)),
            pl.BlockSpec((None, D, tf), lambda i, f, te, nu: (te[i], 0, fblk(i, f, nu))),
            pl.BlockSpec((None, D, tf), lambda i, f, te, nu: (te[i], 0, fblk(i, f, nu))),
            pl.BlockSpec((None, tf, D), lambda i, f, te, nu: (te[i], fblk(i, f, nu), 0)),
        ],
        out_specs=pl.BlockSpec((tm, D), lambda i, f, te, nu: (i, 0)),
    )
    return pl.pallas_call(
        _ffn_moe_kernel,
        grid_spec=grid_spec,
        out_shape=jax.ShapeDtypeStruct((P_MOE, D), F32),
        compiler_params=_cp(("arbitrary", "arbitrary"), 58),
        name="ffn_moe",
    )(tile_e, n_used, xs, wg, wu, wd)


def _combine_kernel(final, cur_ref, nxt_ref, y_ref, meta_ref, h_ref, g_ref, b_ref, *rest):
    if final:
        out_ref, buf, sem = rest
    else:
        hf_ref, hb_ref, buf, sem = rest
    i = pl.program_id(0)
    tm = TOK_TILE
    slot = i % 2

    def copies(idx_ref, s):
        return _row_dma_ops(
            lambda u: pltpu.make_async_copy(y_ref.at[pl.ds(idx_ref[0, 0, u], 1)],
                                            buf.at[s, u % 2, pl.ds(u // 2, 1)], sem.at[s]), 2 * tm)

    @pl.when(i == 0)
    def _():
        copies(cur_ref, 0)[0]()

    @pl.when(i + 1 < pl.num_programs(0))
    def _():
        copies(nxt_ref, 1 - slot)[0]()

    copies(cur_ref, slot)[1]()

    meta = meta_ref[...]
    w1 = meta[:, 2:3]
    w2 = meta[:, 3:4]
    f = w1 * buf[slot, 0] + w2 * buf[slot, 1]
    y = _layer_norm(ALPHA * h_ref[...] + f, g_ref[...], b_ref[...])
    if final:
        out_ref[0] = y
    else:
        y = jnp.where(_valid_rows(i, tm), y, 0.0)
        hf_ref[...] = y
        hb_ref[...] = y.astype(BF16)


def _combine(pos, y, meta, h_f, g, b, final):
    tm = TOK_TILE
    if final:
        out_specs = pl.BlockSpec((1, tm, D), lambda i: (i // NCH, jnp.maximum(i % NCH - 1, 0), 0))
        out_shape = jax.ShapeDtypeStruct((BATCH, SEQ, D), F32)
    else:
        out_specs = [pl.BlockSpec((tm, D), lambda i: (i, 0)), pl.BlockSpec((tm, D), lambda i: (i, 0))]
        out_shape = [jax.ShapeDtypeStruct((TP, D), F32), jax.ShapeDtypeStruct((TP, D), BF16)]
    last = TP // tm - 1
    return pl.pallas_call(
        functools.partial(_combine_kernel, final),
        grid=(TP // tm,),
        in_specs=[
            pl.BlockSpec((1, 1, 2 * tm), lambda i: (i, 0, 0), memory_space=pltpu.SMEM),
            pl.BlockSpec((1, 1, 2 * tm), lambda i: (jnp.minimum(i + 1, last), 0, 0), memory_space=pltpu.SMEM),
            pl.BlockSpec(memory_space=pl.ANY),
            pl.BlockSpec((tm, LANES), lambda i: (i, 0)),
            pl.BlockSpec((tm, D), lambda i: (i, 0)),
            pl.BlockSpec((1, D), lambda i: (0, 0)),
            pl.BlockSpec((1, D), lambda i: (0, 0)),
        ],
        out_specs=out_specs,
        out_shape=out_shape,
        scratch_shapes=[pltpu.VMEM((2, TOP_K, tm, D), F32), pltpu.SemaphoreType.DMA((2,))],
        compiler_params=_cp(("arbitrary",), 32),
        name="moe_combine",
    )(pos, pos, y, meta, h_f, g, b)


def _pad_lanes(v, width=LANES):
    return jnp.pad(v, ((0, 0), (0, width - v.shape[1])))


def _token_mixer(h_f, h_bf, w_in, b_gate, conv_a_w, w_out_a, ssd_conv_w, ssd_conv_b, dt_bias, a_log,
                 d_skip, ssd_norm_w, w_out_b, w_o, ln_g, ln_b, side=None):
    dt0 = G0
    w_main = jnp.concatenate([w_in[:, :dt0], w_in[:, dt0 + HEADS:]], axis=1).astype(BF16)
    w_dt = _pad_lanes(w_in[:, dt0:dt0 + HEADS]).astype(BF16)
    proj, side_out = _in_proj(h_bf, w_main, side)
    dt, cum = _dt_proj(h_bf, w_dt, _pad_lanes(dt_bias[None, :]), _pad_lanes(a_log[None, :]))
    ga_ya = _mixer_a(proj, conv_a_w, w_out_a.astype(BF16), b_gate[None, :D])
    dskip = jnp.repeat(d_skip, HEADDIM)[None, :]
    yn = _ssd(proj, dt, cum, ssd_conv_w, ssd_conv_b[None, :], dskip, ssd_norm_w[None, :])
    m = _out_b(yn, w_out_b.astype(BF16), proj, b_gate[None, D:], ga_ya)
    h_f, h_bf = _wo_ln(m, w_o.astype(BF16), h_f, ln_g[None, :], ln_b[None, :])
    return h_f, h_bf, side_out


def _moe_layer(h_f, router_w, wg, wu, wd, ln_g, ln_b, final):
    meta, cnt = _router(h_f, _pad_lanes(router_w))
    i1 = meta[:, 0].astype(jnp.int32)
    i2 = meta[:, 1].astype(jnp.int32)
    counts = cnt[0, :N_EXPERTS].astype(jnp.int32)
    padded = ((counts + TM_MOE - 1) // TM_MOE) * TM_MOE
    ends = jnp.cumsum(padded)
    starts = ends - padded
    pos = jnp.stack([starts[i1] + meta[:, 4].astype(jnp.int32),
                     starts[i2] + meta[:, 5].astype(jnp.int32)], axis=1)
    pos = jnp.clip(pos, 0, P_MOE - 1)
    tok = jnp.arange(TP, dtype=jnp.int32)
    valid = (tok % LP) >= PADF
    dst = jnp.where(valid[:, None], pos, P_MOE).reshape(-1)
    src = jnp.zeros((P_MOE,), jnp.int32).at[dst].set(jnp.repeat(tok, 2), mode="drop")
    src = src.reshape(N_MOE_TILES, 1, TM_MOE)
    pos = pos.reshape(TP // TOK_TILE, 1, 2 * TOK_TILE)
    n_used = (ends[-1] // TM_MOE).reshape(1).astype(jnp.int32)
    tile_start = jnp.arange(N_MOE_TILES, dtype=jnp.int32) * TM_MOE
    tile_e = jnp.sum((tile_start[:, None] >= ends[None, :]).astype(jnp.int32), axis=1)
    last_e = jnp.minimum(tile_e[jnp.maximum(n_used[0] - 1, 0)], N_EXPERTS - 1)
    tile_e = jnp.where(tile_start < ends[-1], jnp.minimum(tile_e, N_EXPERTS - 1), last_e)
    xs = _gather_rows(n_used, src, h_f)
    y = _ffn_moe(tile_e, n_used, xs, wg, wu, wd)
    return _combine(pos, y, meta, h_f, ln_g[None, :], ln_b[None, :], final)


def kernel(x, meta_tokens, ln_in_g, ln_in_b, w_in, b_gate, conv_a_w, w_out_a, ssd_conv_w, ssd_conv_b,
           dt_bias, a_log, d_skip, ssd_norm_w, w_out_b, w_o, ln1_g, ln1_b, ffn_w_gate, ffn_w_up,
           ffn_w_down, router, moe_w_gate, moe_w_up, moe_w_down, ln2_g, ln2_b):
    h_f, h_bf = _ln_in(x, meta_tokens.astype(x.dtype), ln_in_g[None, :], ln_in_b[None, :])

    pending = [((i, k), w[i // 2]) for i in range(DEPTH) if i % 2 == 1
               for k, w in enumerate((moe_w_gate, moe_w_up, moe_w_down))]
    cast = {}

    def next_side(layer, steps):
        while pending and pending[0][0][0] < layer:
            pending.pop(0)
        if not pending:
            return None, None
        key, w = pending.pop(0)
        w2d = w.reshape(-1, w.shape[-1])
        return key, (w2d, _side_block_rows(w2d.shape[0], steps))

    for i in range(DEPTH):
        key, side = next_side(i, IN_PROJ_STEPS)
        h_f, h_bf, side_out = _token_mixer(h_f, h_bf, w_in[i], b_gate[i], conv_a_w[i], w_out_a[i],
                                           ssd_conv_w[i], ssd_conv_b[i], dt_bias[i], a_log[i], d_skip[i],
                                           ssd_norm_w[i], w_out_b[i], w_o[i], ln1_g[i], ln1_b[i], side)
        if key is not None:
            cast[key] = side_out
        j = i // 2
        if i % 2 == 0:
            key, side = next_side(i + 1, FFN_DENSE_STEPS)
            h_f, h_bf, side_out = _ffn_dense(h_bf, h_f, ffn_w_gate[j].astype(BF16), ffn_w_up[j].astype(BF16),
                                             ffn_w_down[j].astype(BF16), ln2_g[i][None, :], ln2_b[i][None, :],
                                             side)
            if key is not None:
                cast[key] = side_out
        else:
            final = i == DEPTH - 1
            wts = [cast[(i, k)].reshape(w[j].shape) if (i, k) in cast else w[j].astype(BF16)
                   for k, w in enumerate((moe_w_gate, moe_w_up, moe_w_down))]
            res = _moe_layer(h_f, router[j], wts[0], wts[1], wts[2], ln2_g[i], ln2_b[i], final)
            if final:
                return res
            h_f, h_bf = res
    return h_f.reshape(BATCH, LP, D)[:, PADF + N_META:, :]
```

```python
import functools

import jax
import jax.numpy as jnp
from jax import lax
from jax.experimental import pallas as pl
from jax.experimental.pallas import tpu as pltpu

F32 = jnp.float32
BF16 = jnp.bfloat16

D = 2048
BATCH = 4
SEQ = 4096
N_META = 16
DEPTH = 2
D_INNER = 2 * D
HEADDIM = 64
HEADS = D_INNER // HEADDIM
GROUPS = 8
HPG = HEADS // GROUPS
STATE = 128
GW = D_INNER // GROUPS
XBC = D_INNER + 2 * GROUPS * STATE
FF_DENSE = 5632
N_EXPERTS = 8
TOP_K = 2
FF_EXPERT = 7168
ALPHA = (2.0 * DEPTH) ** 0.25
LN_EPS = 1e-5
RMS_EPS = 1e-5
SSD_CONV_K = 4

CH = 256
PADF = CH - N_META
LP = PADF + N_META + SEQ
NCH = LP // CH
TP = BATCH * LP

C0, B0, V0 = 0, D, 2 * D
Z0 = 3 * D
X0 = Z0 + D_INNER
G0 = X0 + XBC
NP = G0 + 2 * D

LANES = 128
SUBLANES = 8
BF16_SUBLANES = 16

TM_MOE = 512
N_MOE_TILES = (2 * BATCH * (SEQ + N_META)) // TM_MOE + N_EXPERTS
P_MOE = N_MOE_TILES * TM_MOE
TF_MOE = 1024
TM_FFN = 256
TN_FFN = 2816
FFN_DENSE_STEPS = (TP // TM_FFN) * (FF_DENSE // TN_FFN)
TM_FFN_DOWN = 256
ROW_CHUNK = 128
TM_INPROJ = 1024
TN_INPROJ = 2048
IN_PROJ_STEPS = (NP // TN_INPROJ) * (TP // TM_INPROJ)
TOK_TILE = 256
COMBINE_TILE = CH
DT_ROWS = 256

SUB = 128
LOG2E = 1.4426950408889634
EXP2_CLAMP = 100.0

MIB = 1024 * 1024


def _cp(sem, vmem_mib):
    return pltpu.CompilerParams(dimension_semantics=sem, vmem_limit_bytes=int(vmem_mib * MIB))


def _sigmoid(x):
    return 0.5 * jnp.tanh(0.5 * x) + 0.5


def _silu(x):
    h = 0.5 * x
    return h + h * jnp.tanh(h)


def _softplus(x):
    return jnp.maximum(x, 0.0) + jnp.log1p(jnp.exp(-jnp.abs(x)))


def _layer_norm(x, g, b):
    mu = jnp.mean(x, axis=-1, keepdims=True)
    xc = x - mu
    var = jnp.mean(xc * xc, axis=-1, keepdims=True)
    return xc * lax.rsqrt(var + LN_EPS) * g + b


def _row_chunks(rows, size=ROW_CHUNK):
    return [slice(r, r + size) for r in range(0, rows, size)]


def _valid_rows(tile_idx, tm):
    row = lax.broadcasted_iota(jnp.int32, (tm, 1), 0)
    valid = None
    for s in range(tm // CH):
        chunk = tile_idx * (tm // CH) + s
        hi = jnp.where(chunk % NCH == 0, s * CH + PADF, s * CH)
        ok = jnp.logical_or(row < s * CH, row >= hi)
        valid = ok if valid is None else jnp.logical_and(valid, ok)
    return valid


def _ln_in_kernel(x_ref, meta_ref, g_ref, b_ref, hf_ref, hb_ref):
    j = pl.program_id(1)

    @pl.when(j == 0)
    def _():
        hf_ref[...] = jnp.zeros_like(hf_ref)
        hb_ref[...] = jnp.zeros_like(hb_ref)
        y = _layer_norm(meta_ref[...], g_ref[...], b_ref[...])
        hf_ref[PADF:, :] = y
        hb_ref[PADF:, :] = y.astype(BF16)

    @pl.when(j > 0)
    def _():
        y = _layer_norm(x_ref[0], g_ref[...], b_ref[...])
        hf_ref[...] = y
        hb_ref[...] = y.astype(BF16)


def _ln_in(x, meta, g, b):
    return pl.pallas_call(
        _ln_in_kernel,
        grid=(BATCH, NCH),
        in_specs=[
            pl.BlockSpec((1, CH, D), lambda bi, j: (bi, jnp.maximum(j - 1, 0), 0)),
            pl.BlockSpec((N_META, D), lambda bi, j: (0, 0)),
            pl.BlockSpec((1, D), lambda bi, j: (0, 0)),
            pl.BlockSpec((1, D), lambda bi, j: (0, 0)),
        ],
        out_specs=[
            pl.BlockSpec((CH, D), lambda bi, j: (bi * NCH + j, 0)),
            pl.BlockSpec((CH, D), lambda bi, j: (bi * NCH + j, 0)),
        ],
        out_shape=[jax.ShapeDtypeStruct((TP, D), F32), jax.ShapeDtypeStruct((TP, D), BF16)],
        compiler_params=_cp(("arbitrary", "arbitrary"), 32),
        name="ln_in",
    )(x, meta, g, b)


def _side_cast_specs(side, linear_step):
    src, rb = side
    nblk = src.shape[0] // rb
    imap = lambda *ids: (jnp.minimum(linear_step(*ids), nblk - 1), 0)
    spec = pl.BlockSpec((rb, src.shape[1]), imap)
    return nblk, spec, jax.ShapeDtypeStruct(src.shape, BF16)


def _side_block_rows(rows, steps):
    for nblk in range(min(steps, rows // BF16_SUBLANES), 0, -1):
        if rows % nblk == 0 and (rows // nblk) % BF16_SUBLANES == 0:
            return rows // nblk
    return rows


def _side_cast(nblk, step, src_ref, dst_ref):
    @pl.when(step < nblk)
    def _():
        dst_ref[...] = src_ref[...].astype(BF16)


def _mm_kernel(nblk, a_ref, w_ref, *rest):
    if nblk is None:
        (o_ref,) = rest
    else:
        c_ref, o_ref, co_ref = rest
        _side_cast(nblk, pl.program_id(0) * pl.num_programs(1) + pl.program_id(1), c_ref, co_ref)
    o_ref[...] = jnp.dot(a_ref[...], w_ref[...], preferred_element_type=F32).astype(o_ref.dtype)


def _in_proj(h_bf, w_main, side=None):
    tm, tn = TM_INPROJ, TN_INPROJ
    grid = (NP // tn, TP // tm)
    in_specs = [
        pl.BlockSpec((tm, D), lambda j, i: (i, 0)),
        pl.BlockSpec((D, tn), lambda j, i: (0, j)),
    ]
    out_specs = [pl.BlockSpec((tm, tn), lambda j, i: (i, j))]
    out_shape = [jax.ShapeDtypeStruct((TP, NP), BF16)]
    args = [h_bf, w_main]
    nblk = None
    if side is not None:
        nblk, spec, shape = _side_cast_specs(side, lambda j, i: j * grid[1] + i)
        in_specs.append(spec)
        out_specs.append(spec)
        out_shape.append(shape)
        args.append(side[0])
    res = pl.pallas_call(
        functools.partial(_mm_kernel, nblk),
        grid=grid,
        in_specs=in_specs,
        out_specs=out_specs,
        out_shape=out_shape,
        compiler_params=_cp(("arbitrary", "arbitrary"), 56),
        name="in_proj",
    )(*args)
    return res if side is not None else (res[0], None)


def _dt_kernel(a_ref, w_ref, bias_ref, alog_ref, dt_ref, cum_ref):
    i = pl.program_id(0)
    raw = jnp.dot(a_ref[...], w_ref[...], preferred_element_type=F32) + bias_ref[...]
    dt = _softplus(raw)
    a_neg = -jnp.exp(alog_ref[...])
    da = jnp.where(_valid_rows(i, DT_ROWS), dt * a_neg, 0.0)
    r = lax.broadcasted_iota(jnp.int32, (DT_ROWS, DT_ROWS), 0)
    c = lax.broadcasted_iota(jnp.int32, (DT_ROWS, DT_ROWS), 1)
    same_chunk = None
    for s in range(DT_ROWS // CH):
        blk = jnp.logical_and(jnp.logical_and(r >= s * CH, r < (s + 1) * CH), c >= s * CH)
        same_chunk = blk if same_chunk is None else jnp.logical_or(same_chunk, blk)
    tril = jnp.logical_and(r >= c, same_chunk).astype(F32)
    cum = jnp.dot(tril, da, precision=lax.Precision.HIGHEST, preferred_element_type=F32)
    dt_ref[...] = dt
    cum_ref[...] = cum


def _dt_proj(h_bf, w_dt, dt_bias, a_log):
    return pl.pallas_call(
        _dt_kernel,
        grid=(TP // DT_ROWS,),
        in_specs=[
            pl.BlockSpec((DT_ROWS, D), lambda i: (i, 0)),
            pl.BlockSpec((D, LANES), lambda i: (0, 0)),
            pl.BlockSpec((1, LANES), lambda i: (0, 0)),
            pl.BlockSpec((1, LANES), lambda i: (0, 0)),
        ],
        out_specs=[
            pl.BlockSpec((DT_ROWS, LANES), lambda i: (i, 0)),
            pl.BlockSpec((DT_ROWS, LANES), lambda i: (i, 0)),
        ],
        out_shape=[jax.ShapeDtypeStruct((TP, LANES), F32), jax.ShapeDtypeStruct((TP, LANES), F32)],
        compiler_params=_cp(("arbitrary",), 16),
        name="dt_proj",
    )(h_bf, w_dt, dt_bias, a_log)


def _causal_conv(x, halo, w, k):
    acc = w[k - 1:k, :] * x
    for s in range(1, k):
        acc = acc + w[k - 1 - s:k - s, :] * pltpu.roll(x, s, 0)
    x8 = x[0:SUBLANES, :]
    h8 = halo[BF16_SUBLANES - SUBLANES:, :]
    row = lax.broadcasted_iota(jnp.int32, (SUBLANES, 1), 0)
    top = w[k - 1:k, :] * x8
    for s in range(1, k):
        shifted = jnp.where(row < s, pltpu.roll(h8, s, 0), pltpu.roll(x8, s, 0))
        top = top + w[k - 1 - s:k - s, :] * shifted
    return jnp.concatenate([top, acc[SUBLANES:, :]], axis=0)


def _mixa_kernel(c_ref, b_ref, v_ref, ch_ref, vh_ref, g_ref, cw_ref, wo_ref, bg_ref, o_ref):
    u = c_ref[...].astype(F32) * v_ref[...].astype(F32)
    hu = ch_ref[...].astype(F32) * vh_ref[...].astype(F32)
    conv = _causal_conv(u, hu, cw_ref[...], 3)
    lhs = (b_ref[...].astype(F32) * conv).astype(BF16)
    ya = jnp.dot(lhs, wo_ref[...], preferred_element_type=F32)
    ga = _sigmoid(g_ref[...].astype(F32) + bg_ref[...])
    o_ref[...] = (ga * ya).astype(BF16)


def _mixer_a(proj, conv_w, w_out_a, bgate_a):
    tm = 512
    hb = tm // BF16_SUBLANES
    halo = lambda col: (lambda i: (jnp.maximum(i * hb - 1, 0), col))
    return pl.pallas_call(
        _mixa_kernel,
        grid=(TP // tm,),
        in_specs=[
            pl.BlockSpec((tm, D), lambda i: (i, C0 // D)),
            pl.BlockSpec((tm, D), lambda i: (i, B0 // D)),
            pl.BlockSpec((tm, D), lambda i: (i, V0 // D)),
            pl.BlockSpec((BF16_SUBLANES, D), halo(C0 // D)),
            pl.BlockSpec((BF16_SUBLANES, D), halo(V0 // D)),
            pl.BlockSpec((tm, D), lambda i: (i, G0 // D)),
            pl.BlockSpec((3, D), lambda i: (0, 0)),
            pl.BlockSpec((D, D), lambda i: (0, 0), pipeline_mode=pl.Buffered(1)),
            pl.BlockSpec((1, D), lambda i: (0, 0)),
        ],
        out_specs=pl.BlockSpec((tm, D), lambda i: (i, 0)),
        out_shape=jax.ShapeDtypeStruct((TP, D), BF16),
        compiler_params=_cp(("arbitrary",), 56),
        name="mixer_a",
    )(proj, proj, proj, proj, proj, proj, conv_w, w_out_a, bgate_a)


def _ssd_kernel(xs_ref, bm_ref, cm_ref, z_ref, dt_ref, cum_ref,
                wx_ref, wb_ref, wc_ref, bx_ref, bb_ref, bc_ref, dsk_ref, nw_ref,
                o_ref, st_ref):
    g = pl.program_id(1)
    st_ref[...] = jnp.zeros_like(st_ref)

    rr = lax.broadcasted_iota(jnp.int32, (CH, CH), 0)
    cc = lax.broadcasted_iota(jnp.int32, (CH, CH), 1)
    shift_mats = [(rr - cc == s).astype(BF16) for s in range(1, SSD_CONV_K)]
    w_all = jnp.concatenate([wx_ref[...], wb_ref[...], wc_ref[...]], axis=1)
    b_all = jnp.concatenate([bx_ref[...], bb_ref[...], bc_ref[...]], axis=1)
    taps = [w_all[k:k + 1, :] for k in range(SSD_CONV_K)]
    lane_shift = (LANES - HPG * g) % LANES
    row = lax.broadcasted_iota(jnp.int32, (CH, 1), 0)
    row8 = lax.broadcasted_iota(jnp.int32, (SUBLANES, 1), 0)
    r2 = lax.broadcasted_iota(jnp.int32, (SUB, SUB), 0)
    c2 = lax.broadcasted_iota(jnp.int32, (SUB, SUB), 1)
    causal = r2 >= c2
    lane = lax.broadcasted_iota(jnp.int32, (1, LANES), 1)
    lo = lane < HEADDIM
    dskip = dsk_ref[...]
    norm_w = nw_ref[...]

    def chunk(c, carry):
        r0 = pl.multiple_of(c * CH, CH)
        h0 = pl.multiple_of(jnp.maximum(r0 - BF16_SUBLANES, 0), BF16_SUBLANES)
        rows = pl.ds(r0, CH)
        above = pl.ds(h0, BF16_SUBLANES)
        raw = jnp.concatenate([xs_ref[rows, :], bm_ref[rows, :], cm_ref[rows, :]], axis=1)
        halo = jnp.concatenate([xs_ref[above, :], bm_ref[above, :], cm_ref[above, :]], axis=1)
        h8 = halo[BF16_SUBLANES - SUBLANES:, :].astype(F32)

        acc = taps[SSD_CONV_K - 1] * raw.astype(F32) + b_all
        fix = jnp.zeros((SUBLANES, raw.shape[1]), F32)
        for s in range(1, SSD_CONV_K):
            tap = taps[SSD_CONV_K - 1 - s]
            acc = acc + tap * jnp.dot(shift_mats[s - 1], raw, preferred_element_type=F32)
            fix = fix + jnp.where(row8 < s, tap * pltpu.roll(h8, s, 0), 0.0)
        act = _silu(jnp.concatenate([acc[0:SUBLANES, :] + fix, acc[SUBLANES:, :]], axis=0))
        first = jnp.where(c == 0, PADF, 0)
        xs = jnp.where(row >= first, act[:, :GW], 0.0)
        bm = act[:, GW:GW + STATE]
        cm = act[:, GW + STATE:]

        dtg = pltpu.roll(dt_ref[rows, :], lane_shift, 1)
        cumg = pltpu.roll(cum_ref[rows, :], lane_shift, 1) * LOG2E
        bm_bf = bm.astype(BF16)
        cm_bf = cm.astype(BF16)
        xs_bf = xs.astype(BF16)

        st = st_ref[...]
        y_rows = []
        for q in range(CH // SUB):
            rs = slice(q * SUB, (q + 1) * SUB)
            cq = cumg[rs, :]
            if q > 0:
                cq = cq - cumg[q * SUB - 1:q * SUB, :]
            dq = dtg[rs, :]
            last = cq[SUB - 1:SUB, :]
            ecum = jnp.exp2(cq)
            wts = jnp.exp2(last - cq) * dq
            elast = jnp.exp2(last)
            src_t = (cq - jnp.log2(dq)).T

            cb = lax.dot_general(cm_bf[rs, :], bm_bf[rs, :], (((1,), (1,)), ((), ())),
                                 preferred_element_type=F32)
            cbm = jnp.where(causal, cb, 0.0)
            y_off = jnp.dot(cm_bf[rs, :], st.astype(BF16), preferred_element_type=F32)

            def head_mat(j):
                seg = cq[:, j:j + 1] - src_t[j:j + 1, :]
                return (cbm * jnp.exp2(jnp.minimum(seg, EXP2_CLAMP))).astype(BF16)

            ys, xws, els = [], [], []
            for p in range(GW // LANES):
                ja, jb = 2 * p, 2 * p + 1
                sl = slice(p * LANES, (p + 1) * LANES)
                xp = xs_bf[rs, sl]
                ya = jnp.dot(head_mat(ja), xp, preferred_element_type=F32)
                yb = jnp.dot(head_mat(jb), xp, preferred_element_type=F32)
                e_p = jnp.where(lo, ecum[:, ja:ja + 1], ecum[:, jb:jb + 1])
                ys.append(jnp.where(lo, ya, yb) + e_p * y_off[:, sl])
                w_p = jnp.where(lo, wts[:, ja:ja + 1], wts[:, jb:jb + 1])
                xws.append((xs[rs, sl] * w_p).astype(BF16))
                els.append(jnp.where(lo, elast[:, ja:ja + 1], elast[:, jb:jb + 1]))
            y_rows.append(jnp.concatenate(ys, axis=1))
            xw = jnp.concatenate(xws, axis=1)
            el = jnp.concatenate(els, axis=1)
            bm_t = bm[rs, :].T.astype(BF16)
            st = st * el + jnp.dot(bm_t, xw, preferred_element_type=F32)
        st_ref[...] = st
        y = jnp.concatenate(y_rows, axis=0)

        y = y + xs * dskip
        yg = y * _silu(z_ref[rows, :].astype(F32))
        ms = jnp.mean(yg * yg, axis=-1, keepdims=True)
        o_ref[rows, :] = (yg * lax.rsqrt(ms + RMS_EPS) * norm_w).astype(BF16)
        return carry

    lax.fori_loop(0, NCH, chunk, 0)


def _ssd(proj, dt, cum, conv_w, conv_b, dskip, norm_w):
    xcol = lambda g: X0 // GW + g
    bcol = lambda g: (X0 + D_INNER) // STATE + g
    ccol = lambda g: (X0 + D_INNER + GROUPS * STATE) // STATE + g
    zcol = lambda g: Z0 // GW + g
    wbcol = lambda g: D_INNER // STATE + g
    wccol = lambda g: (D_INNER + GROUPS * STATE) // STATE + g
    rows = lambda col_fn: (lambda b, g: (b, col_fn(g)))
    par = lambda col_fn: (lambda b, g: (0, col_fn(g)))
    return pl.pallas_call(
        _ssd_kernel,
        grid=(BATCH, GROUPS),
        in_specs=[
            pl.BlockSpec((LP, GW), rows(xcol)),
            pl.BlockSpec((LP, STATE), rows(bcol)),
            pl.BlockSpec((LP, STATE), rows(ccol)),
            pl.BlockSpec((LP, GW), rows(zcol)),
            pl.BlockSpec((LP, LANES), lambda b, g: (b, 0)),
            pl.BlockSpec((LP, LANES), lambda b, g: (b, 0)),
            pl.BlockSpec((SSD_CONV_K, GW), par(lambda g: g)),
            pl.BlockSpec((SSD_CONV_K, STATE), par(wbcol)),
            pl.BlockSpec((SSD_CONV_K, STATE), par(wccol)),
            pl.BlockSpec((1, GW), par(lambda g: g)),
            pl.BlockSpec((1, STATE), par(wbcol)),
            pl.BlockSpec((1, STATE), par(wccol)),
            pl.BlockSpec((1, GW), par(lambda g: g)),
            pl.BlockSpec((1, GW), par(lambda g: g)),
        ],
        out_specs=pl.BlockSpec((LP, GW), lambda b, g: (b, g)),
        out_shape=jax.ShapeDtypeStruct((TP, D_INNER), BF16),
        scratch_shapes=[pltpu.VMEM((STATE, GW), F32)],
        compiler_params=_cp(("arbitrary", "arbitrary"), 56),
        name="ssd",
    )(proj, proj, proj, proj, dt, cum,
      conv_w, conv_w, conv_w, conv_b, conv_b, conv_b, dskip, norm_w)


def _outb_kernel(y_ref, w_ref, g_ref, bg_ref, ya_ref, o_ref):
    for rs in _row_chunks(y_ref.shape[0]):
        yb = jnp.dot(y_ref[rs, :], w_ref[...], preferred_element_type=F32)
        gb = _sigmoid(g_ref[rs, :].astype(F32) + bg_ref[...])
        o_ref[rs, :] = (gb * yb + ya_ref[rs, :].astype(F32)).astype(BF16)


def _out_b(yn, w_out_b, proj, bgate_b, ga_ya):
    tm = 512
    return pl.pallas_call(
        _outb_kernel,
        grid=(TP // tm,),
        in_specs=[
            pl.BlockSpec((tm, D_INNER), lambda i: (i, 0)),
            pl.BlockSpec((D_INNER, D), lambda i: (0, 0), pipeline_mode=pl.Buffered(1)),
            pl.BlockSpec((tm, D), lambda i: (i, G0 // D + 1)),
            pl.BlockSpec((1, D), lambda i: (0, 0)),
            pl.BlockSpec((tm, D), lambda i: (i, 0)),
        ],
        out_specs=pl.BlockSpec((tm, D), lambda i: (i, 0)),
        out_shape=jax.ShapeDtypeStruct((TP, D), BF16),
        compiler_params=_cp(("arbitrary",), 56),
        name="out_b",
    )(yn, w_out_b, proj, bgate_b, ga_ya)


def _wo_kernel(m_ref, w_ref, h_ref, g_ref, b_ref, hf_ref, hb_ref):
    for rs in _row_chunks(m_ref.shape[0]):
        mix = jnp.dot(m_ref[rs, :], w_ref[...], preferred_element_type=F32)
        y = _layer_norm(ALPHA * h_ref[rs, :] + mix, g_ref[...], b_ref[...])
        hf_ref[rs, :] = y
        hb_ref[rs, :] = y.astype(BF16)


def _wo_ln(m, w_o, h, g, b):
    tm = 512
    return pl.pallas_call(
        _wo_kernel,
        grid=(TP // tm,),
        in_specs=[
            pl.BlockSpec((tm, D), lambda i: (i, 0)),
            pl.BlockSpec((D, D), lambda i: (0, 0), pipeline_mode=pl.Buffered(1)),
            pl.BlockSpec((tm, D), lambda i: (i, 0)),
            pl.BlockSpec((1, D), lambda i: (0, 0)),
            pl.BlockSpec((1, D), lambda i: (0, 0)),
        ],
        out_specs=[
            pl.BlockSpec((tm, D), lambda i: (i, 0)),
            pl.BlockSpec((tm, D), lambda i: (i, 0)),
        ],
        out_shape=[jax.ShapeDtypeStruct((TP, D), F32), jax.ShapeDtypeStruct((TP, D), BF16)],
        compiler_params=_cp(("arbitrary",), 56),
        name="wo_ln",
    )(m, w_o, h, g, b)


def _swiglu_step(x_bf, wg_ref, wu_ref, wd_ref):
    gate = jnp.dot(x_bf, wg_ref[...], preferred_element_type=F32)
    up = jnp.dot(x_bf, wu_ref[...], preferred_element_type=F32)
    mid = (_silu(gate) * up).astype(BF16)
    return jnp.dot(mid, wd_ref[...], preferred_element_type=F32)


def _ffn_gu_kernel(nblk, x_ref, wg_ref, wu_ref, *rest):
    if nblk is None:
        (o_ref,) = rest
    else:
        c_ref, o_ref, co_ref = rest
        _side_cast(nblk, pl.program_id(0) * pl.num_programs(1) + pl.program_id(1), c_ref, co_ref)
    for rs in _row_chunks(x_ref.shape[0]):
        x = x_ref[rs, :]
        gate = jnp.dot(x, wg_ref[...], preferred_element_type=F32)
        up = jnp.dot(x, wu_ref[...], preferred_element_type=F32)
        o_ref[rs, :] = (_silu(gate) * up).astype(BF16)


def _ffn_down_kernel(a_ref, w_ref, h_ref, g_ref, b_ref, hf_ref, hb_ref):
    valid = _valid_rows(pl.program_id(0), TM_FFN_DOWN)
    for rs in _row_chunks(TM_FFN_DOWN):
        f = jnp.dot(a_ref[rs, :], w_ref[...], preferred_element_type=F32)
        y = _layer_norm(ALPHA * h_ref[rs, :] + f, g_ref[...], b_ref[...])
        y = jnp.where(valid[rs, :], y, 0.0)
        hf_ref[rs, :] = y
        hb_ref[rs, :] = y.astype(BF16)


def _ffn_dense(h_bf, h_f, wg, wu, wd, g, b, side=None):
    tm, tn = TM_FFN, TN_FFN
    grid = (FF_DENSE // tn, TP // tm)
    in_specs = [
        pl.BlockSpec((tm, D), lambda j, i: (i, 0)),
        pl.BlockSpec((D, tn), lambda j, i: (0, j), pipeline_mode=pl.Buffered(1)),
        pl.BlockSpec((D, tn), lambda j, i: (0, j), pipeline_mode=pl.Buffered(1)),
    ]
    out_specs = [pl.BlockSpec((tm, tn), lambda j, i: (i, j))]
    out_shape = [jax.ShapeDtypeStruct((TP, FF_DENSE), BF16)]
    args = [h_bf, wg, wu]
    nblk = None
    if side is not None:
        nblk, spec, shape = _side_cast_specs(side, lambda j, i: j * grid[1] + i)
        in_specs.append(spec)
        out_specs.append(spec)
        out_shape.append(shape)
        args.append(side[0])
    res = pl.pallas_call(
        functools.partial(_ffn_gu_kernel, nblk),
        grid=grid,
        in_specs=in_specs,
        out_specs=out_specs,
        out_shape=out_shape,
        compiler_params=_cp(("arbitrary", "arbitrary"), 56),
        name="ffn_gate_up",
    )(*args)
    mid, side_out = (res[0], res[1]) if side is not None else (res[0], None)

    tmd = TM_FFN_DOWN
    h_f, h_bf = pl.pallas_call(
        _ffn_down_kernel,
        grid=(TP // tmd,),
        in_specs=[
            pl.BlockSpec((tmd, FF_DENSE), lambda i: (i, 0)),
            pl.BlockSpec((FF_DENSE, D), lambda i: (0, 0), pipeline_mode=pl.Buffered(1)),
            pl.BlockSpec((tmd, D), lambda i: (i, 0)),
            pl.BlockSpec((1, D), lambda i: (0, 0)),
            pl.BlockSpec((1, D), lambda i: (0, 0)),
        ],
        out_specs=[pl.BlockSpec((tmd, D), lambda i: (i, 0)), pl.BlockSpec((tmd, D), lambda i: (i, 0))],
        out_shape=[jax.ShapeDtypeStruct((TP, D), F32), jax.ShapeDtypeStruct((TP, D), BF16)],
        compiler_params=_cp(("arbitrary",), 56),
        name="ffn_down",
    )(mid, wd, h_f, g, b)
    return h_f, h_bf, side_out


def _router_kernel(h_ref, r_ref, meta_ref, cnt_ref, carry_ref):
    i = pl.program_id(0)
    tm = TOK_TILE

    @pl.when(i == 0)
    def _():
        carry_ref[...] = jnp.zeros_like(carry_ref)

    logits = jnp.dot(h_ref[...], r_ref[...], precision=lax.Precision.HIGHEST,
                     preferred_element_type=F32)
    lane = lax.broadcasted_iota(jnp.int32, (tm, LANES), 1).astype(F32)
    neg = jnp.float32(-jnp.inf)
    logits = jnp.where(lane < N_EXPERTS, logits, neg)
    m1 = jnp.max(logits, axis=-1, keepdims=True)
    i1 = jnp.min(jnp.where(logits == m1, lane, float(LANES)), axis=-1, keepdims=True)
    rest = jnp.where(lane == i1, neg, logits)
    m2 = jnp.max(rest, axis=-1, keepdims=True)
    i2 = jnp.min(jnp.where(rest == m2, lane, float(LANES)), axis=-1, keepdims=True)
    e = jnp.exp(m2 - m1)
    w1 = 1.0 / (1.0 + e)
    w2 = e / (1.0 + e)

    valid = _valid_rows(i, tm)
    sel = jnp.logical_and(jnp.logical_or(lane == i1, lane == i2), valid)
    onehot = sel.astype(F32)
    r = lax.broadcasted_iota(jnp.int32, (tm, tm), 0)
    c = lax.broadcasted_iota(jnp.int32, (tm, tm), 1)
    strict = (r > c).astype(BF16)
    before = jnp.dot(strict, onehot.astype(BF16), preferred_element_type=F32) + carry_ref[0:1, :]
    rank1 = jnp.sum(jnp.where(lane == i1, before, 0.0), axis=-1, keepdims=True)
    rank2 = jnp.sum(jnp.where(lane == i2, before, 0.0), axis=-1, keepdims=True)
    total = carry_ref[0:1, :] + jnp.sum(onehot, axis=0, keepdims=True)
    carry_ref[...] = jnp.broadcast_to(total, carry_ref.shape)
    cnt_ref[...] = jnp.broadcast_to(total, cnt_ref.shape)

    meta = jnp.where(lane == 0, i1, 0.0)
    meta = jnp.where(lane == 1, i2, meta)
    meta = jnp.where(lane == 2, w1, meta)
    meta = jnp.where(lane == 3, w2, meta)
    meta = jnp.where(lane == 4, rank1, meta)
    meta = jnp.where(lane == 5, rank2, meta)
    meta_ref[...] = meta


def _router(h_f, router_w):
    tm = TOK_TILE
    return pl.pallas_call(
        _router_kernel,
        grid=(TP // tm,),
        in_specs=[
            pl.BlockSpec((tm, D), lambda i: (i, 0)),
            pl.BlockSpec((D, LANES), lambda i: (0, 0)),
        ],
        out_specs=[
            pl.BlockSpec((tm, LANES), lambda i: (i, 0)),
            pl.BlockSpec((8, LANES), lambda i: (0, 0)),
        ],
        out_shape=[jax.ShapeDtypeStruct((TP, LANES), F32), jax.ShapeDtypeStruct((8, LANES), F32)],
        scratch_shapes=[pltpu.VMEM((8, LANES), F32)],
        compiler_params=_cp(("arbitrary",), 32),
        name="router",
    )(h_f, router_w)


def _row_dma_ops(copy, count):
    def start(t, carry):
        copy(t).start()
        return carry

    def wait(t, carry):
        copy(t).wait()
        return carry

    return (lambda: lax.fori_loop(0, count, start, 0, unroll=8),
            lambda: lax.fori_loop(0, count, wait, 0, unroll=8))


def _gather_kernel(nu_ref, cur_ref, nxt_ref, h_ref, o_ref, buf, sem):
    i = pl.program_id(0)
    nu = nu_ref[0]
    slot = i % 2

    def copies(idx_ref, s):
        return _row_dma_ops(
            lambda t: pltpu.make_async_copy(h_ref.at[pl.ds(idx_ref[0, 0, t], 1)],
                                            buf.at[s, pl.ds(t, 1)], sem.at[s]), TM_MOE)

    @pl.when(jnp.logical_and(i == 0, nu > 0))
    def _():
        copies(cur_ref, 0)[0]()

    @pl.when(i + 1 < nu)
    def _():
        copies(nxt_ref, 1 - slot)[0]()

    @pl.when(i < nu)
    def _():
        copies(cur_ref, slot)[1]()
        o_ref[...] = buf[slot].astype(BF16)

    @pl.when(i >= nu)
    def _():
        o_ref[...] = jnp.zeros_like(o_ref)


def _gather_rows(n_used, src, h_f):
    last = N_MOE_TILES - 1
    grid_spec = pltpu.PrefetchScalarGridSpec(
        num_scalar_prefetch=1,
        grid=(N_MOE_TILES,),
        in_specs=[
            pl.BlockSpec((1, 1, TM_MOE), lambda i, nu: (i, 0, 0), memory_space=pltpu.SMEM),
            pl.BlockSpec((1, 1, TM_MOE), lambda i, nu: (jnp.minimum(i + 1, last), 0, 0),
                         memory_space=pltpu.SMEM),
            pl.BlockSpec(memory_space=pl.ANY),
        ],
        out_specs=pl.BlockSpec((TM_MOE, D), lambda i, nu: (i, 0)),
        scratch_shapes=[pltpu.VMEM((2, TM_MOE, D), F32), pltpu.SemaphoreType.DMA((2,))],
    )
    return pl.pallas_call(
        _gather_kernel,
        grid_spec=grid_spec,
        out_shape=jax.ShapeDtypeStruct((P_MOE, D), BF16),
        compiler_params=_cp(("arbitrary",), 32),
        name="moe_gather",
    )(n_used, src, src, h_f)


def _ffn_moe_kernel(te_ref, nu_ref, x_ref, wg_ref, wu_ref, wd_ref, o_ref, acc_ref):
    del te_ref
    i = pl.program_id(0)
    f = pl.program_id(1)
    used = i < nu_ref[0]

    @pl.when(jnp.logical_and(used, f == 0))
    def _():
        acc_ref[...] = jnp.zeros_like(acc_ref)

    @pl.when(used)
    def _():
        acc_ref[...] += _swiglu_step(x_ref[...], wg_ref, wu_ref, wd_ref)

    @pl.when(jnp.logical_and(used, f == pl.num_programs(1) - 1))
    def _():
        o_ref[...] = acc_ref[...]

    @pl.when(jnp.logical_and(jnp.logical_not(used), f == 0))
    def _():
        o_ref[...] = jnp.zeros_like(o_ref)


def _ffn_moe(tile_e, n_used, xs, wg, wu, wd):
    tm, tf = TM_MOE, TF_MOE
    nf = FF_EXPERT // tf

    def fblk(i, f, nu):
        return jnp.where(i < nu[0], f, nf - 1)

    grid_spec = pltpu.PrefetchScalarGridSpec(
        num_scalar_prefetch=2,
        grid=(N_MOE_TILES, nf),
        in_specs=[
            pl.BlockSpec((tm, D), lambda i, f, te, nu: (i, 0)),
            pl.BlockSpec((None, D, tf), lambda i, f, te, nu: (te[i], 0, fblk(i, f, nu))),
            pl.BlockSpec((None, D, tf), lambda i, f, te, nu: (te[i], 0, fblk(i, f, nu))),
            pl.BlockSpec((None, tf, D), lambda i, f, te, nu: (te[i], fblk(i, f, nu), 0)),
        ],
        out_specs=pl.BlockSpec((tm, D), lambda i, f, te, nu: (i, 0)),
        scratch_shapes=[pltpu.VMEM((tm, D), F32)],
    )
    return pl.pallas_call(
        _ffn_moe_kernel,
        grid_spec=grid_spec,
        out_shape=jax.ShapeDtypeStruct((P_MOE, D), F32),
        compiler_params=_cp(("arbitrary", "arbitrary"), 58),
        name="ffn_moe",
    )(tile_e, n_used, xs, wg, wu, wd)


def _combine_kernel(final, cur_ref, nxt_ref, y_ref, meta_ref, h_ref, g_ref, b_ref, *rest):
    if final:
        out_ref, buf, sem = rest
    else:
        hf_ref, hb_ref, buf, sem = rest
    i = pl.program_id(0)
    tm = COMBINE_TILE
    slot = i % 2

    def copies(idx_ref, s):
        def copy(t, k):
            return pltpu.make_async_copy(y_ref.at[pl.ds(idx_ref[0, 0, TOP_K * t + k], 1)],
                                         buf.at[s, k, pl.ds(t, 1)], sem.at[s])

        def start(t, carry):
            for k in range(TOP_K):
                copy(t, k).start()
            return carry

        def wait(t, carry):
            for k in range(TOP_K):
                copy(t, k).wait()
            return carry

        return (lambda: lax.fori_loop(0, tm, start, 0, unroll=8),
                lambda: lax.fori_loop(0, tm, wait, 0, unroll=8))

    @pl.when(i == 0)
    def _():
        copies(cur_ref, 0)[0]()

    @pl.when(i + 1 < pl.num_programs(0))
    def _():
        copies(nxt_ref, 1 - slot)[0]()

    copies(cur_ref, slot)[1]()

    meta = meta_ref[...]
    w1 = meta[:, 2:3]
    w2 = meta[:, 3:4]
    f = w1 * buf[slot, 0] + w2 * buf[slot, 1]
    y = _layer_norm(ALPHA * h_ref[...] + f, g_ref[...], b_ref[...])
    if final:
        out_ref[0] = y
    else:
        y = jnp.where(_valid_rows(i, tm), y, 0.0)
        hf_ref[...] = y
        hb_ref[...] = y.astype(BF16)


def _combine(pos, y, meta, h_f, g, b, final):
    tm = COMBINE_TILE
    if final:
        out_specs = pl.BlockSpec((1, tm, D), lambda i: (i // NCH, jnp.maximum(i % NCH - 1, 0), 0))
        out_shape = jax.ShapeDtypeStruct((BATCH, SEQ, D), F32)
    else:
        out_specs = [pl.BlockSpec((tm, D), lambda i: (i, 0)), pl.BlockSpec((tm, D), lambda i: (i, 0))]
        out_shape = [jax.ShapeDtypeStruct((TP, D), F32), jax.ShapeDtypeStruct((TP, D), BF16)]
    last = TP // tm - 1
    return pl.pallas_call(
        functools.partial(_combine_kernel, final),
        grid=(TP // tm,),
        in_specs=[
            pl.BlockSpec((1, 1, 2 * tm), lambda i: (i, 0, 0), memory_space=pltpu.SMEM),
            pl.BlockSpec((1, 1, 2 * tm), lambda i: (jnp.minimum(i + 1, last), 0, 0), memory_space=pltpu.SMEM),
            pl.BlockSpec(memory_space=pl.ANY),
            pl.BlockSpec((tm, LANES), lambda i: (i, 0)),
            pl.BlockSpec((tm, D), lambda i: (i, 0)),
            pl.BlockSpec((1, D), lambda i: (0, 0)),
            pl.BlockSpec((1, D), lambda i: (0, 0)),
        ],
        out_specs=out_specs,
        out_shape=out_shape,
        scratch_shapes=[pltpu.VMEM((2, TOP_K, tm, D), F32), pltpu.SemaphoreType.DMA((2,))],
        compiler_params=_cp(("arbitrary",), 32),
        name="moe_combine",
    )(pos, pos, y, meta, h_f, g, b)


def _pad_lanes(v, width=LANES):
    return jnp.pad(v, ((0, 0), (0, width - v.shape[1])))


def _token_mixer(h_f, h_bf, w_in, b_gate, conv_a_w, w_out_a, ssd_conv_w, ssd_conv_b, dt_bias, a_log,
                 d_skip, ssd_norm_w, w_out_b, w_o, ln_g, ln_b, side=None):
    dt0 = G0
    w_main = jnp.concatenate([w_in[:, :dt0], w_in[:, dt0 + HEADS:]], axis=1).astype(BF16)
    w_dt = _pad_lanes(w_in[:, dt0:dt0 + HEADS]).astype(BF16)
    proj, side_out = _in_proj(h_bf, w_main, side)
    dt, cum = _dt_proj(h_bf, w_dt, _pad_lanes(dt_bias[None, :]), _pad_lanes(a_log[None, :]))
    ga_ya = _mixer_a(proj, conv_a_w, w_out_a.astype(BF16), b_gate[None, :D])
    dskip = jnp.repeat(d_skip, HEADDIM)[None, :]
    yn = _ssd(proj, dt, cum, ssd_conv_w, ssd_conv_b[None, :], dskip, ssd_norm_w[None, :])
    m = _out_b(yn, w_out_b.astype(BF16), proj, b_gate[None, D:], ga_ya)
    h_f, h_bf = _wo_ln(m, w_o.astype(BF16), h_f, ln_g[None, :], ln_b[None, :])
    return h_f, h_bf, side_out


def _moe_layer(h_f, router_w, wg, wu, wd, ln_g, ln_b, final):
    meta, cnt = _router(h_f, _pad_lanes(router_w))
    i1 = meta[:, 0].astype(jnp.int32)
    i2 = meta[:, 1].astype(jnp.int32)
    counts = cnt[0, :N_EXPERTS].astype(jnp.int32)
    padded = ((counts + TM_MOE - 1) // TM_MOE) * TM_MOE
    ends = jnp.cumsum(padded)
    starts = ends - padded
    pos = jnp.stack([starts[i1] + meta[:, 4].astype(jnp.int32),
                     starts[i2] + meta[:, 5].astype(jnp.int32)], axis=1)
    pos = jnp.clip(pos, 0, P_MOE - 1)
    tok = jnp.arange(TP, dtype=jnp.int32)
    valid = (tok % LP) >= PADF
    dst = jnp.where(valid[:, None], pos, P_MOE).reshape(-1)
    src = jnp.zeros((P_MOE,), jnp.int32).at[dst].set(jnp.repeat(tok, 2), mode="drop")
    src = src.reshape(N_MOE_TILES, 1, TM_MOE)
    pos = pos.reshape(TP // COMBINE_TILE, 1, TOP_K * COMBINE_TILE)
    n_used = (ends[-1] // TM_MOE).reshape(1).astype(jnp.int32)
    tile_start = jnp.arange(N_MOE_TILES, dtype=jnp.int32) * TM_MOE
    tile_e = jnp.sum((tile_start[:, None] >= ends[None, :]).astype(jnp.int32), axis=1)
    last_e = jnp.minimum(tile_e[jnp.maximum(n_used[0] - 1, 0)], N_EXPERTS - 1)
    tile_e = jnp.where(tile_start < ends[-1], jnp.minimum(tile_e, N_EXPERTS - 1), last_e)
    xs = _gather_rows(n_used, src, h_f)
    y = _ffn_moe(tile_e, n_used, xs, wg, wu, wd)
    return _combine(pos, y, meta, h_f, ln_g[None, :], ln_b[None, :], final)


def kernel(x, meta_tokens, ln_in_g, ln_in_b, w_in, b_gate, conv_a_w, w_out_a, ssd_conv_w, ssd_conv_b,
           dt_bias, a_log, d_skip, ssd_norm_w, w_out_b, w_o, ln1_g, ln1_b, ffn_w_gate, ffn_w_up,
           ffn_w_down, router, moe_w_gate, moe_w_up, moe_w_down, ln2_g, ln2_b):
    h_f, h_bf = _ln_in(x, meta_tokens.astype(x.dtype), ln_in_g[None, :], ln_in_b[None, :])

    pending = [((i, k), w[i // 2]) for i in range(DEPTH) if i % 2 == 1
               for k, w in enumerate((moe_w_gate, moe_w_up, moe_w_down))]
    cast = {}

    def next_side(layer, steps):
        while pending and pending[0][0][0] < layer:
            pending.pop(0)
        if not pending:
            return None, None
        key, w = pending.pop(0)
        w2d = w.reshape(-1, w.shape[-1])
        return key, (w2d, _side_block_rows(w2d.shape[0], steps))

    for i in range(DEPTH):
        key, side = next_side(i, IN_PROJ_STEPS)
        h_f, h_bf, side_out = _token_mixer(h_f, h_bf, w_in[i], b_gate[i], conv_a_w[i], w_out_a[i],
                                           ssd_conv_w[i], ssd_conv_b[i], dt_bias[i], a_log[i], d_skip[i],
                                           ssd_norm_w[i], w_out_b[i], w_o[i], ln1_g[i], ln1_b[i], side)
        if key is not None:
            cast[key] = side_out
        j = i // 2
        if i % 2 == 0:
            key, side = next_side(i + 1, FFN_DENSE_STEPS)
            h_f, h_bf, side_out = _ffn_dense(h_bf, h_f, ffn_w_gate[j].astype(BF16), ffn_w_up[j].astype(BF16),
                                             ffn_w_down[j].astype(BF16), ln2_g[i][None, :], ln2_b[i][None, :],
                                             side)
            if key is not None:
                cast[key] = side_out
        else:
            final = i == DEPTH - 1
            wts = [cast[(i, k)].reshape(w[j].shape) if (i, k) in cast else w[j].astype(BF16)
                   for k, w in enumerate((moe_w_gate, moe_w_up, moe_w_down))]
            res = _moe_layer(h_f, router[j], wts[0], wts[1], wts[2], ln2_g[i], ln2_b[i], final)
            if final:
                return res
            h_f, h_bf = res
    return h_f.reshape(BATCH, LP, D)[:, PADF + N_META:, :]
```

```python
import functools

import jax
import jax.numpy as jnp
from jax import lax
from jax.experimental import pallas as pl
from jax.experimental.pallas import tpu as pltpu

F32 = jnp.float32
BF16 = jnp.bfloat16

D = 2048
BATCH = 4
SEQ = 4096
N_META = 16
DEPTH = 2
D_INNER = 2 * D
HEADDIM = 64
HEADS = D_INNER // HEADDIM
GROUPS = 8
HPG = HEADS // GROUPS
STATE = 128
GW = D_INNER // GROUPS
XBC = D_INNER + 2 * GROUPS * STATE
FF_DENSE = 5632
N_EXPERTS = 8
TOP_K = 2
FF_EXPERT = 7168
ALPHA = (2.0 * DEPTH) ** 0.25
LN_EPS = 1e-5
RMS_EPS = 1e-5
SSD_CONV_K = 4

CH = 256
PADF = CH - N_META
LP = PADF + N_META + SEQ
NCH = LP // CH
TP = BATCH * LP

C0, B0, V0 = 0, D, 2 * D
Z0 = 3 * D
X0 = Z0 + D_INNER
G0 = X0 + XBC
NP = G0

LANES = 128
SUBLANES = 8
BF16_SUBLANES = 16

GATE_OFF = HEADS
GATE_W = D + LANES

TM_MOE = 512
N_MOE_TILES = (2 * BATCH * (SEQ + N_META)) // TM_MOE + N_EXPERTS
P_MOE = N_MOE_TILES * TM_MOE
TF_MOE = 1024
TM_FFN = 256
TN_FFN = 2816
FFN_DENSE_STEPS = (TP // TM_FFN) * (FF_DENSE // TN_FFN)
TM_FFN_DOWN = 256
ROW_CHUNK = 128
TM_INPROJ = 1024
TN_INPROJ = 2048
IN_PROJ_STEPS = (NP // TN_INPROJ) * (TP // TM_INPROJ)
TOK_TILE = 256
COMBINE_TILE = CH
DT_ROWS = 256

SUB = 128
LOG2E = 1.4426950408889634
EXP2_CLAMP = 100.0

MIB = 1024 * 1024


def _cp(sem, vmem_mib):
    return pltpu.CompilerParams(dimension_semantics=sem, vmem_limit_bytes=int(vmem_mib * MIB))


def _sigmoid(x):
    return 0.5 * jnp.tanh(0.5 * x) + 0.5


def _silu(x):
    h = 0.5 * x
    return h + h * jnp.tanh(h)


def _softplus(x):
    return jnp.maximum(x, 0.0) + jnp.log1p(jnp.exp(-jnp.abs(x)))


def _layer_norm(x, g, b):
    mu = jnp.mean(x, axis=-1, keepdims=True)
    xc = x - mu
    var = jnp.mean(xc * xc, axis=-1, keepdims=True)
    return xc * lax.rsqrt(var + LN_EPS) * g + b


def _row_chunks(rows, size=ROW_CHUNK):
    return [slice(r, r + size) for r in range(0, rows, size)]


def _valid_rows(tile_idx, tm):
    row = lax.broadcasted_iota(jnp.int32, (tm, 1), 0)
    valid = None
    for s in range(tm // CH):
        chunk = tile_idx * (tm // CH) + s
        hi = jnp.where(chunk % NCH == 0, s * CH + PADF, s * CH)
        ok = jnp.logical_or(row < s * CH, row >= hi)
        valid = ok if valid is None else jnp.logical_and(valid, ok)
    return valid


def _ln_in_kernel(x_ref, meta_ref, g_ref, b_ref, hf_ref, hb_ref):
    j = pl.program_id(1)

    @pl.when(j == 0)
    def _():
        hf_ref[...] = jnp.zeros_like(hf_ref)
        hb_ref[...] = jnp.zeros_like(hb_ref)
        y = _layer_norm(meta_ref[...], g_ref[...], b_ref[...])
        hf_ref[PADF:, :] = y
        hb_ref[PADF:, :] = y.astype(BF16)

    @pl.when(j > 0)
    def _():
        y = _layer_norm(x_ref[0], g_ref[...], b_ref[...])
        hf_ref[...] = y
        hb_ref[...] = y.astype(BF16)


def _ln_in(x, meta, g, b):
    return pl.pallas_call(
        _ln_in_kernel,
        grid=(BATCH, NCH),
        in_specs=[
            pl.BlockSpec((1, CH, D), lambda bi, j: (bi, jnp.maximum(j - 1, 0), 0)),
            pl.BlockSpec((N_META, D), lambda bi, j: (0, 0)),
            pl.BlockSpec((1, D), lambda bi, j: (0, 0)),
            pl.BlockSpec((1, D), lambda bi, j: (0, 0)),
        ],
        out_specs=[
            pl.BlockSpec((CH, D), lambda bi, j: (bi * NCH + j, 0)),
            pl.BlockSpec((CH, D), lambda bi, j: (bi * NCH + j, 0)),
        ],
        out_shape=[jax.ShapeDtypeStruct((TP, D), F32), jax.ShapeDtypeStruct((TP, D), BF16)],
        compiler_params=_cp(("arbitrary", "arbitrary"), 32),
        name="ln_in",
    )(x, meta, g, b)


def _side_cast_specs(side, linear_step):
    src, rb = side
    nblk = src.shape[0] // rb
    imap = lambda *ids: (jnp.minimum(linear_step(*ids), nblk - 1), 0)
    spec = pl.BlockSpec((rb, src.shape[1]), imap)
    return nblk, spec, jax.ShapeDtypeStruct(src.shape, BF16)


def _side_block_rows(rows, steps):
    for nblk in range(min(steps, rows // BF16_SUBLANES), 0, -1):
        if rows % nblk == 0 and (rows // nblk) % BF16_SUBLANES == 0:
            return rows // nblk
    return rows


def _side_cast(nblk, step, src_ref, dst_ref):
    @pl.when(step < nblk)
    def _():
        dst_ref[...] = src_ref[...].astype(BF16)


def _mm_kernel(nblk, a_ref, w_ref, *rest):
    if nblk is None:
        (o_ref,) = rest
    else:
        c_ref, o_ref, co_ref = rest
        _side_cast(nblk, pl.program_id(0) * pl.num_programs(1) + pl.program_id(1), c_ref, co_ref)
    o_ref[...] = jnp.dot(a_ref[...], w_ref[...], preferred_element_type=F32).astype(o_ref.dtype)


def _in_proj(h_bf, w_main, side=None):
    tm, tn = TM_INPROJ, TN_INPROJ
    grid = (NP // tn, TP // tm)
    in_specs = [
        pl.BlockSpec((tm, D), lambda j, i: (i, 0)),
        pl.BlockSpec((D, tn), lambda j, i: (0, j)),
    ]
    out_specs = [pl.BlockSpec((tm, tn), lambda j, i: (i, j))]
    out_shape = [jax.ShapeDtypeStruct((TP, NP), BF16)]
    args = [h_bf, w_main]
    nblk = None
    if side is not None:
        nblk, spec, shape = _side_cast_specs(side, lambda j, i: j * grid[1] + i)
        in_specs.append(spec)
        out_specs.append(spec)
        out_shape.append(shape)
        args.append(side[0])
    res = pl.pallas_call(
        functools.partial(_mm_kernel, nblk),
        grid=grid,
        in_specs=in_specs,
        out_specs=out_specs,
        out_shape=out_shape,
        compiler_params=_cp(("arbitrary", "arbitrary"), 56),
        name="in_proj",
    )(*args)
    return res if side is not None else (res[0], None)


def _gate_proj(h_bf, w_gates):
    tm = TM_INPROJ
    return pl.pallas_call(
        functools.partial(_mm_kernel, None),
        grid=(2, TP // tm),
        in_specs=[
            pl.BlockSpec((tm, D), lambda j, i: (i, 0)),
            pl.BlockSpec((None, D, GATE_W), lambda j, i: (j, 0, 0)),
        ],
        out_specs=[pl.BlockSpec((None, tm, GATE_W), lambda j, i: (j, i, 0))],
        out_shape=[jax.ShapeDtypeStruct((2, TP, GATE_W), BF16)],
        compiler_params=_cp(("arbitrary", "arbitrary"), 56),
        name="gate_proj",
    )(h_bf, w_gates)[0]


def _dt_kernel(a_ref, w_ref, bias_ref, alog_ref, dt_ref, cum_ref):
    i = pl.program_id(0)
    raw = jnp.dot(a_ref[...], w_ref[...], preferred_element_type=F32) + bias_ref[...]
    dt = _softplus(raw)
    a_neg = -jnp.exp(alog_ref[...])
    da = jnp.where(_valid_rows(i, DT_ROWS), dt * a_neg, 0.0)
    r = lax.broadcasted_iota(jnp.int32, (DT_ROWS, DT_ROWS), 0)
    c = lax.broadcasted_iota(jnp.int32, (DT_ROWS, DT_ROWS), 1)
    same_chunk = None
    for s in range(DT_ROWS // CH):
        blk = jnp.logical_and(jnp.logical_and(r >= s * CH, r < (s + 1) * CH), c >= s * CH)
        same_chunk = blk if same_chunk is None else jnp.logical_or(same_chunk, blk)
    tril = jnp.logical_and(r >= c, same_chunk).astype(F32)
    cum = jnp.dot(tril, da, precision=lax.Precision.HIGHEST, preferred_element_type=F32)
    dt_ref[...] = dt
    cum_ref[...] = cum


def _dt_proj(h_bf, w_dt, dt_bias, a_log):
    return pl.pallas_call(
        _dt_kernel,
        grid=(TP // DT_ROWS,),
        in_specs=[
            pl.BlockSpec((DT_ROWS, D), lambda i: (i, 0)),
            pl.BlockSpec((D, LANES), lambda i: (0, 0)),
            pl.BlockSpec((1, LANES), lambda i: (0, 0)),
            pl.BlockSpec((1, LANES), lambda i: (0, 0)),
        ],
        out_specs=[
            pl.BlockSpec((DT_ROWS, LANES), lambda i: (i, 0)),
            pl.BlockSpec((DT_ROWS, LANES), lambda i: (i, 0)),
        ],
        out_shape=[jax.ShapeDtypeStruct((TP, LANES), F32), jax.ShapeDtypeStruct((TP, LANES), F32)],
        compiler_params=_cp(("arbitrary",), 16),
        name="dt_proj",
    )(h_bf, w_dt, dt_bias, a_log)


def _causal_conv(x, halo, w, k):
    acc = w[k - 1:k, :] * x
    for s in range(1, k):
        acc = acc + w[k - 1 - s:k - s, :] * pltpu.roll(x, s, 0)
    x8 = x[0:SUBLANES, :]
    h8 = halo[BF16_SUBLANES - SUBLANES:, :]
    row = lax.broadcasted_iota(jnp.int32, (SUBLANES, 1), 0)
    top = w[k - 1:k, :] * x8
    for s in range(1, k):
        shifted = jnp.where(row < s, pltpu.roll(h8, s, 0), pltpu.roll(x8, s, 0))
        top = top + w[k - 1 - s:k - s, :] * shifted
    return jnp.concatenate([top, acc[SUBLANES:, :]], axis=0)


def _mixa_kernel(c_ref, b_ref, v_ref, ch_ref, vh_ref, g_ref, cw_ref, wo_ref, bg_ref, o_ref):
    u = c_ref[...].astype(F32) * v_ref[...].astype(F32)
    hu = ch_ref[...].astype(F32) * vh_ref[...].astype(F32)
    conv = _causal_conv(u, hu, cw_ref[...], 3)
    lhs = (b_ref[...].astype(F32) * conv).astype(BF16)
    ya = jnp.dot(lhs, wo_ref[...], preferred_element_type=F32)
    ga = _sigmoid(g_ref[:, GATE_OFF:GATE_OFF + D].astype(F32) + bg_ref[...])
    o_ref[...] = (ga * ya).astype(BF16)


def _mixer_a(proj, gates, conv_w, w_out_a, bgate_a):
    tm = 512
    hb = tm // BF16_SUBLANES
    halo = lambda col: (lambda i: (jnp.maximum(i * hb - 1, 0), col))
    return pl.pallas_call(
        _mixa_kernel,
        grid=(TP // tm,),
        in_specs=[
            pl.BlockSpec((tm, D), lambda i: (i, C0 // D)),
            pl.BlockSpec((tm, D), lambda i: (i, B0 // D)),
            pl.BlockSpec((tm, D), lambda i: (i, V0 // D)),
            pl.BlockSpec((BF16_SUBLANES, D), halo(C0 // D)),
            pl.BlockSpec((BF16_SUBLANES, D), halo(V0 // D)),
            pl.BlockSpec((None, tm, GATE_W), lambda i: (0, i, 0)),
            pl.BlockSpec((3, D), lambda i: (0, 0)),
            pl.BlockSpec((D, D), lambda i: (0, 0), pipeline_mode=pl.Buffered(1)),
            pl.BlockSpec((1, D), lambda i: (0, 0)),
        ],
        out_specs=pl.BlockSpec((tm, D), lambda i: (i, 0)),
        out_shape=jax.ShapeDtypeStruct((TP, D), BF16),
        compiler_params=_cp(("arbitrary",), 56),
        name="mixer_a",
    )(proj, proj, proj, proj, proj, gates, conv_w, w_out_a, bgate_a)


def _ssd_kernel(xs_ref, bm_ref, cm_ref, z_ref, dt_ref, cum_ref,
                wx_ref, wb_ref, wc_ref, bx_ref, bb_ref, bc_ref, dsk_ref, nw_ref,
                o_ref, st_ref):
    g = pl.program_id(1)
    st_ref[...] = jnp.zeros_like(st_ref)

    rr = lax.broadcasted_iota(jnp.int32, (CH, CH), 0)
    cc = lax.broadcasted_iota(jnp.int32, (CH, CH), 1)
    shift_mats = [(rr - cc == s).astype(BF16) for s in range(1, SSD_CONV_K)]
    w_all = jnp.concatenate([wx_ref[...], wb_ref[...], wc_ref[...]], axis=1)
    b_all = jnp.concatenate([bx_ref[...], bb_ref[...], bc_ref[...]], axis=1)
    taps = [w_all[k:k + 1, :] for k in range(SSD_CONV_K)]
    lane_shift = (LANES - HPG * g) % LANES
    row = lax.broadcasted_iota(jnp.int32, (CH, 1), 0)
    row8 = lax.broadcasted_iota(jnp.int32, (SUBLANES, 1), 0)
    r2 = lax.broadcasted_iota(jnp.int32, (SUB, SUB), 0)
    c2 = lax.broadcasted_iota(jnp.int32, (SUB, SUB), 1)
    causal = r2 >= c2
    lane = lax.broadcasted_iota(jnp.int32, (1, LANES), 1)
    lo = lane < HEADDIM
    dskip = dsk_ref[...]
    norm_w = nw_ref[...]

    def chunk(c, carry):
        r0 = pl.multiple_of(c * CH, CH)
        h0 = pl.multiple_of(jnp.maximum(r0 - BF16_SUBLANES, 0), BF16_SUBLANES)
        rows = pl.ds(r0, CH)
        above = pl.ds(h0, BF16_SUBLANES)
        raw = jnp.concatenate([xs_ref[rows, :], bm_ref[rows, :], cm_ref[rows, :]], axis=1)
        halo = jnp.concatenate([xs_ref[above, :], bm_ref[above, :], cm_ref[above, :]], axis=1)
        h8 = halo[BF16_SUBLANES - SUBLANES:, :].astype(F32)

        acc = taps[SSD_CONV_K - 1] * raw.astype(F32) + b_all
        fix = jnp.zeros((SUBLANES, raw.shape[1]), F32)
        for s in range(1, SSD_CONV_K):
            tap = taps[SSD_CONV_K - 1 - s]
            acc = acc + tap * jnp.dot(shift_mats[s - 1], raw, preferred_element_type=F32)
            fix = fix + jnp.where(row8 < s, tap * pltpu.roll(h8, s, 0), 0.0)
        act = _silu(jnp.concatenate([acc[0:SUBLANES, :] + fix, acc[SUBLANES:, :]], axis=0))
        first = jnp.where(c == 0, PADF, 0)
        xs = jnp.where(row >= first, act[:, :GW], 0.0)
        bm = act[:, GW:GW + STATE]
        cm = act[:, GW + STATE:]

        dtg = pltpu.roll(dt_ref[rows, :], lane_shift, 1)
        cumg = pltpu.roll(cum_ref[rows, :], lane_shift, 1) * LOG2E
        bm_bf = bm.astype(BF16)
        cm_bf = cm.astype(BF16)
        xs_bf = xs.astype(BF16)

        st = st_ref[...]
        y_rows = []
        for q in range(CH // SUB):
            rs = slice(q * SUB, (q + 1) * SUB)
            cq = cumg[rs, :]
            if q > 0:
                cq = cq - cumg[q * SUB - 1:q * SUB, :]
            dq = dtg[rs, :]
            last = cq[SUB - 1:SUB, :]
            ecum = jnp.exp2(cq)
            wts = jnp.exp2(last - cq) * dq
            elast = jnp.exp2(last)
            src_t = (cq - jnp.log2(dq)).T

            cb = lax.dot_general(cm_bf[rs, :], bm_bf[rs, :], (((1,), (1,)), ((), ())),
                                 preferred_element_type=F32)
            cbm = jnp.where(causal, cb, 0.0)
            y_off = jnp.dot(cm_bf[rs, :], st.astype(BF16), preferred_element_type=F32)

            def head_mat(j):
                seg = cq[:, j:j + 1] - src_t[j:j + 1, :]
                return (cbm * jnp.exp2(jnp.minimum(seg, EXP2_CLAMP))).astype(BF16)

            ys, xws, els = [], [], []
            for p in range(GW // LANES):
                ja, jb = 2 * p, 2 * p + 1
                sl = slice(p * LANES, (p + 1) * LANES)
                xp = xs_bf[rs, sl]
                ya = jnp.dot(head_mat(ja), xp, preferred_element_type=F32)
                yb = jnp.dot(head_mat(jb), xp, preferred_element_type=F32)
                e_p = jnp.where(lo, ecum[:, ja:ja + 1], ecum[:, jb:jb + 1])
                ys.append(jnp.where(lo, ya, yb) + e_p * y_off[:, sl])
                w_p = jnp.where(lo, wts[:, ja:ja + 1], wts[:, jb:jb + 1])
                xws.append((xs[rs, sl] * w_p).astype(BF16))
                els.append(jnp.where(lo, elast[:, ja:ja + 1], elast[:, jb:jb + 1]))
            y_rows.append(jnp.concatenate(ys, axis=1))
            xw = jnp.concatenate(xws, axis=1)
            el = jnp.concatenate(els, axis=1)
            bm_t = bm[rs, :].T.astype(BF16)
            st = st * el + jnp.dot(bm_t, xw, preferred_element_type=F32)
        st_ref[...] = st
        y = jnp.concatenate(y_rows, axis=0)

        y = y + xs * dskip
        yg = y * _silu(z_ref[rows, :].astype(F32))
        ms = jnp.mean(yg * yg, axis=-1, keepdims=True)
        o_ref[rows, :] = (yg * lax.rsqrt(ms + RMS_EPS) * norm_w).astype(BF16)
        return carry

    lax.fori_loop(0, NCH, chunk, 0)


def _ssd(proj, dt, cum, conv_w, conv_b, dskip, norm_w):
    xcol = lambda g: X0 // GW + g
    bcol = lambda g: (X0 + D_INNER) // STATE + g
    ccol = lambda g: (X0 + D_INNER + GROUPS * STATE) // STATE + g
    zcol = lambda g: Z0 // GW + g
    wbcol = lambda g: D_INNER // STATE + g
    wccol = lambda g: (D_INNER + GROUPS * STATE) // STATE + g
    rows = lambda col_fn: (lambda b, g: (b, col_fn(g)))
    par = lambda col_fn: (lambda b, g: (0, col_fn(g)))
    return pl.pallas_call(
        _ssd_kernel,
        grid=(BATCH, GROUPS),
        in_specs=[
            pl.BlockSpec((LP, GW), rows(xcol)),
            pl.BlockSpec((LP, STATE), rows(bcol)),
            pl.BlockSpec((LP, STATE), rows(ccol)),
            pl.BlockSpec((LP, GW), rows(zcol)),
            pl.BlockSpec((LP, LANES), lambda b, g: (b, 0)),
            pl.BlockSpec((LP, LANES), lambda b, g: (b, 0)),
            pl.BlockSpec((SSD_CONV_K, GW), par(lambda g: g)),
            pl.BlockSpec((SSD_CONV_K, STATE), par(wbcol)),
            pl.BlockSpec((SSD_CONV_K, STATE), par(wccol)),
            pl.BlockSpec((1, GW), par(lambda g: g)),
            pl.BlockSpec((1, STATE), par(wbcol)),
            pl.BlockSpec((1, STATE), par(wccol)),
            pl.BlockSpec((1, GW), par(lambda g: g)),
            pl.BlockSpec((1, GW), par(lambda g: g)),
        ],
        out_specs=pl.BlockSpec((LP, GW), lambda b, g: (b, g)),
        out_shape=jax.ShapeDtypeStruct((TP, D_INNER), BF16),
        scratch_shapes=[pltpu.VMEM((STATE, GW), F32)],
        compiler_params=_cp(("arbitrary", "arbitrary"), 56),
        name="ssd",
    )(proj, proj, proj, proj, dt, cum,
      conv_w, conv_w, conv_w, conv_b, conv_b, conv_b, dskip, norm_w)


def _outb_kernel(y_ref, w_ref, g_ref, bg_ref, ya_ref, o_ref):
    for rs in _row_chunks(y_ref.shape[0]):
        yb = jnp.dot(y_ref[rs, :], w_ref[...], preferred_element_type=F32)
        gb = _sigmoid(g_ref[rs, GATE_OFF:GATE_OFF + D].astype(F32) + bg_ref[...])
        o_ref[rs, :] = (gb * yb + ya_ref[rs, :].astype(F32)).astype(BF16)


def _out_b(yn, w_out_b, gates, bgate_b, ga_ya):
    tm = 512
    return pl.pallas_call(
        _outb_kernel,
        grid=(TP // tm,),
        in_specs=[
            pl.BlockSpec((tm, D_INNER), lambda i: (i, 0)),
            pl.BlockSpec((D_INNER, D), lambda i: (0, 0), pipeline_mode=pl.Buffered(1)),
            pl.BlockSpec((None, tm, GATE_W), lambda i: (1, i, 0)),
            pl.BlockSpec((1, D), lambda i: (0, 0)),
            pl.BlockSpec((tm, D), lambda i: (i, 0)),
        ],
        out_specs=pl.BlockSpec((tm, D), lambda i: (i, 0)),
        out_shape=jax.ShapeDtypeStruct((TP, D), BF16),
        compiler_params=_cp(("arbitrary",), 56),
        name="out_b",
    )(yn, w_out_b, gates, bgate_b, ga_ya)


def _wo_kernel(m_ref, w_ref, h_ref, g_ref, b_ref, hf_ref, hb_ref):
    for rs in _row_chunks(m_ref.shape[0]):
        mix = jnp.dot(m_ref[rs, :], w_ref[...], preferred_element_type=F32)
        y = _layer_norm(ALPHA * h_ref[rs, :] + mix, g_ref[...], b_ref[...])
        hf_ref[rs, :] = y
        hb_ref[rs, :] = y.astype(BF16)


def _wo_ln(m, w_o, h, g, b):
    tm = 512
    return pl.pallas_call(
        _wo_kernel,
        grid=(TP // tm,),
        in_specs=[
            pl.BlockSpec((tm, D), lambda i: (i, 0)),
            pl.BlockSpec((D, D), lambda i: (0, 0), pipeline_mode=pl.Buffered(1)),
            pl.BlockSpec((tm, D), lambda i: (i, 0)),
            pl.BlockSpec((1, D), lambda i: (0, 0)),
            pl.BlockSpec((1, D), lambda i: (0, 0)),
        ],
        out_specs=[
            pl.BlockSpec((tm, D), lambda i: (i, 0)),
            pl.BlockSpec((tm, D), lambda i: (i, 0)),
        ],
        out_shape=[jax.ShapeDtypeStruct((TP, D), F32), jax.ShapeDtypeStruct((TP, D), BF16)],
        compiler_params=_cp(("arbitrary",), 56),
        name="wo_ln",
    )(m, w_o, h, g, b)


def _swiglu_step(x_bf, wg_ref, wu_ref, wd_ref):
    gate = jnp.dot(x_bf, wg_ref[...], preferred_element_type=F32)
    up = jnp.dot(x_bf, wu_ref[...], preferred_element_type=F32)
    mid = (_silu(gate) * up).astype(BF16)
    return jnp.dot(mid, wd_ref[...], preferred_element_type=F32)


def _ffn_gu_kernel(nblk, x_ref, wg_ref, wu_ref, *rest):
    if nblk is None:
        (o_ref,) = rest
    else:
        c_ref, o_ref, co_ref = rest
        _side_cast(nblk, pl.program_id(0) * pl.num_programs(1) + pl.program_id(1), c_ref, co_ref)
    for rs in _row_chunks(x_ref.shape[0]):
        x = x_ref[rs, :]
        gate = jnp.dot(x, wg_ref[...], preferred_element_type=F32)
        up = jnp.dot(x, wu_ref[...], preferred_element_type=F32)
        o_ref[rs, :] = (_silu(gate) * up).astype(BF16)


def _ffn_down_kernel(a_ref, w_ref, h_ref, g_ref, b_ref, hf_ref, hb_ref):
    valid = _valid_rows(pl.program_id(0), TM_FFN_DOWN)
    for rs in _row_chunks(TM_FFN_DOWN):
        f = jnp.dot(a_ref[rs, :], w_ref[...], preferred_element_type=F32)
        y = _layer_norm(ALPHA * h_ref[rs, :] + f, g_ref[...], b_ref[...])
        y = jnp.where(valid[rs, :], y, 0.0)
        hf_ref[rs, :] = y
        hb_ref[rs, :] = y.astype(BF16)


def _ffn_dense(h_bf, h_f, wg, wu, wd, g, b, side=None):
    tm, tn = TM_FFN, TN_FFN
    grid = (FF_DENSE // tn, TP // tm)
    in_specs = [
        pl.BlockSpec((tm, D), lambda j, i: (i, 0)),
        pl.BlockSpec((D, tn), lambda j, i: (0, j), pipeline_mode=pl.Buffered(1)),
        pl.BlockSpec((D, tn), lambda j, i: (0, j), pipeline_mode=pl.Buffered(1)),
    ]
    out_specs = [pl.BlockSpec((tm, tn), lambda j, i: (i, j))]
    out_shape = [jax.ShapeDtypeStruct((TP, FF_DENSE), BF16)]
    args = [h_bf, wg, wu]
    nblk = None
    if side is not None:
        nblk, spec, shape = _side_cast_specs(side, lambda j, i: j * grid[1] + i)
        in_specs.append(spec)
        out_specs.append(spec)
        out_shape.append(shape)
        args.append(side[0])
    res = pl.pallas_call(
        functools.partial(_ffn_gu_kernel, nblk),
        grid=grid,
        in_specs=in_specs,
        out_specs=out_specs,
        out_shape=out_shape,
        compiler_params=_cp(("arbitrary", "arbitrary"), 56),
        name="ffn_gate_up",
    )(*args)
    mid, side_out = (res[0], res[1]) if side is not None else (res[0], None)

    tmd = TM_FFN_DOWN
    h_f, h_bf = pl.pallas_call(
        _ffn_down_kernel,
        grid=(TP // tmd,),
        in_specs=[
            pl.BlockSpec((tmd, FF_DENSE), lambda i: (i, 0)),
            pl.BlockSpec((FF_DENSE, D), lambda i: (0, 0), pipeline_mode=pl.Buffered(1)),
            pl.BlockSpec((tmd, D), lambda i: (i, 0)),
            pl.BlockSpec((1, D), lambda i: (0, 0)),
            pl.BlockSpec((1, D), lambda i: (0, 0)),
        ],
        out_specs=[pl.BlockSpec((tmd, D), lambda i: (i, 0)), pl.BlockSpec((tmd, D), lambda i: (i, 0))],
        out_shape=[jax.ShapeDtypeStruct((TP, D), F32), jax.ShapeDtypeStruct((TP, D), BF16)],
        compiler_params=_cp(("arbitrary",), 56),
        name="ffn_down",
    )(mid, wd, h_f, g, b)
    return h_f, h_bf, side_out


def _router_kernel(h_ref, r_ref, meta_ref, cnt_ref, carry_ref):
    i = pl.program_id(0)
    tm = TOK_TILE

    @pl.when(i == 0)
    def _():
        carry_ref[...] = jnp.zeros_like(carry_ref)

    logits = jnp.dot(h_ref[...], r_ref[...], precision=lax.Precision.HIGHEST,
                     preferred_element_type=F32)
    lane = lax.broadcasted_iota(jnp.int32, (tm, LANES), 1).astype(F32)
    neg = jnp.float32(-jnp.inf)
    logits = jnp.where(lane < N_EXPERTS, logits, neg)
    m1 = jnp.max(logits, axis=-1, keepdims=True)
    i1 = jnp.min(jnp.where(logits == m1, lane, float(LANES)), axis=-1, keepdims=True)
    rest = jnp.where(lane == i1, neg, logits)
    m2 = jnp.max(rest, axis=-1, keepdims=True)
    i2 = jnp.min(jnp.where(rest == m2, lane, float(LANES)), axis=-1, keepdims=True)
    e = jnp.exp(m2 - m1)
    w1 = 1.0 / (1.0 + e)
    w2 = e / (1.0 + e)

    valid = _valid_rows(i, tm)
    sel = jnp.logical_and(jnp.logical_or(lane == i1, lane == i2), valid)
    onehot = sel.astype(F32)
    r = lax.broadcasted_iota(jnp.int32, (tm, tm), 0)
    c = lax.broadcasted_iota(jnp.int32, (tm, tm), 1)
    strict = (r > c).astype(BF16)
    before = jnp.dot(strict, onehot.astype(BF16), preferred_element_type=F32) + carry_ref[0:1, :]
    rank1 = jnp.sum(jnp.where(lane == i1, before, 0.0), axis=-1, keepdims=True)
    rank2 = jnp.sum(jnp.where(lane == i2, before, 0.0), axis=-1, keepdims=True)
    total = carry_ref[0:1, :] + jnp.sum(onehot, axis=0, keepdims=True)
    carry_ref[...] = jnp.broadcast_to(total, carry_ref.shape)
    cnt_ref[...] = jnp.broadcast_to(total, cnt_ref.shape)

    meta = jnp.where(lane == 0, i1, 0.0)
    meta = jnp.where(lane == 1, i2, meta)
    meta = jnp.where(lane == 2, w1, meta)
    meta = jnp.where(lane == 3, w2, meta)
    meta = jnp.where(lane == 4, rank1, meta)
    meta = jnp.where(lane == 5, rank2, meta)
    meta_ref[...] = meta


def _router(h_f, router_w):
    tm = TOK_TILE
    return pl.pallas_call(
        _router_kernel,
        grid=(TP // tm,),
        in_specs=[
            pl.BlockSpec((tm, D), lambda i: (i, 0)),
            pl.BlockSpec((D, LANES), lambda i: (0, 0)),
        ],
        out_specs=[
            pl.BlockSpec((tm, LANES), lambda i: (i, 0)),
            pl.BlockSpec((8, LANES), lambda i: (0, 0)),
        ],
        out_shape=[jax.ShapeDtypeStruct((TP, LANES), F32), jax.ShapeDtypeStruct((8, LANES), F32)],
        scratch_shapes=[pltpu.VMEM((8, LANES), F32)],
        compiler_params=_cp(("arbitrary",), 32),
        name="router",
    )(h_f, router_w)


def _row_dma_ops(copy, count):
    def start(t, carry):
        copy(t).start()
        return carry

    def wait(t, carry):
        copy(t).wait()
        return carry

    return (lambda: lax.fori_loop(0, count, start, 0, unroll=8),
            lambda: lax.fori_loop(0, count, wait, 0, unroll=8))


def _gather_kernel(nu_ref, cur_ref, nxt_ref, h_ref, o_ref, buf, sem):
    i = pl.program_id(0)
    nu = nu_ref[0]
    slot = i % 2

    def copies(idx_ref, s):
        return _row_dma_ops(
            lambda t: pltpu.make_async_copy(h_ref.at[pl.ds(idx_ref[0, 0, t], 1)],
                                            buf.at[s, pl.ds(t, 1)], sem.at[s]), TM_MOE)

    @pl.when(jnp.logical_and(i == 0, nu > 0))
    def _():
        copies(cur_ref, 0)[0]()

    @pl.when(i + 1 < nu)
    def _():
        copies(nxt_ref, 1 - slot)[0]()

    @pl.when(i < nu)
    def _():
        copies(cur_ref, slot)[1]()
        o_ref[...] = buf[slot].astype(BF16)

    @pl.when(i >= nu)
    def _():
        o_ref[...] = jnp.zeros_like(o_ref)


def _gather_rows(n_used, src, h_f):
    last = N_MOE_TILES - 1
    grid_spec = pltpu.PrefetchScalarGridSpec(
        num_scalar_prefetch=1,
        grid=(N_MOE_TILES,),
        in_specs=[
            pl.BlockSpec((1, 1, TM_MOE), lambda i, nu: (i, 0, 0), memory_space=pltpu.SMEM),
            pl.BlockSpec((1, 1, TM_MOE), lambda i, nu: (jnp.minimum(i + 1, last), 0, 0),
                         memory_space=pltpu.SMEM),
            pl.BlockSpec(memory_space=pl.ANY),
        ],
        out_specs=pl.BlockSpec((TM_MOE, D), lambda i, nu: (i, 0)),
        scratch_shapes=[pltpu.VMEM((2, TM_MOE, D), F32), pltpu.SemaphoreType.DMA((2,))],
    )
    return pl.pallas_call(
        _gather_kernel,
        grid_spec=grid_spec,
        out_shape=jax.ShapeDtypeStruct((P_MOE, D), BF16),
        compiler_params=_cp(("arbitrary",), 32),
        name="moe_gather",
    )(n_used, src, src, h_f)


def _ffn_moe_kernel(te_ref, nu_ref, x_ref, wg_ref, wu_ref, wd_ref, o_ref, acc_ref):
    del te_ref
    i = pl.program_id(0)
    f = pl.program_id(1)
    used = i < nu_ref[0]

    @pl.when(jnp.logical_and(used, f == 0))
    def _():
        acc_ref[...] = jnp.zeros_like(acc_ref)

    @pl.when(used)
    def _():
        acc_ref[...] += _swiglu_step(x_ref[...], wg_ref, wu_ref, wd_ref)

    @pl.when(jnp.logical_and(used, f == pl.num_programs(1) - 1))
    def _():
        o_ref[...] = acc_ref[...]

    @pl.when(jnp.logical_and(jnp.logical_not(used), f == 0))
    def _():
        o_ref[...] = jnp.zeros_like(o_ref)


def _ffn_moe(tile_e, n_used, xs, wg, wu, wd):
    tm, tf = TM_MOE, TF_MOE
    nf = FF_EXPERT // tf

    def fblk(i, f, nu):
        return jnp.where(i < nu[0], f, nf - 1)

    grid_spec = pltpu.PrefetchScalarGridSpec(
        num_scalar_prefetch=2,
        grid=(N_MOE_TILES, nf),
        in_specs=[
            pl.BlockSpec((tm, D), lambda i, f, te, nu: (i, 0)),
            pl.BlockSpec((None, D, tf), lambda i, f, te, nu: (te[i], 0, fblk(i, f, nu))),
            pl.BlockSpec((None, D, tf), lambda i, f, te, nu: (te[i], 0, fblk(i, f, nu))),
            pl.BlockSpec((None, tf, D), lambda i, f, te, nu: (te[i], fblk(i, f, nu), 0)),
        ],
        out_specs=pl.BlockSpec((tm, D), lambda i, f, te, nu: (i, 0)),
        scratch_shapes=[pltpu.VMEM((tm, D), F32)],
    )
    return pl.pallas_call(
        _ffn_moe_kernel,
        grid_spec=grid_spec,
        out_shape=jax.ShapeDtypeStruct((P_MOE, D), F32),
        compiler_params=_cp(("arbitrary", "arbitrary"), 58),
        name="ffn_moe",
    )(tile_e, n_used, xs, wg, wu, wd)


def _combine_kernel(final, cur_ref, nxt_ref, y_ref, meta_ref, h_ref, g_ref, b_ref, *rest):
    if final:
        out_ref, buf, sem = rest
    else:
        hf_ref, hb_ref, buf, sem = rest
    i = pl.program_id(0)
    tm = COMBINE_TILE
    slot = i % 2

    def copies(idx_ref, s):
        def copy(t, k):
            return pltpu.make_async_copy(y_ref.at[pl.ds(idx_ref[0, 0, TOP_K * t + k], 1)],
                                         buf.at[s, k, pl.ds(t, 1)], sem.at[s])

        def start(t, carry):
            for k in range(TOP_K):
                copy(t, k).start()
            return carry

        def wait(t, carry):
            for k in range(TOP_K):
                copy(t, k).wait()
            return carry

        return (lambda: lax.fori_loop(0, tm, start, 0, unroll=8),
                lambda: lax.fori_loop(0, tm, wait, 0, unroll=8))

    @pl.when(i == 0)
    def _():
        copies(cur_ref, 0)[0]()

    @pl.when(i + 1 < pl.num_programs(0))
    def _():
        copies(nxt_ref, 1 - slot)[0]()

    copies(cur_ref, slot)[1]()

    meta = meta_ref[...]
    w1 = meta[:, 2:3]
    w2 = meta[:, 3:4]
    f = w1 * buf[slot, 0] + w2 * buf[slot, 1]
    y = _layer_norm(ALPHA * h_ref[...] + f, g_ref[...], b_ref[...])
    if final:
        out_ref[0] = y
    else:
        y = jnp.where(_valid_rows(i, tm), y, 0.0)
        hf_ref[...] = y
        hb_ref[...] = y.astype(BF16)


def _combine(pos, y, meta, h_f, g, b, final):
    tm = COMBINE_TILE
    if final:
        out_specs = pl.BlockSpec((1, tm, D), lambda i: (i // NCH, jnp.maximum(i % NCH - 1, 0), 0))
        out_shape = jax.ShapeDtypeStruct((BATCH, SEQ, D), F32)
    else:
        out_specs = [pl.BlockSpec((tm, D), lambda i: (i, 0)), pl.BlockSpec((tm, D), lambda i: (i, 0))]
        out_shape = [jax.ShapeDtypeStruct((TP, D), F32), jax.ShapeDtypeStruct((TP, D), BF16)]
    last = TP // tm - 1
    return pl.pallas_call(
        functools.partial(_combine_kernel, final),
        grid=(TP // tm,),
        in_specs=[
            pl.BlockSpec((1, 1, 2 * tm), lambda i: (i, 0, 0), memory_space=pltpu.SMEM),
            pl.BlockSpec((1, 1, 2 * tm), lambda i: (jnp.minimum(i + 1, last), 0, 0), memory_space=pltpu.SMEM),
            pl.BlockSpec(memory_space=pl.ANY),
            pl.BlockSpec((tm, LANES), lambda i: (i, 0)),
            pl.BlockSpec((tm, D), lambda i: (i, 0)),
            pl.BlockSpec((1, D), lambda i: (0, 0)),
            pl.BlockSpec((1, D), lambda i: (0, 0)),
        ],
        out_specs=out_specs,
        out_shape=out_shape,
        scratch_shapes=[pltpu.VMEM((2, TOP_K, tm, D), F32), pltpu.SemaphoreType.DMA((2,))],
        compiler_params=_cp(("arbitrary",), 32),
        name="moe_combine",
    )(pos, pos, y, meta, h_f, g, b)


def _pad_lanes(v, width=LANES):
    return jnp.pad(v, ((0, 0), (0, width - v.shape[1])))


def _token_mixer(h_f, h_bf, w_in, b_gate, conv_a_w, w_out_a, ssd_conv_w, ssd_conv_b, dt_bias, a_log,
                 d_skip, ssd_norm_w, w_out_b, w_o, ln_g, ln_b, side=None):
    w_main = w_in[:, :G0].astype(BF16)
    lane = jnp.arange(LANES)[None, :]
    w_dt = jnp.where(lane < HEADS, w_in[:, G0:G0 + LANES], 0.0).astype(BF16)
    win_a = w_in[:, G0:G0 + GATE_W]
    win_b = w_in[:, G0 + D:]
    win_b = jnp.pad(win_b, ((0, 0), (0, GATE_W - win_b.shape[1])))
    w_gates = jnp.stack([win_a, win_b]).astype(BF16)
    proj, side_out = _in_proj(h_bf, w_main, side)
    gates = _gate_proj(h_bf, w_gates)
    dt, cum = _dt_proj(h_bf, w_dt, _pad_lanes(dt_bias[None, :]), _pad_lanes(a_log[None, :]))
    ga_ya = _mixer_a(proj, gates, conv_a_w, w_out_a.astype(BF16), b_gate[None, :D])
    dskip = jnp.repeat(d_skip, HEADDIM)[None, :]
    yn = _ssd(proj, dt, cum, ssd_conv_w, ssd_conv_b[None, :], dskip, ssd_norm_w[None, :])
    m = _out_b(yn, w_out_b.astype(BF16), gates, b_gate[None, D:], ga_ya)
    h_f, h_bf = _wo_ln(m, w_o.astype(BF16), h_f, ln_g[None, :], ln_b[None, :])
    return h_f, h_bf, side_out


def _moe_layer(h_f, router_w, wg, wu, wd, ln_g, ln_b, final):
    meta, cnt = _router(h_f, _pad_lanes(router_w))
    i1 = meta[:, 0].astype(jnp.int32)
    i2 = meta[:, 1].astype(jnp.int32)
    counts = cnt[0, :N_EXPERTS].astype(jnp.int32)
    padded = ((counts + TM_MOE - 1) // TM_MOE) * TM_MOE
    ends = jnp.cumsum(padded)
    starts = ends - padded
    pos = jnp.stack([starts[i1] + meta[:, 4].astype(jnp.int32),
                     starts[i2] + meta[:, 5].astype(jnp.int32)], axis=1)
    pos = jnp.clip(pos, 0, P_MOE - 1)
    tok = jnp.arange(TP, dtype=jnp.int32)
    valid = (tok % LP) >= PADF
    dst = jnp.where(valid[:, None], pos, P_MOE).reshape(-1)
    src = jnp.zeros((P_MOE,), jnp.int32).at[dst].set(jnp.repeat(tok, 2), mode="drop")
    src = src.reshape(N_MOE_TILES, 1, TM_MOE)
    pos = pos.reshape(TP // COMBINE_TILE, 1, TOP_K * COMBINE_TILE)
    n_used = (ends[-1] // TM_MOE).reshape(1).astype(jnp.int32)
    tile_start = jnp.arange(N_MOE_TILES, dtype=jnp.int32) * TM_MOE
    tile_e = jnp.sum((tile_start[:, None] >= ends[None, :]).astype(jnp.int32), axis=1)
    last_e = jnp.minimum(tile_e[jnp.maximum(n_used[0] - 1, 0)], N_EXPERTS - 1)
    tile_e = jnp.where(tile_start < ends[-1], jnp.minimum(tile_e, N_EXPERTS - 1), last_e)
    xs = _gather_rows(n_used, src, h_f)
    y = _ffn_moe(tile_e, n_used, xs, wg, wu, wd)
    return _combine(pos, y, meta, h_f, ln_g[None, :], ln_b[None, :], final)


def kernel(x, meta_tokens, ln_in_g, ln_in_b, w_in, b_gate, conv_a_w, w_out_a, ssd_conv_w, ssd_conv_b,
           dt_bias, a_log, d_skip, ssd_norm_w, w_out_b, w_o, ln1_g, ln1_b, ffn_w_gate, ffn_w_up,
           ffn_w_down, router, moe_w_gate, moe_w_up, moe_w_down, ln2_g, ln2_b):
    h_f, h_bf = _ln_in(x, meta_tokens.astype(x.dtype), ln_in_g[None, :], ln_in_b[None, :])

    pending = [((i, k), w[i // 2]) for i in range(DEPTH) if i % 2 == 1
               for k, w in enumerate((moe_w_gate, moe_w_up, moe_w_down))]
    cast = {}

    def next_side(layer, steps):
        while pending and pending[0][0][0] < layer:
            pending.pop(0)
        if not pending:
            return None, None
        key, w = pending.pop(0)
        w2d = w.reshape(-1, w.shape[-1])
        return key, (w2d, _side_block_rows(w2d.shape[0], steps))

    for i in range(DEPTH):
        key, side = next_side(i, IN_PROJ_STEPS)
        h_f, h_bf, side_out = _token_mixer(h_f, h_bf, w_in[i], b_gate[i], conv_a_w[i], w_out_a[i],
                                           ssd_conv_w[i], ssd_conv_b[i], dt_bias[i], a_log[i], d_skip[i],
                                           ssd_norm_w[i], w_out_b[i], w_o[i], ln1_g[i], ln1_b[i], side)
        if key is not None:
            cast[key] = side_out
        j = i // 2
        if i % 2 == 0:
            key, side = next_side(i + 1, FFN_DENSE_STEPS)
            h_f, h_bf, side_out = _ffn_dense(h_bf, h_f, ffn_w_gate[j].astype(BF16), ffn_w_up[j].astype(BF16),
                                             ffn_w_down[j].astype(BF16), ln2_g[i][None, :], ln2_b[i][None, :],
                                             side)
            if key is not None:
                cast[key] = side_out
        else:
            final = i == DEPTH - 1
            wts = [cast[(i, k)].reshape(w[j].shape) if (i, k) in cast else w[j].astype(BF16)
                   for k, w in enumerate((moe_w_gate, moe_w_up, moe_w_down))]
            res = _moe_layer(h_f, router[j], wts[0], wts[1], wts[2], ln2_g[i], ln2_b[i], final)
            if final:
                return res
            h_f, h_bf = res
    return h_f.reshape(BATCH, LP, D)[:, PADF + N_META:, :]
```

```python
import functools

import jax
import jax.numpy as jnp
from jax import lax
from jax.experimental import pallas as pl
from jax.experimental.pallas import tpu as pltpu

F32 = jnp.float32
BF16 = jnp.bfloat16

D = 2048
BATCH = 4
SEQ = 4096
N_META = 16
DEPTH = 2
D_INNER = 2 * D
HEADDIM = 64
HEADS = D_INNER // HEADDIM
GROUPS = 8
HPG = HEADS // GROUPS
STATE = 128
GW = D_INNER // GROUPS
XBC = D_INNER + 2 * GROUPS * STATE
FF_DENSE = 5632
N_EXPERTS = 8
TOP_K = 2
FF_EXPERT = 7168
ALPHA = (2.0 * DEPTH) ** 0.25
LN_EPS = 1e-5
RMS_EPS = 1e-5
SSD_CONV_K = 4

CH = 256
PADF = CH - N_META
LP = PADF + N_META + SEQ
NCH = LP // CH
TP = BATCH * LP

C0, B0, V0 = 0, D, 2 * D
Z0 = 3 * D
X0 = Z0 + D_INNER
G0 = X0 + XBC
NP = G0

LANES = 128
SUBLANES = 8
BF16_SUBLANES = 16

GATE_OFF = HEADS
GATE_W = D + LANES

TM_MOE = 512
N_MOE_TILES = (2 * BATCH * (SEQ + N_META)) // TM_MOE + N_EXPERTS
P_MOE = N_MOE_TILES * TM_MOE
TF_MOE = 1024
TM_FFN = 256
TN_FFN = 2816
FFN_DENSE_STEPS = (TP // TM_FFN) * (FF_DENSE // TN_FFN)
TM_FFN_DOWN = 256
ROW_CHUNK = 128
TM_INPROJ = 1024
TN_INPROJ = 2048
IN_PROJ_STEPS = (NP // TN_INPROJ) * (TP // TM_INPROJ)
TOK_TILE = 256
COMBINE_TILE = CH
DT_ROWS = 256

SUB = 128
LOG2E = 1.4426950408889634
EXP2_CLAMP = 100.0

MIB = 1024 * 1024


def _cp(sem, vmem_mib):
    return pltpu.CompilerParams(dimension_semantics=sem, vmem_limit_bytes=int(vmem_mib * MIB))


def _sigmoid(x):
    return 0.5 * jnp.tanh(0.5 * x) + 0.5


def _silu(x):
    h = 0.5 * x
    return h + h * jnp.tanh(h)


def _softplus(x):
    return jnp.maximum(x, 0.0) + jnp.log1p(jnp.exp(-jnp.abs(x)))


def _layer_norm(x, g, b):
    mu = jnp.mean(x, axis=-1, keepdims=True)
    xc = x - mu
    var = jnp.mean(xc * xc, axis=-1, keepdims=True)
    return xc * lax.rsqrt(var + LN_EPS) * g + b


def _row_chunks(rows, size=ROW_CHUNK):
    return [slice(r, r + size) for r in range(0, rows, size)]


def _valid_rows(tile_idx, tm):
    row = lax.broadcasted_iota(jnp.int32, (tm, 1), 0)
    valid = None
    for s in range(tm // CH):
        chunk = tile_idx * (tm // CH) + s
        hi = jnp.where(chunk % NCH == 0, s * CH + PADF, s * CH)
        ok = jnp.logical_or(row < s * CH, row >= hi)
        valid = ok if valid is None else jnp.logical_and(valid, ok)
    return valid


def _ln_in_kernel(x_ref, meta_ref, g_ref, b_ref, hf_ref, hb_ref):
    j = pl.program_id(1)

    @pl.when(j == 0)
    def _():
        hf_ref[...] = jnp.zeros_like(hf_ref)
        hb_ref[...] = jnp.zeros_like(hb_ref)
        y = _layer_norm(meta_ref[...], g_ref[...], b_ref[...])
        hf_ref[PADF:, :] = y
        hb_ref[PADF:, :] = y.astype(BF16)

    @pl.when(j > 0)
    def _():
        y = _layer_norm(x_ref[0], g_ref[...], b_ref[...])
        hf_ref[...] = y
        hb_ref[...] = y.astype(BF16)


def _ln_in(x, meta, g, b):
    return pl.pallas_call(
        _ln_in_kernel,
        grid=(BATCH, NCH),
        in_specs=[
            pl.BlockSpec((1, CH, D), lambda bi, j: (bi, jnp.maximum(j - 1, 0), 0)),
            pl.BlockSpec((N_META, D), lambda bi, j: (0, 0)),
            pl.BlockSpec((1, D), lambda bi, j: (0, 0)),
            pl.BlockSpec((1, D), lambda bi, j: (0, 0)),
        ],
        out_specs=[
            pl.BlockSpec((CH, D), lambda bi, j: (bi * NCH + j, 0)),
            pl.BlockSpec((CH, D), lambda bi, j: (bi * NCH + j, 0)),
        ],
        out_shape=[jax.ShapeDtypeStruct((TP, D), F32), jax.ShapeDtypeStruct((TP, D), BF16)],
        compiler_params=_cp(("arbitrary", "arbitrary"), 32),
        name="ln_in",
    )(x, meta, g, b)


def _side_cast_specs(side, linear_step):
    src, rb = side
    nblk = src.shape[0] // rb
    imap = lambda *ids: (jnp.minimum(linear_step(*ids), nblk - 1), 0)
    spec = pl.BlockSpec((rb, src.shape[1]), imap)
    return nblk, spec, jax.ShapeDtypeStruct(src.shape, BF16)


def _side_block_rows(rows, steps):
    for nblk in range(min(steps, rows // BF16_SUBLANES), 0, -1):
        if rows % nblk == 0 and (rows // nblk) % BF16_SUBLANES == 0:
            return rows // nblk
    return rows


def _side_cast(nblk, step, src_ref, dst_ref):
    @pl.when(step < nblk)
    def _():
        dst_ref[...] = src_ref[...].astype(BF16)


def _mm_kernel(nblk, a_ref, w_ref, *rest):
    if nblk is None:
        (o_ref,) = rest
    else:
        c_ref, o_ref, co_ref = rest
        _side_cast(nblk, pl.program_id(0) * pl.num_programs(1) + pl.program_id(1), c_ref, co_ref)
    o_ref[...] = jnp.dot(a_ref[...], w_ref[...], preferred_element_type=F32).astype(o_ref.dtype)


def _in_proj(h_bf, w_main, side=None):
    tm, tn = TM_INPROJ, TN_INPROJ
    grid = (NP // tn, TP // tm)
    in_specs = [
        pl.BlockSpec((tm, D), lambda j, i: (i, 0)),
        pl.BlockSpec((D, tn), lambda j, i: (0, j)),
    ]
    out_specs = [pl.BlockSpec((tm, tn), lambda j, i: (i, j))]
    out_shape = [jax.ShapeDtypeStruct((TP, NP), BF16)]
    args = [h_bf, w_main]
    nblk = None
    if side is not None:
        nblk, spec, shape = _side_cast_specs(side, lambda j, i: j * grid[1] + i)
        in_specs.append(spec)
        out_specs.append(spec)
        out_shape.append(shape)
        args.append(side[0])
    res = pl.pallas_call(
        functools.partial(_mm_kernel, nblk),
        grid=grid,
        in_specs=in_specs,
        out_specs=out_specs,
        out_shape=out_shape,
        compiler_params=_cp(("arbitrary", "arbitrary"), 56),
        name="in_proj",
    )(*args)
    return res if side is not None else (res[0], None)


def _gate_proj(h_bf, w_gates):
    tm = TM_INPROJ
    return pl.pallas_call(
        functools.partial(_mm_kernel, None),
        grid=(2, TP // tm),
        in_specs=[
            pl.BlockSpec((tm, D), lambda j, i: (i, 0)),
            pl.BlockSpec((None, D, GATE_W), lambda j, i: (j, 0, 0)),
        ],
        out_specs=[pl.BlockSpec((None, tm, GATE_W), lambda j, i: (j, i, 0))],
        out_shape=[jax.ShapeDtypeStruct((2, TP, GATE_W), BF16)],
        compiler_params=_cp(("arbitrary", "arbitrary"), 56),
        name="gate_proj",
    )(h_bf, w_gates)[0]


def _dt_kernel(a_ref, w_ref, bias_ref, alog_ref, dt_ref, cum_ref):
    i = pl.program_id(0)
    raw = jnp.dot(a_ref[...], w_ref[...], preferred_element_type=F32) + bias_ref[...]
    dt = _softplus(raw)
    a_neg = -jnp.exp(alog_ref[...])
    da = jnp.where(_valid_rows(i, DT_ROWS), dt * a_neg, 0.0)
    r = lax.broadcasted_iota(jnp.int32, (DT_ROWS, DT_ROWS), 0)
    c = lax.broadcasted_iota(jnp.int32, (DT_ROWS, DT_ROWS), 1)
    same_chunk = None
    for s in range(DT_ROWS // CH):
        blk = jnp.logical_and(jnp.logical_and(r >= s * CH, r < (s + 1) * CH), c >= s * CH)
        same_chunk = blk if same_chunk is None else jnp.logical_or(same_chunk, blk)
    tril = jnp.logical_and(r >= c, same_chunk).astype(F32)
    cum = jnp.dot(tril, da, precision=lax.Precision.HIGHEST, preferred_element_type=F32)
    dt_ref[...] = dt
    cum_ref[...] = cum


def _dt_proj(h_bf, w_dt, dt_bias, a_log):
    return pl.pallas_call(
        _dt_kernel,
        grid=(TP // DT_ROWS,),
        in_specs=[
            pl.BlockSpec((DT_ROWS, D), lambda i: (i, 0)),
            pl.BlockSpec((D, LANES), lambda i: (0, 0)),
            pl.BlockSpec((1, LANES), lambda i: (0, 0)),
            pl.BlockSpec((1, LANES), lambda i: (0, 0)),
        ],
        out_specs=[
            pl.BlockSpec((DT_ROWS, LANES), lambda i: (i, 0)),
            pl.BlockSpec((DT_ROWS, LANES), lambda i: (i, 0)),
        ],
        out_shape=[jax.ShapeDtypeStruct((TP, LANES), F32), jax.ShapeDtypeStruct((TP, LANES), F32)],
        compiler_params=_cp(("arbitrary",), 16),
        name="dt_proj",
    )(h_bf, w_dt, dt_bias, a_log)


def _causal_conv(x, halo, w, k):
    acc = w[k - 1:k, :] * x
    for s in range(1, k):
        acc = acc + w[k - 1 - s:k - s, :] * pltpu.roll(x, s, 0)
    x8 = x[0:SUBLANES, :]
    h8 = halo[BF16_SUBLANES - SUBLANES:, :]
    row = lax.broadcasted_iota(jnp.int32, (SUBLANES, 1), 0)
    top = w[k - 1:k, :] * x8
    for s in range(1, k):
        shifted = jnp.where(row < s, pltpu.roll(h8, s, 0), pltpu.roll(x8, s, 0))
        top = top + w[k - 1 - s:k - s, :] * shifted
    return jnp.concatenate([top, acc[SUBLANES:, :]], axis=0)


def _mixa_kernel(c_ref, b_ref, v_ref, ch_ref, vh_ref, g_ref, cw_ref, wo_ref, bg_ref, o_ref):
    u = c_ref[...].astype(F32) * v_ref[...].astype(F32)
    hu = ch_ref[...].astype(F32) * vh_ref[...].astype(F32)
    conv = _causal_conv(u, hu, cw_ref[...], 3)
    lhs = (b_ref[...].astype(F32) * conv).astype(BF16)
    ya = jnp.dot(lhs, wo_ref[...], preferred_element_type=F32)
    ga = _sigmoid(g_ref[:, GATE_OFF:GATE_OFF + D].astype(F32) + bg_ref[...])
    o_ref[...] = (ga * ya).astype(BF16)


def _mixer_a(proj, gates, conv_w, w_out_a, bgate_a):
    tm = 512
    hb = tm // BF16_SUBLANES
    halo = lambda col: (lambda i: (jnp.maximum(i * hb - 1, 0), col))
    return pl.pallas_call(
        _mixa_kernel,
        grid=(TP // tm,),
        in_specs=[
            pl.BlockSpec((tm, D), lambda i: (i, C0 // D)),
            pl.BlockSpec((tm, D), lambda i: (i, B0 // D)),
            pl.BlockSpec((tm, D), lambda i: (i, V0 // D)),
            pl.BlockSpec((BF16_SUBLANES, D), halo(C0 // D)),
            pl.BlockSpec((BF16_SUBLANES, D), halo(V0 // D)),
            pl.BlockSpec((None, tm, GATE_W), lambda i: (0, i, 0)),
            pl.BlockSpec((3, D), lambda i: (0, 0)),
            pl.BlockSpec((D, D), lambda i: (0, 0), pipeline_mode=pl.Buffered(1)),
            pl.BlockSpec((1, D), lambda i: (0, 0)),
        ],
        out_specs=pl.BlockSpec((tm, D), lambda i: (i, 0)),
        out_shape=jax.ShapeDtypeStruct((TP, D), BF16),
        compiler_params=_cp(("arbitrary",), 56),
        name="mixer_a",
    )(proj, proj, proj, proj, proj, gates, conv_w, w_out_a, bgate_a)


def _ssd_kernel(xs_ref, bm_ref, cm_ref, z_ref, dt_ref, cum_ref,
                wx_ref, wb_ref, wc_ref, bx_ref, bb_ref, bc_ref, dsk_ref, nw_ref,
                o_ref, st_ref):
    g = pl.program_id(1)
    st_ref[...] = jnp.zeros_like(st_ref)

    rr = lax.broadcasted_iota(jnp.int32, (CH, CH), 0)
    cc = lax.broadcasted_iota(jnp.int32, (CH, CH), 1)
    shift_mats = [(rr - cc == s).astype(BF16) for s in range(1, SSD_CONV_K)]
    w_all = jnp.concatenate([wx_ref[...], wb_ref[...], wc_ref[...]], axis=1)
    b_all = jnp.concatenate([bx_ref[...], bb_ref[...], bc_ref[...]], axis=1)
    taps = [w_all[k:k + 1, :] for k in range(SSD_CONV_K)]
    lane_shift = (LANES - HPG * g) % LANES
    row = lax.broadcasted_iota(jnp.int32, (CH, 1), 0)
    row8 = lax.broadcasted_iota(jnp.int32, (SUBLANES, 1), 0)
    r2 = lax.broadcasted_iota(jnp.int32, (SUB, SUB), 0)
    c2 = lax.broadcasted_iota(jnp.int32, (SUB, SUB), 1)
    causal = r2 >= c2
    lane = lax.broadcasted_iota(jnp.int32, (1, LANES), 1)
    lo = lane < HEADDIM
    dskip = dsk_ref[...]
    norm_w = nw_ref[...]

    def chunk(c, carry):
        r0 = pl.multiple_of(c * CH, CH)
        h0 = pl.multiple_of(jnp.maximum(r0 - BF16_SUBLANES, 0), BF16_SUBLANES)
        rows = pl.ds(r0, CH)
        above = pl.ds(h0, BF16_SUBLANES)
        raw = jnp.concatenate([xs_ref[rows, :], bm_ref[rows, :], cm_ref[rows, :]], axis=1)
        halo = jnp.concatenate([xs_ref[above, :], bm_ref[above, :], cm_ref[above, :]], axis=1)
        h8 = halo[BF16_SUBLANES - SUBLANES:, :].astype(F32)

        acc = taps[SSD_CONV_K - 1] * raw.astype(F32) + b_all
        fix = jnp.zeros((SUBLANES, raw.shape[1]), F32)
        for s in range(1, SSD_CONV_K):
            tap = taps[SSD_CONV_K - 1 - s]
            acc = acc + tap * jnp.dot(shift_mats[s - 1], raw, preferred_element_type=F32)
            fix = fix + jnp.where(row8 < s, tap * pltpu.roll(h8, s, 0), 0.0)
        act = _silu(jnp.concatenate([acc[0:SUBLANES, :] + fix, acc[SUBLANES:, :]], axis=0))
        first = jnp.where(c == 0, PADF, 0)
        xs = jnp.where(row >= first, act[:, :GW], 0.0)
        bm = act[:, GW:GW + STATE]
        cm = act[:, GW + STATE:]

        dtg = pltpu.roll(dt_ref[rows, :], lane_shift, 1)
        cumg = pltpu.roll(cum_ref[rows, :], lane_shift, 1) * LOG2E
        bm_bf = bm.astype(BF16)
        cm_bf = cm.astype(BF16)
        xs_bf = xs.astype(BF16)

        st = st_ref[...]
        y_rows = []
        for q in range(CH // SUB):
            rs = slice(q * SUB, (q + 1) * SUB)
            cq = cumg[rs, :]
            if q > 0:
                cq = cq - cumg[q * SUB - 1:q * SUB, :]
            dq = dtg[rs, :]
            last = cq[SUB - 1:SUB, :]
            ecum = jnp.exp2(cq)
            wts = jnp.exp2(last - cq) * dq
            elast = jnp.exp2(last)
            src_t = (cq - jnp.log2(dq)).T

            cb = lax.dot_general(cm_bf[rs, :], bm_bf[rs, :], (((1,), (1,)), ((), ())),
                                 preferred_element_type=F32)
            cbm = jnp.where(causal, cb, 0.0)
            y_off = jnp.dot(cm_bf[rs, :], st.astype(BF16), preferred_element_type=F32)

            def head_mat(j):
                seg = cq[:, j:j + 1] - src_t[j:j + 1, :]
                return (cbm * jnp.exp2(jnp.minimum(seg, EXP2_CLAMP))).astype(BF16)

            ys, xws, els = [], [], []
            for p in range(GW // LANES):
                ja, jb = 2 * p, 2 * p + 1
                sl = slice(p * LANES, (p + 1) * LANES)
                xp = xs_bf[rs, sl]
                ya = jnp.dot(head_mat(ja), xp, preferred_element_type=F32)
                yb = jnp.dot(head_mat(jb), xp, preferred_element_type=F32)
                e_p = jnp.where(lo, ecum[:, ja:ja + 1], ecum[:, jb:jb + 1])
                ys.append(jnp.where(lo, ya, yb) + e_p * y_off[:, sl])
                w_p = jnp.where(lo, wts[:, ja:ja + 1], wts[:, jb:jb + 1])
                xws.append((xs[rs, sl] * w_p).astype(BF16))
                els.append(jnp.where(lo, elast[:, ja:ja + 1], elast[:, jb:jb + 1]))
            y_rows.append(jnp.concatenate(ys, axis=1))
            xw = jnp.concatenate(xws, axis=1)
            el = jnp.concatenate(els, axis=1)
            bm_t = bm[rs, :].T.astype(BF16)
            st = st * el + jnp.dot(bm_t, xw, preferred_element_type=F32)
        st_ref[...] = st
        y = jnp.concatenate(y_rows, axis=0)

        y = y + xs * dskip
        yg = y * _silu(z_ref[rows, :].astype(F32))
        ms = jnp.mean(yg * yg, axis=-1, keepdims=True)
        o_ref[rows, :] = (yg * lax.rsqrt(ms + RMS_EPS) * norm_w).astype(BF16)
        return carry

    lax.fori_loop(0, NCH, chunk, 0)


def _ssd(proj, dt, cum, conv_w, conv_b, dskip, norm_w):
    xcol = lambda g: X0 // GW + g
    bcol = lambda g: (X0 + D_INNER) // STATE + g
    ccol = lambda g: (X0 + D_INNER + GROUPS * STATE) // STATE + g
    zcol = lambda g: Z0 // GW + g
    wbcol = lambda g: D_INNER // STATE + g
    wccol = lambda g: (D_INNER + GROUPS * STATE) // STATE + g
    rows = lambda col_fn: (lambda b, g: (b, col_fn(g)))
    par = lambda col_fn: (lambda b, g: (0, col_fn(g)))
    return pl.pallas_call(
        _ssd_kernel,
        grid=(BATCH, GROUPS),
        in_specs=[
            pl.BlockSpec((LP, GW), rows(xcol)),
            pl.BlockSpec((LP, STATE), rows(bcol)),
            pl.BlockSpec((LP, STATE), rows(ccol)),
            pl.BlockSpec((LP, GW), rows(zcol)),
            pl.BlockSpec((LP, LANES), lambda b, g: (b, 0)),
            pl.BlockSpec((LP, LANES), lambda b, g: (b, 0)),
            pl.BlockSpec((SSD_CONV_K, GW), par(lambda g: g)),
            pl.BlockSpec((SSD_CONV_K, STATE), par(wbcol)),
            pl.BlockSpec((SSD_CONV_K, STATE), par(wccol)),
            pl.BlockSpec((1, GW), par(lambda g: g)),
            pl.BlockSpec((1, STATE), par(wbcol)),
            pl.BlockSpec((1, STATE), par(wccol)),
            pl.BlockSpec((1, GW), par(lambda g: g)),
            pl.BlockSpec((1, GW), par(lambda g: g)),
        ],
        out_specs=pl.BlockSpec((LP, GW), lambda b, g: (b, g)),
        out_shape=jax.ShapeDtypeStruct((TP, D_INNER), BF16),
        scratch_shapes=[pltpu.VMEM((STATE, GW), F32)],
        compiler_params=_cp(("arbitrary", "arbitrary"), 56),
        name="ssd",
    )(proj, proj, proj, proj, dt, cum,
      conv_w, conv_w, conv_w, conv_b, conv_b, conv_b, dskip, norm_w)


def _outb_kernel(y_ref, w_ref, g_ref, bg_ref, ya_ref, o_ref):
    for rs in _row_chunks(y_ref.shape[0]):
        yb = jnp.dot(y_ref[rs, :], w_ref[...], preferred_element_type=F32)
        gb = _sigmoid(g_ref[rs, GATE_OFF:GATE_OFF + D].astype(F32) + bg_ref[...])
        o_ref[rs, :] = (gb * yb + ya_ref[rs, :].astype(F32)).astype(BF16)


def _out_b(yn, w_out_b, gates, bgate_b, ga_ya):
    tm = 512
    return pl.pallas_call(
        _outb_kernel,
        grid=(TP // tm,),
        in_specs=[
            pl.BlockSpec((tm, D_INNER), lambda i: (i, 0)),
            pl.BlockSpec((D_INNER, D), lambda i: (0, 0), pipeline_mode=pl.Buffered(1)),
            pl.BlockSpec((None, tm, GATE_W), lambda i: (1, i, 0)),
            pl.BlockSpec((1, D), lambda i: (0, 0)),
            pl.BlockSpec((tm, D), lambda i: (i, 0)),
        ],
        out_specs=pl.BlockSpec((tm, D), lambda i: (i, 0)),
        out_shape=jax.ShapeDtypeStruct((TP, D), BF16),
        compiler_params=_cp(("arbitrary",), 56),
        name="out_b",
    )(yn, w_out_b, gates, bgate_b, ga_ya)


def _wo_kernel(m_ref, w_ref, h_ref, g_ref, b_ref, hf_ref, hb_ref):
    for rs in _row_chunks(m_ref.shape[0]):
        mix = jnp.dot(m_ref[rs, :], w_ref[...], preferred_element_type=F32)
        y = _layer_norm(ALPHA * h_ref[rs, :] + mix, g_ref[...], b_ref[...])
        hf_ref[rs, :] = y
        hb_ref[rs, :] = y.astype(BF16)


def _wo_ln(m, w_o, h, g, b):
    tm = 512
    return pl.pallas_call(
        _wo_kernel,
        grid=(TP // tm,),
        in_specs=[
            pl.BlockSpec((tm, D), lambda i: (i, 0)),
            pl.BlockSpec((D, D), lambda i: (0, 0), pipeline_mode=pl.Buffered(1)),
            pl.BlockSpec((tm, D), lambda i: (i, 0)),
            pl.BlockSpec((1, D), lambda i: (0, 0)),
            pl.BlockSpec((1, D), lambda i: (0, 0)),
        ],
        out_specs=[
            pl.BlockSpec((tm, D), lambda i: (i, 0)),
            pl.BlockSpec((tm, D), lambda i: (i, 0)),
        ],
        out_shape=[jax.ShapeDtypeStruct((TP, D), F32), jax.ShapeDtypeStruct((TP, D), BF16)],
        compiler_params=_cp(("arbitrary",), 56),
        name="wo_ln",
    )(m, w_o, h, g, b)


def _swiglu_step(x_bf, wg_ref, wu_ref, wd_ref):
    gate = jnp.dot(x_bf, wg_ref[...], preferred_element_type=F32)
    up = jnp.dot(x_bf, wu_ref[...], preferred_element_type=F32)
    mid = (_silu(gate) * up).astype(BF16)
    return jnp.dot(mid, wd_ref[...], preferred_element_type=F32)


def _ffn_gu_kernel(nblk, x_ref, wg_ref, wu_ref, *rest):
    if nblk is None:
        (o_ref,) = rest
    else:
        c_ref, o_ref, co_ref = rest
        _side_cast(nblk, pl.program_id(0) * pl.num_programs(1) + pl.program_id(1), c_ref, co_ref)
    for rs in _row_chunks(x_ref.shape[0]):
        x = x_ref[rs, :]
        gate = jnp.dot(x, wg_ref[...], preferred_element_type=F32)
        up = jnp.dot(x, wu_ref[...], preferred_element_type=F32)
        o_ref[rs, :] = (_silu(gate) * up).astype(BF16)


def _ffn_down_kernel(a_ref, w_ref, h_ref, g_ref, b_ref, hf_ref, hb_ref):
    valid = _valid_rows(pl.program_id(0), TM_FFN_DOWN)
    for rs in _row_chunks(TM_FFN_DOWN):
        f = jnp.dot(a_ref[rs, :], w_ref[...], preferred_element_type=F32)
        y = _layer_norm(ALPHA * h_ref[rs, :] + f, g_ref[...], b_ref[...])
        y = jnp.where(valid[rs, :], y, 0.0)
        hf_ref[rs, :] = y
        hb_ref[rs, :] = y.astype(BF16)


def _ffn_dense(h_bf, h_f, wg, wu, wd, g, b, side=None):
    tm, tn = TM_FFN, TN_FFN
    grid = (FF_DENSE // tn, TP // tm)
    in_specs = [
        pl.BlockSpec((tm, D), lambda j, i: (i, 0)),
        pl.BlockSpec((D, tn), lambda j, i: (0, j), pipeline_mode=pl.Buffered(1)),
        pl.BlockSpec((D, tn), lambda j, i: (0, j), pipeline_mode=pl.Buffered(1)),
    ]
    out_specs = [pl.BlockSpec((tm, tn), lambda j, i: (i, j))]
    out_shape = [jax.ShapeDtypeStruct((TP, FF_DENSE), BF16)]
    args = [h_bf, wg, wu]
    nblk = None
    if side is not None:
        nblk, spec, shape = _side_cast_specs(side, lambda j, i: j * grid[1] + i)
        in_specs.append(spec)
        out_specs.append(spec)
        out_shape.append(shape)
        args.append(side[0])
    res = pl.pallas_call(
        functools.partial(_ffn_gu_kernel, nblk),
        grid=grid,
        in_specs=in_specs,
        out_specs=out_specs,
        out_shape=out_shape,
        compiler_params=_cp(("arbitrary", "arbitrary"), 56),
        name="ffn_gate_up",
    )(*args)
    mid, side_out = (res[0], res[1]) if side is not None else (res[0], None)

    tmd = TM_FFN_DOWN
    h_f, h_bf = pl.pallas_call(
        _ffn_down_kernel,
        grid=(TP // tmd,),
        in_specs=[
            pl.BlockSpec((tmd, FF_DENSE), lambda i: (i, 0)),
            pl.BlockSpec((FF_DENSE, D), lambda i: (0, 0), pipeline_mode=pl.Buffered(1)),
            pl.BlockSpec((tmd, D), lambda i: (i, 0)),
            pl.BlockSpec((1, D), lambda i: (0, 0)),
            pl.BlockSpec((1, D), lambda i: (0, 0)),
        ],
        out_specs=[pl.BlockSpec((tmd, D), lambda i: (i, 0)), pl.BlockSpec((tmd, D), lambda i: (i, 0))],
        out_shape=[jax.ShapeDtypeStruct((TP, D), F32), jax.ShapeDtypeStruct((TP, D), BF16)],
        compiler_params=_cp(("arbitrary",), 56),
        name="ffn_down",
    )(mid, wd, h_f, g, b)
    return h_f, h_bf, side_out


def _router_kernel(h_ref, r_ref, meta_ref, cnt_ref, carry_ref):
    i = pl.program_id(0)
    tm = TOK_TILE

    @pl.when(i == 0)
    def _():
        carry_ref[...] = jnp.zeros_like(carry_ref)

    logits = jnp.dot(h_ref[...], r_ref[...], precision=lax.Precision.HIGHEST,
                     preferred_element_type=F32)
    lane = lax.broadcasted_iota(jnp.int32, (tm, LANES), 1).astype(F32)
    neg = jnp.float32(-jnp.inf)
    logits = jnp.where(lane < N_EXPERTS, logits, neg)
    m1 = jnp.max(logits, axis=-1, keepdims=True)
    i1 = jnp.min(jnp.where(logits == m1, lane, float(LANES)), axis=-1, keepdims=True)
    rest = jnp.where(lane == i1, neg, logits)
    m2 = jnp.max(rest, axis=-1, keepdims=True)
    i2 = jnp.min(jnp.where(rest == m2, lane, float(LANES)), axis=-1, keepdims=True)
    e = jnp.exp(m2 - m1)
    w1 = 1.0 / (1.0 + e)
    w2 = e / (1.0 + e)

    valid = _valid_rows(i, tm)
    sel = jnp.logical_and(jnp.logical_or(lane == i1, lane == i2), valid)
    onehot = sel.astype(F32)
    r = lax.broadcasted_iota(jnp.int32, (tm, tm), 0)
    c = lax.broadcasted_iota(jnp.int32, (tm, tm), 1)
    strict = (r > c).astype(BF16)
    before = jnp.dot(strict, onehot.astype(BF16), preferred_element_type=F32) + carry_ref[0:1, :]
    rank1 = jnp.sum(jnp.where(lane == i1, before, 0.0), axis=-1, keepdims=True)
    rank2 = jnp.sum(jnp.where(lane == i2, before, 0.0), axis=-1, keepdims=True)
    total = carry_ref[0:1, :] + jnp.sum(onehot, axis=0, keepdims=True)
    carry_ref[...] = jnp.broadcast_to(total, carry_ref.shape)
    cnt_ref[...] = jnp.broadcast_to(total, cnt_ref.shape)

    meta = jnp.where(lane == 0, i1, 0.0)
    meta = jnp.where(lane == 1, i2, meta)
    meta = jnp.where(lane == 2, w1, meta)
    meta = jnp.where(lane == 3, w2, meta)
    meta = jnp.where(lane == 4, rank1, meta)
    meta = jnp.where(lane == 5, rank2, meta)
    meta_ref[...] = meta


def _router(h_f, router_w):
    tm = TOK_TILE
    return pl.pallas_call(
        _router_kernel,
        grid=(TP // tm,),
        in_specs=[
            pl.BlockSpec((tm, D), lambda i: (i, 0)),
            pl.BlockSpec((D, LANES), lambda i: (0, 0)),
        ],
        out_specs=[
            pl.BlockSpec((tm, LANES), lambda i: (i, 0)),
            pl.BlockSpec((8, LANES), lambda i: (0, 0)),
        ],
        out_shape=[jax.ShapeDtypeStruct((TP, LANES), F32), jax.ShapeDtypeStruct((8, LANES), F32)],
        scratch_shapes=[pltpu.VMEM((8, LANES), F32)],
        compiler_params=_cp(("arbitrary",), 32),
        name="router",
    )(h_f, router_w)


def _row_dma_ops(copy, count):
    def start(t, carry):
        copy(t).start()
        return carry

    def wait(t, carry):
        copy(t).wait()
        return carry

    return (lambda: lax.fori_loop(0, count, start, 0, unroll=8),
            lambda: lax.fori_loop(0, count, wait, 0, unroll=8))


def _gather_kernel(nu_ref, cur_ref, nxt_ref, h_ref, o_ref, buf, sem):
    i = pl.program_id(0)
    nu = nu_ref[0]
    slot = i % 2

    def copies(idx_ref, s):
        return _row_dma_ops(
            lambda t: pltpu.make_async_copy(h_ref.at[pl.ds(idx_ref[0, 0, t], 1)],
                                            buf.at[s, pl.ds(t, 1)], sem.at[s]), TM_MOE)

    @pl.when(jnp.logical_and(i == 0, nu > 0))
    def _():
        copies(cur_ref, 0)[0]()

    @pl.when(i + 1 < nu)
    def _():
        copies(nxt_ref, 1 - slot)[0]()

    @pl.when(i < nu)
    def _():
        copies(cur_ref, slot)[1]()
        o_ref[...] = buf[slot].astype(BF16)

    @pl.when(i >= nu)
    def _():
        o_ref[...] = jnp.zeros_like(o_ref)


def _gather_rows(n_used, src, h_f):
    last = N_MOE_TILES - 1
    grid_spec = pltpu.PrefetchScalarGridSpec(
        num_scalar_prefetch=1,
        grid=(N_MOE_TILES,),
        in_specs=[
            pl.BlockSpec((1, 1, TM_MOE), lambda i, nu: (i, 0, 0), memory_space=pltpu.SMEM),
            pl.BlockSpec((1, 1, TM_MOE), lambda i, nu: (jnp.minimum(i + 1, last), 0, 0),
                         memory_space=pltpu.SMEM),
            pl.BlockSpec(memory_space=pl.ANY),
        ],
        out_specs=pl.BlockSpec((TM_MOE, D), lambda i, nu: (i, 0)),
        scratch_shapes=[pltpu.VMEM((2, TM_MOE, D), F32), pltpu.SemaphoreType.DMA((2,))],
    )
    return pl.pallas_call(
        _gather_kernel,
        grid_spec=grid_spec,
        out_shape=jax.ShapeDtypeStruct((P_MOE, D), BF16),
        compiler_params=_cp(("arbitrary",), 32),
        name="moe_gather",
    )(n_used, src, src, h_f)


def _ffn_moe_kernel(te_ref, nu_ref, x_ref, wg_ref, wu_ref, wd_ref, o_ref, acc_ref):
    del te_ref
    i = pl.program_id(0)
    f = pl.program_id(1)
    used = i < nu_ref[0]

    @pl.when(jnp.logical_and(used, f == 0))
    def _():
        acc_ref[...] = jnp.zeros_like(acc_ref)

    @pl.when(used)
    def _():
        acc_ref[...] += _swiglu_step(x_ref[...], wg_ref, wu_ref, wd_ref)

    @pl.when(jnp.logical_and(used, f == pl.num_programs(1) - 1))
    def _():
        o_ref[...] = acc_ref[...]

    @pl.when(jnp.logical_and(jnp.logical_not(used), f == 0))
    def _():
        o_ref[...] = jnp.zeros_like(o_ref)


def _ffn_moe(tile_e, n_used, xs, wg, wu, wd):
    tm, tf = TM_MOE, TF_MOE
    nf = FF_EXPERT // tf

    def fblk(i, f, nu):
        return jnp.where(i < nu[0], f, nf - 1)

    grid_spec = pltpu.PrefetchScalarGridSpec(
        num_scalar_prefetch=2,
        grid=(N_MOE_TILES, nf),
        in_specs=[
            pl.BlockSpec((tm, D), lambda i, f, te, nu: (i, 0)),
            pl.BlockSpec((None, D, tf), lambda i, f, te, nu: (te[i], 0, fblk(i, f, nu))),
            pl.BlockSpec((None, D, tf), lambda i, f, te, nu: (te[i], 0, fblk(i, f, nu))),
            pl.BlockSpec((None, tf, D), lambda i, f, te, nu: (te[i], fblk(i, f, nu), 0)),
        ],
        out_specs=pl.BlockSpec((tm, D), lambda i, f, te, nu: (i, 0)),
        scratch_shapes=[pltpu.VMEM((tm, D), F32)],
    )
    return pl.pallas_call(
        _ffn_moe_kernel,
        grid_spec=grid_spec,
        out_shape=jax.ShapeDtypeStruct((P_MOE, D), F32),
        compiler_params=_cp(("arbitrary", "arbitrary"), 58),
        name="ffn_moe",
    )(tile_e, n_used, xs, wg, wu, wd)


def _combine_kernel(final, cur_ref, nxt_ref, y_ref, meta_ref, h_ref, g_ref, b_ref, *rest):
    if final:
        out_ref, buf, sem = rest
    else:
        hf_ref, hb_ref, buf, sem = rest
    i = pl.program_id(0)
    tm = COMBINE_TILE
    slot = i % 2

    def copies(idx_ref, s):
        def copy(t, k):
            return pltpu.make_async_copy(y_ref.at[pl.ds(idx_ref[0, 0, TOP_K * t + k], 1)],
                                         buf.at[s, k, pl.ds(t, 1)], sem.at[s])

        def start(t, carry):
            for k in range(TOP_K):
                copy(t, k).start()
            return carry

        def wait(t, carry):
            for k in range(TOP_K):
                copy(t, k).wait()
            return carry

        return (lambda: lax.fori_loop(0, tm, start, 0, unroll=8),
                lambda: lax.fori_loop(0, tm, wait, 0, unroll=8))

    @pl.when(i == 0)
    def _():
        copies(cur_ref, 0)[0]()

    @pl.when(i + 1 < pl.num_programs(0))
    def _():
        copies(nxt_ref, 1 - slot)[0]()

    copies(cur_ref, slot)[1]()

    meta = meta_ref[...]
    w1 = meta[:, 2:3]
    w2 = meta[:, 3:4]
    f = w1 * buf[slot, 0] + w2 * buf[slot, 1]
    y = _layer_norm(ALPHA * h_ref[...] + f, g_ref[...], b_ref[...])
    if final:
        out_ref[0] = y
    else:
        y = jnp.where(_valid_rows(i, tm), y, 0.0)
        hf_ref[...] = y
        hb_ref[...] = y.astype(BF16)


def _combine(pos, y, meta, h_f, g, b, final):
    tm = COMBINE_TILE
    if final:
        out_specs = pl.BlockSpec((1, tm, D), lambda i: (i // NCH, jnp.maximum(i % NCH - 1, 0), 0))
        out_shape = jax.ShapeDtypeStruct((BATCH, SEQ, D), F32)
    else:
        out_specs = [pl.BlockSpec((tm, D), lambda i: (i, 0)), pl.BlockSpec((tm, D), lambda i: (i, 0))]
        out_shape = [jax.ShapeDtypeStruct((TP, D), F32), jax.ShapeDtypeStruct((TP, D), BF16)]
    last = TP // tm - 1
    return pl.pallas_call(
        functools.partial(_combine_kernel, final),
        grid=(TP // tm,),
        in_specs=[
            pl.BlockSpec((1, 1, 2 * tm), lambda i: (i, 0, 0), memory_space=pltpu.SMEM),
            pl.BlockSpec((1, 1, 2 * tm), lambda i: (jnp.minimum(i + 1, last), 0, 0), memory_space=pltpu.SMEM),
            pl.BlockSpec(memory_space=pl.ANY),
            pl.BlockSpec((tm, LANES), lambda i: (i, 0)),
            pl.BlockSpec((tm, D), lambda i: (i, 0)),
            pl.BlockSpec((1, D), lambda i: (0, 0)),
            pl.BlockSpec((1, D), lambda i: (0, 0)),
        ],
        out_specs=out_specs,
        out_shape=out_shape,
        scratch_shapes=[pltpu.VMEM((2, TOP_K, tm, D), F32), pltpu.SemaphoreType.DMA((2,))],
        compiler_params=_cp(("arbitrary",), 32),
        name="moe_combine",
    )(pos, pos, y, meta, h_f, g, b)


def _pad_lanes(v, width=LANES):
    return jnp.pad(v, ((0, 0), (0, width - v.shape[1])))


def _token_mixer(h_f, h_bf, w_in, b_gate, conv_a_w, w_out_a, ssd_conv_w, ssd_conv_b, dt_bias, a_log,
                 d_skip, ssd_norm_w, w_out_b, w_o, ln_g, ln_b, side=None):
    w_bf = w_in.astype(BF16)
    lane = jnp.arange(LANES)[None, :]
    w_dt = jnp.where(lane < HEADS, w_in[:, G0:G0 + LANES], 0.0).astype(BF16)
    win_a = w_bf[:, G0:G0 + GATE_W]
    win_b = w_bf[:, G0 + D:]
    win_b = jnp.pad(win_b, ((0, 0), (0, GATE_W - win_b.shape[1])))
    w_gates = jnp.stack([win_a, win_b])
    proj, side_out = _in_proj(h_bf, w_bf, side)
    gates = _gate_proj(h_bf, w_gates)
    dt, cum = _dt_proj(h_bf, w_dt, _pad_lanes(dt_bias[None, :]), _pad_lanes(a_log[None, :]))
    ga_ya = _mixer_a(proj, gates, conv_a_w, w_out_a.astype(BF16), b_gate[None, :D])
    dskip = jnp.repeat(d_skip, HEADDIM)[None, :]
    yn = _ssd(proj, dt, cum, ssd_conv_w, ssd_conv_b[None, :], dskip, ssd_norm_w[None, :])
    m = _out_b(yn, w_out_b.astype(BF16), gates, b_gate[None, D:], ga_ya)
    h_f, h_bf = _wo_ln(m, w_o.astype(BF16), h_f, ln_g[None, :], ln_b[None, :])
    return h_f, h_bf, side_out


def _moe_layer(h_f, router_w, wg, wu, wd, ln_g, ln_b, final):
    meta, cnt = _router(h_f, _pad_lanes(router_w))
    i1 = meta[:, 0].astype(jnp.int32)
    i2 = meta[:, 1].astype(jnp.int32)
    counts = cnt[0, :N_EXPERTS].astype(jnp.int32)
    padded = ((counts + TM_MOE - 1) // TM_MOE) * TM_MOE
    ends = jnp.cumsum(padded)
    starts = ends - padded
    pos = jnp.stack([starts[i1] + meta[:, 4].astype(jnp.int32),
                     starts[i2] + meta[:, 5].astype(jnp.int32)], axis=1)
    pos = jnp.clip(pos, 0, P_MOE - 1)
    tok = jnp.arange(TP, dtype=jnp.int32)
    valid = (tok % LP) >= PADF
    dst = jnp.where(valid[:, None], pos, P_MOE).reshape(-1)
    src = jnp.zeros((P_MOE,), jnp.int32).at[dst].set(jnp.repeat(tok, 2), mode="drop")
    src = src.reshape(N_MOE_TILES, 1, TM_MOE)
    pos = pos.reshape(TP // COMBINE_TILE, 1, TOP_K * COMBINE_TILE)
    n_used = (ends[-1] // TM_MOE).reshape(1).astype(jnp.int32)
    tile_start = jnp.arange(N_MOE_TILES, dtype=jnp.int32) * TM_MOE
    tile_e = jnp.sum((tile_start[:, None] >= ends[None, :]).astype(jnp.int32), axis=1)
    last_e = jnp.minimum(tile_e[jnp.maximum(n_used[0] - 1, 0)], N_EXPERTS - 1)
    tile_e = jnp.where(tile_start < ends[-1], jnp.minimum(tile_e, N_EXPERTS - 1), last_e)
    xs = _gather_rows(n_used, src, h_f)
    y = _ffn_moe(tile_e, n_used, xs, wg, wu, wd)
    return _combine(pos, y, meta, h_f, ln_g[None, :], ln_b[None, :], final)


def kernel(x, meta_tokens, ln_in_g, ln_in_b, w_in, b_gate, conv_a_w, w_out_a, ssd_conv_w, ssd_conv_b,
           dt_bias, a_log, d_skip, ssd_norm_w, w_out_b, w_o, ln1_g, ln1_b, ffn_w_gate, ffn_w_up,
           ffn_w_down, router, moe_w_gate, moe_w_up, moe_w_down, ln2_g, ln2_b):
    h_f, h_bf = _ln_in(x, meta_tokens.astype(x.dtype), ln_in_g[None, :], ln_in_b[None, :])

    pending = [((i, k), w[i // 2]) for i in range(DEPTH) if i % 2 == 1
               for k, w in enumerate((moe_w_gate, moe_w_up, moe_w_down))]
    cast = {}

    def next_side(layer, steps):
        while pending and pending[0][0][0] < layer:
            pending.pop(0)
        if not pending:
            return None, None
        key, w = pending.pop(0)
        w2d = w.reshape(-1, w.shape[-1])
        return key, (w2d, _side_block_rows(w2d.shape[0], steps))

    for i in range(DEPTH):
        key, side = next_side(i, IN_PROJ_STEPS)
        h_f, h_bf, side_out = _token_mixer(h_f, h_bf, w_in[i], b_gate[i], conv_a_w[i], w_out_a[i],
                                           ssd_conv_w[i], ssd_conv_b[i], dt_bias[i], a_log[i], d_skip[i],
                                           ssd_norm_w[i], w_out_b[i], w_o[i], ln1_g[i], ln1_b[i], side)
        if key is not None:
            cast[key] = side_out
        j = i // 2
        if i % 2 == 0:
            key, side = next_side(i + 1, FFN_DENSE_STEPS)
            h_f, h_bf, side_out = _ffn_dense(h_bf, h_f, ffn_w_gate[j].astype(BF16), ffn_w_up[j].astype(BF16),
                                             ffn_w_down[j].astype(BF16), ln2_g[i][None, :], ln2_b[i][None, :],
                                             side)
            if key is not None:
                cast[key] = side_out
        else:
            final = i == DEPTH - 1
            wts = [cast[(i, k)].reshape(w[j].shape) if (i, k) in cast else w[j].astype(BF16)
                   for k, w in enumerate((moe_w_gate, moe_w_up, moe_w_down))]
            res = _moe_layer(h_f, router[j], wts[0], wts[1], wts[2], ln2_g[i], ln2_b[i], final)
            if final:
                return res
            h_f, h_bf = res
    return h_f.reshape(BATCH, LP, D)[:, PADF + N_META:, :]
```

```python
import functools

import jax
import jax.numpy as jnp
from jax import lax
from jax.experimental import pallas as pl
from jax.experimental.pallas import tpu as pltpu

F32 = jnp.float32
BF16 = jnp.bfloat16

D = 2048
BATCH = 4
SEQ = 4096
N_META = 16
DEPTH = 2
D_INNER = 2 * D
HEADDIM = 64
HEADS = D_INNER // HEADDIM
GROUPS = 8
HPG = HEADS // GROUPS
STATE = 128
GW = D_INNER // GROUPS
XBC = D_INNER + 2 * GROUPS * STATE
FF_DENSE = 5632
N_EXPERTS = 8
TOP_K = 2
DMA_PRIORITIES = 2
FF_EXPERT = 7168
ALPHA = (2.0 * DEPTH) ** 0.25
LN_EPS = 1e-5
RMS_EPS = 1e-5
SSD_CONV_K = 4

CH = 256
PADF = CH - N_META
LP = PADF + N_META + SEQ
NCH = LP // CH
TP = BATCH * LP

C0, B0, V0 = 0, D, 2 * D
Z0 = 3 * D
X0 = Z0 + D_INNER
G0 = X0 + XBC
NP = G0

LANES = 128
SUBLANES = 8
BF16_SUBLANES = 16

GATE_OFF = HEADS
GATE_W = D + LANES

TM_MOE = 512
N_MOE_TILES = (2 * BATCH * (SEQ + N_META)) // TM_MOE + N_EXPERTS
P_MOE = N_MOE_TILES * TM_MOE
TF_MOE = 1024
TM_FFN = 256
TN_FFN = 2816
FFN_DENSE_STEPS = (TP // TM_FFN) * (FF_DENSE // TN_FFN)
TM_FFN_DOWN = 256
ROW_CHUNK = 128
TM_INPROJ = 1024
TN_INPROJ = 2048
IN_PROJ_STEPS = (NP // TN_INPROJ) * (TP // TM_INPROJ)
TOK_TILE = 256
COMBINE_TILE = CH
DT_ROWS = 256

SUB = 128
LOG2E = 1.4426950408889634
EXP2_CLAMP = 100.0

MIB = 1024 * 1024


def _cp(sem, vmem_mib):
    return pltpu.CompilerParams(dimension_semantics=sem, vmem_limit_bytes=int(vmem_mib * MIB))


def _sigmoid(x):
    return 0.5 * jnp.tanh(0.5 * x) + 0.5


def _silu(x):
    h = 0.5 * x
    return h + h * jnp.tanh(h)


def _softplus(x):
    return jnp.maximum(x, 0.0) + jnp.log1p(jnp.exp(-jnp.abs(x)))


def _layer_norm(x, g, b):
    mu = jnp.mean(x, axis=-1, keepdims=True)
    xc = x - mu
    var = jnp.mean(xc * xc, axis=-1, keepdims=True)
    return xc * lax.rsqrt(var + LN_EPS) * g + b


def _row_chunks(rows, size=ROW_CHUNK):
    return [slice(r, r + size) for r in range(0, rows, size)]


def _valid_rows(tile_idx, tm):
    row = lax.broadcasted_iota(jnp.int32, (tm, 1), 0)
    valid = None
    for s in range(tm // CH):
        chunk = tile_idx * (tm // CH) + s
        hi = jnp.where(chunk % NCH == 0, s * CH + PADF, s * CH)
        ok = jnp.logical_or(row < s * CH, row >= hi)
        valid = ok if valid is None else jnp.logical_and(valid, ok)
    return valid


def _ln_in_kernel(x_ref, meta_ref, g_ref, b_ref, hf_ref, hb_ref):
    j = pl.program_id(1)

    @pl.when(j == 0)
    def _():
        hf_ref[...] = jnp.zeros_like(hf_ref)
        hb_ref[...] = jnp.zeros_like(hb_ref)
        y = _layer_norm(meta_ref[...], g_ref[...], b_ref[...])
        hf_ref[PADF:, :] = y
        hb_ref[PADF:, :] = y.astype(BF16)

    @pl.when(j > 0)
    def _():
        y = _layer_norm(x_ref[0], g_ref[...], b_ref[...])
        hf_ref[...] = y
        hb_ref[...] = y.astype(BF16)


def _ln_in(x, meta, g, b):
    return pl.pallas_call(
        _ln_in_kernel,
        grid=(BATCH, NCH),
        in_specs=[
            pl.BlockSpec((1, CH, D), lambda bi, j: (bi, jnp.maximum(j - 1, 0), 0)),
            pl.BlockSpec((N_META, D), lambda bi, j: (0, 0)),
            pl.BlockSpec((1, D), lambda bi, j: (0, 0)),
            pl.BlockSpec((1, D), lambda bi, j: (0, 0)),
        ],
        out_specs=[
            pl.BlockSpec((CH, D), lambda bi, j: (bi * NCH + j, 0)),
            pl.BlockSpec((CH, D), lambda bi, j: (bi * NCH + j, 0)),
        ],
        out_shape=[jax.ShapeDtypeStruct((TP, D), F32), jax.ShapeDtypeStruct((TP, D), BF16)],
        compiler_params=_cp(("arbitrary", "arbitrary"), 32),
        name="ln_in",
    )(x, meta, g, b)


def _side_cast_specs(side, linear_step):
    src, rb = side
    nblk = src.shape[0] // rb
    imap = lambda *ids: (jnp.minimum(linear_step(*ids), nblk - 1), 0)
    spec = pl.BlockSpec((rb, src.shape[1]), imap)
    return nblk, spec, jax.ShapeDtypeStruct(src.shape, BF16)


def _side_block_rows(rows, steps):
    for nblk in range(min(steps, rows // BF16_SUBLANES), 0, -1):
        if rows % nblk == 0 and (rows // nblk) % BF16_SUBLANES == 0:
            return rows // nblk
    return rows


def _side_cast(nblk, step, src_ref, dst_ref):
    @pl.when(step < nblk)
    def _():
        dst_ref[...] = src_ref[...].astype(BF16)


def _mm_kernel(nblk, a_ref, w_ref, *rest):
    if nblk is None:
        (o_ref,) = rest
    else:
        c_ref, o_ref, co_ref = rest
        _side_cast(nblk, pl.program_id(0) * pl.num_programs(1) + pl.program_id(1), c_ref, co_ref)
    o_ref[...] = jnp.dot(a_ref[...], w_ref[...], preferred_element_type=F32).astype(o_ref.dtype)


def _in_proj(h_bf, w_main, side=None):
    tm, tn = TM_INPROJ, TN_INPROJ
    grid = (NP // tn, TP // tm)
    in_specs = [
        pl.BlockSpec((tm, D), lambda j, i: (i, 0)),
        pl.BlockSpec((D, tn), lambda j, i: (0, j)),
    ]
    out_specs = [pl.BlockSpec((tm, tn), lambda j, i: (i, j))]
    out_shape = [jax.ShapeDtypeStruct((TP, NP), BF16)]
    args = [h_bf, w_main]
    nblk = None
    if side is not None:
        nblk, spec, shape = _side_cast_specs(side, lambda j, i: j * grid[1] + i)
        in_specs.append(spec)
        out_specs.append(spec)
        out_shape.append(shape)
        args.append(side[0])
    res = pl.pallas_call(
        functools.partial(_mm_kernel, nblk),
        grid=grid,
        in_specs=in_specs,
        out_specs=out_specs,
        out_shape=out_shape,
        compiler_params=_cp(("arbitrary", "arbitrary"), 56),
        name="in_proj",
    )(*args)
    return res if side is not None else (res[0], None)


def _gate_proj(h_bf, w_gates):
    tm = TM_INPROJ
    return pl.pallas_call(
        functools.partial(_mm_kernel, None),
        grid=(2, TP // tm),
        in_specs=[
            pl.BlockSpec((tm, D), lambda j, i: (i, 0)),
            pl.BlockSpec((None, D, GATE_W), lambda j, i: (j, 0, 0)),
        ],
        out_specs=[pl.BlockSpec((None, tm, GATE_W), lambda j, i: (j, i, 0))],
        out_shape=[jax.ShapeDtypeStruct((2, TP, GATE_W), BF16)],
        compiler_params=_cp(("arbitrary", "arbitrary"), 56),
        name="gate_proj",
    )(h_bf, w_gates)[0]


def _dt_kernel(a_ref, w_ref, bias_ref, alog_ref, dt_ref, cum_ref):
    i = pl.program_id(0)
    raw = jnp.dot(a_ref[...], w_ref[...], preferred_element_type=F32) + bias_ref[...]
    dt = _softplus(raw)
    a_neg = -jnp.exp(alog_ref[...])
    da = jnp.where(_valid_rows(i, DT_ROWS), dt * a_neg, 0.0)
    r = lax.broadcasted_iota(jnp.int32, (DT_ROWS, DT_ROWS), 0)
    c = lax.broadcasted_iota(jnp.int32, (DT_ROWS, DT_ROWS), 1)
    same_chunk = None
    for s in range(DT_ROWS // CH):
        blk = jnp.logical_and(jnp.logical_and(r >= s * CH, r < (s + 1) * CH), c >= s * CH)
        same_chunk = blk if same_chunk is None else jnp.logical_or(same_chunk, blk)
    tril = jnp.logical_and(r >= c, same_chunk).astype(F32)
    cum = jnp.dot(tril, da, precision=lax.Precision.HIGHEST, preferred_element_type=F32)
    dt_ref[...] = dt
    cum_ref[...] = cum


def _dt_proj(h_bf, w_dt, dt_bias, a_log):
    return pl.pallas_call(
        _dt_kernel,
        grid=(TP // DT_ROWS,),
        in_specs=[
            pl.BlockSpec((DT_ROWS, D), lambda i: (i, 0)),
            pl.BlockSpec((D, LANES), lambda i: (0, 0)),
            pl.BlockSpec((1, LANES), lambda i: (0, 0)),
            pl.BlockSpec((1, LANES), lambda i: (0, 0)),
        ],
        out_specs=[
            pl.BlockSpec((DT_ROWS, LANES), lambda i: (i, 0)),
            pl.BlockSpec((DT_ROWS, LANES), lambda i: (i, 0)),
        ],
        out_shape=[jax.ShapeDtypeStruct((TP, LANES), F32), jax.ShapeDtypeStruct((TP, LANES), F32)],
        compiler_params=_cp(("arbitrary",), 16),
        name="dt_proj",
    )(h_bf, w_dt, dt_bias, a_log)


def _causal_conv(x, halo, w, k):
    acc = w[k - 1:k, :] * x
    for s in range(1, k):
        acc = acc + w[k - 1 - s:k - s, :] * pltpu.roll(x, s, 0)
    x8 = x[0:SUBLANES, :]
    h8 = halo[BF16_SUBLANES - SUBLANES:, :]
    row = lax.broadcasted_iota(jnp.int32, (SUBLANES, 1), 0)
    top = w[k - 1:k, :] * x8
    for s in range(1, k):
        shifted = jnp.where(row < s, pltpu.roll(h8, s, 0), pltpu.roll(x8, s, 0))
        top = top + w[k - 1 - s:k - s, :] * shifted
    return jnp.concatenate([top, acc[SUBLANES:, :]], axis=0)


def _mixa_kernel(c_ref, b_ref, v_ref, ch_ref, vh_ref, g_ref, cw_ref, wo_ref, bg_ref, o_ref):
    u = c_ref[...].astype(F32) * v_ref[...].astype(F32)
    hu = ch_ref[...].astype(F32) * vh_ref[...].astype(F32)
    conv = _causal_conv(u, hu, cw_ref[...], 3)
    lhs = (b_ref[...].astype(F32) * conv).astype(BF16)
    ya = jnp.dot(lhs, wo_ref[...], preferred_element_type=F32)
    ga = _sigmoid(g_ref[:, GATE_OFF:GATE_OFF + D].astype(F32) + bg_ref[...])
    o_ref[...] = (ga * ya).astype(BF16)


def _mixer_a(proj, gates, conv_w, w_out_a, bgate_a):
    tm = 512
    hb = tm // BF16_SUBLANES
    halo = lambda col: (lambda i: (jnp.maximum(i * hb - 1, 0), col))
    return pl.pallas_call(
        _mixa_kernel,
        grid=(TP // tm,),
        in_specs=[
            pl.BlockSpec((tm, D), lambda i: (i, C0 // D)),
            pl.BlockSpec((tm, D), lambda i: (i, B0 // D)),
            pl.BlockSpec((tm, D), lambda i: (i, V0 // D)),
            pl.BlockSpec((BF16_SUBLANES, D), halo(C0 // D)),
            pl.BlockSpec((BF16_SUBLANES, D), halo(V0 // D)),
            pl.BlockSpec((None, tm, GATE_W), lambda i: (0, i, 0)),
            pl.BlockSpec((3, D), lambda i: (0, 0)),
            pl.BlockSpec((D, D), lambda i: (0, 0), pipeline_mode=pl.Buffered(1)),
            pl.BlockSpec((1, D), lambda i: (0, 0)),
        ],
        out_specs=pl.BlockSpec((tm, D), lambda i: (i, 0)),
        out_shape=jax.ShapeDtypeStruct((TP, D), BF16),
        compiler_params=_cp(("arbitrary",), 56),
        name="mixer_a",
    )(proj, proj, proj, proj, proj, gates, conv_w, w_out_a, bgate_a)


def _ssd_kernel(xs_ref, bm_ref, cm_ref, z_ref, dt_ref, cum_ref,
                wx_ref, wb_ref, wc_ref, bx_ref, bb_ref, bc_ref, dsk_ref, nw_ref,
                o_ref, st_ref):
    g = pl.program_id(1)
    st_ref[...] = jnp.zeros_like(st_ref)

    rr = lax.broadcasted_iota(jnp.int32, (CH, CH), 0)
    cc = lax.broadcasted_iota(jnp.int32, (CH, CH), 1)
    shift_mats = [(rr - cc == s).astype(BF16) for s in range(1, SSD_CONV_K)]
    w_all = jnp.concatenate([wx_ref[...], wb_ref[...], wc_ref[...]], axis=1)
    b_all = jnp.concatenate([bx_ref[...], bb_ref[...], bc_ref[...]], axis=1)
    taps = [w_all[k:k + 1, :] for k in range(SSD_CONV_K)]
    lane_shift = (LANES - HPG * g) % LANES
    row = lax.broadcasted_iota(jnp.int32, (CH, 1), 0)
    row8 = lax.broadcasted_iota(jnp.int32, (SUBLANES, 1), 0)
    r2 = lax.broadcasted_iota(jnp.int32, (SUB, SUB), 0)
    c2 = lax.broadcasted_iota(jnp.int32, (SUB, SUB), 1)
    causal = r2 >= c2
    lane = lax.broadcasted_iota(jnp.int32, (1, LANES), 1)
    lo = lane < HEADDIM
    dskip = dsk_ref[...]
    norm_w = nw_ref[...]

    def chunk(c, carry):
        r0 = pl.multiple_of(c * CH, CH)
        h0 = pl.multiple_of(jnp.maximum(r0 - BF16_SUBLANES, 0), BF16_SUBLANES)
        rows = pl.ds(r0, CH)
        above = pl.ds(h0, BF16_SUBLANES)
        raw = jnp.concatenate([xs_ref[rows, :], bm_ref[rows, :], cm_ref[rows, :]], axis=1)
        halo = jnp.concatenate([xs_ref[above, :], bm_ref[above, :], cm_ref[above, :]], axis=1)
        h8 = halo[BF16_SUBLANES - SUBLANES:, :].astype(F32)

        acc = taps[SSD_CONV_K - 1] * raw.astype(F32) + b_all
        fix = jnp.zeros((SUBLANES, raw.shape[1]), F32)
        for s in range(1, SSD_CONV_K):
            tap = taps[SSD_CONV_K - 1 - s]
            acc = acc + tap * jnp.dot(shift_mats[s - 1], raw, preferred_element_type=F32)
            fix = fix + jnp.where(row8 < s, tap * pltpu.roll(h8, s, 0), 0.0)
        act = _silu(jnp.concatenate([acc[0:SUBLANES, :] + fix, acc[SUBLANES:, :]], axis=0))
        first = jnp.where(c == 0, PADF, 0)
        xs = jnp.where(row >= first, act[:, :GW], 0.0)
        bm = act[:, GW:GW + STATE]
        cm = act[:, GW + STATE:]

        dtg = pltpu.roll(dt_ref[rows, :], lane_shift, 1)
        cumg = pltpu.roll(cum_ref[rows, :], lane_shift, 1) * LOG2E
        bm_bf = bm.astype(BF16)
        cm_bf = cm.astype(BF16)
        xs_bf = xs.astype(BF16)

        st = st_ref[...]
        y_rows = []
        for q in range(CH // SUB):
            rs = slice(q * SUB, (q + 1) * SUB)
            cq = cumg[rs, :]
            if q > 0:
                cq = cq - cumg[q * SUB - 1:q * SUB, :]
            dq = dtg[rs, :]
            last = cq[SUB - 1:SUB, :]
            ecum = jnp.exp2(cq)
            wts = jnp.exp2(last - cq) * dq
            elast = jnp.exp2(last)
            src_t = (cq - jnp.log2(dq)).T

            cb = lax.dot_general(cm_bf[rs, :], bm_bf[rs, :], (((1,), (1,)), ((), ())),
                                 preferred_element_type=F32)
            cbm = jnp.where(causal, cb, 0.0)
            y_off = jnp.dot(cm_bf[rs, :], st.astype(BF16), preferred_element_type=F32)

            def head_mat(j):
                seg = cq[:, j:j + 1] - src_t[j:j + 1, :]
                return (cbm * jnp.exp2(jnp.minimum(seg, EXP2_CLAMP))).astype(BF16)

            ys, xws, els = [], [], []
            for p in range(GW // LANES):
                ja, jb = 2 * p, 2 * p + 1
                sl = slice(p * LANES, (p + 1) * LANES)
                xp = xs_bf[rs, sl]
                ya = jnp.dot(head_mat(ja), xp, preferred_element_type=F32)
                yb = jnp.dot(head_mat(jb), xp, preferred_element_type=F32)
                e_p = jnp.where(lo, ecum[:, ja:ja + 1], ecum[:, jb:jb + 1])
                ys.append(jnp.where(lo, ya, yb) + e_p * y_off[:, sl])
                w_p = jnp.where(lo, wts[:, ja:ja + 1], wts[:, jb:jb + 1])
                xws.append((xs[rs, sl] * w_p).astype(BF16))
                els.append(jnp.where(lo, elast[:, ja:ja + 1], elast[:, jb:jb + 1]))
            y_rows.append(jnp.concatenate(ys, axis=1))
            xw = jnp.concatenate(xws, axis=1)
            el = jnp.concatenate(els, axis=1)
            bm_t = bm[rs, :].T.astype(BF16)
            st = st * el + jnp.dot(bm_t, xw, preferred_element_type=F32)
        st_ref[...] = st
        y = jnp.concatenate(y_rows, axis=0)

        y = y + xs * dskip
        yg = y * _silu(z_ref[rows, :].astype(F32))
        ms = jnp.mean(yg * yg, axis=-1, keepdims=True)
        o_ref[rows, :] = (yg * lax.rsqrt(ms + RMS_EPS) * norm_w).astype(BF16)
        return carry

    lax.fori_loop(0, NCH, chunk, 0)


def _ssd(proj, dt, cum, conv_w, conv_b, dskip, norm_w):
    xcol = lambda g: X0 // GW + g
    bcol = lambda g: (X0 + D_INNER) // STATE + g
    ccol = lambda g: (X0 + D_INNER + GROUPS * STATE) // STATE + g
    zcol = lambda g: Z0 // GW + g
    wbcol = lambda g: D_INNER // STATE + g
    wccol = lambda g: (D_INNER + GROUPS * STATE) // STATE + g
    rows = lambda col_fn: (lambda b, g: (b, col_fn(g)))
    par = lambda col_fn: (lambda b, g: (0, col_fn(g)))
    return pl.pallas_call(
        _ssd_kernel,
        grid=(BATCH, GROUPS),
        in_specs=[
            pl.BlockSpec((LP, GW), rows(xcol)),
            pl.BlockSpec((LP, STATE), rows(bcol)),
            pl.BlockSpec((LP, STATE), rows(ccol)),
            pl.BlockSpec((LP, GW), rows(zcol)),
            pl.BlockSpec((LP, LANES), lambda b, g: (b, 0)),
            pl.BlockSpec((LP, LANES), lambda b, g: (b, 0)),
            pl.BlockSpec((SSD_CONV_K, GW), par(lambda g: g)),
            pl.BlockSpec((SSD_CONV_K, STATE), par(wbcol)),
            pl.BlockSpec((SSD_CONV_K, STATE), par(wccol)),
            pl.BlockSpec((1, GW), par(lambda g: g)),
            pl.BlockSpec((1, STATE), par(wbcol)),
            pl.BlockSpec((1, STATE), par(wccol)),
            pl.BlockSpec((1, GW), par(lambda g: g)),
            pl.BlockSpec((1, GW), par(lambda g: g)),
        ],
        out_specs=pl.BlockSpec((LP, GW), lambda b, g: (b, g)),
        out_shape=jax.ShapeDtypeStruct((TP, D_INNER), BF16),
        scratch_shapes=[pltpu.VMEM((STATE, GW), F32)],
        compiler_params=_cp(("arbitrary", "arbitrary"), 56),
        name="ssd",
    )(proj, proj, proj, proj, dt, cum,
      conv_w, conv_w, conv_w, conv_b, conv_b, conv_b, dskip, norm_w)


def _outb_kernel(y_ref, w_ref, g_ref, bg_ref, ya_ref, o_ref):
    for rs in _row_chunks(y_ref.shape[0]):
        yb = jnp.dot(y_ref[rs, :], w_ref[...], preferred_element_type=F32)
        gb = _sigmoid(g_ref[rs, GATE_OFF:GATE_OFF + D].astype(F32) + bg_ref[...])
        o_ref[rs, :] = (gb * yb + ya_ref[rs, :].astype(F32)).astype(BF16)


def _out_b(yn, w_out_b, gates, bgate_b, ga_ya):
    tm = 512
    return pl.pallas_call(
        _outb_kernel,
        grid=(TP // tm,),
        in_specs=[
            pl.BlockSpec((tm, D_INNER), lambda i: (i, 0)),
            pl.BlockSpec((D_INNER, D), lambda i: (0, 0), pipeline_mode=pl.Buffered(1)),
            pl.BlockSpec((None, tm, GATE_W), lambda i: (1, i, 0)),
            pl.BlockSpec((1, D), lambda i: (0, 0)),
            pl.BlockSpec((tm, D), lambda i: (i, 0)),
        ],
        out_specs=pl.BlockSpec((tm, D), lambda i: (i, 0)),
        out_shape=jax.ShapeDtypeStruct((TP, D), BF16),
        compiler_params=_cp(("arbitrary",), 56),
        name="out_b",
    )(yn, w_out_b, gates, bgate_b, ga_ya)


def _wo_kernel(m_ref, w_ref, h_ref, g_ref, b_ref, hf_ref, hb_ref):
    for rs in _row_chunks(m_ref.shape[0]):
        mix = jnp.dot(m_ref[rs, :], w_ref[...], preferred_element_type=F32)
        y = _layer_norm(ALPHA * h_ref[rs, :] + mix, g_ref[...], b_ref[...])
        hf_ref[rs, :] = y
        hb_ref[rs, :] = y.astype(BF16)


def _wo_ln(m, w_o, h, g, b):
    tm = 512
    return pl.pallas_call(
        _wo_kernel,
        grid=(TP // tm,),
        in_specs=[
            pl.BlockSpec((tm, D), lambda i: (i, 0)),
            pl.BlockSpec((D, D), lambda i: (0, 0), pipeline_mode=pl.Buffered(1)),
            pl.BlockSpec((tm, D), lambda i: (i, 0)),
            pl.BlockSpec((1, D), lambda i: (0, 0)),
            pl.BlockSpec((1, D), lambda i: (0, 0)),
        ],
        out_specs=[
            pl.BlockSpec((tm, D), lambda i: (i, 0)),
            pl.BlockSpec((tm, D), lambda i: (i, 0)),
        ],
        out_shape=[jax.ShapeDtypeStruct((TP, D), F32), jax.ShapeDtypeStruct((TP, D), BF16)],
        compiler_params=_cp(("arbitrary",), 56),
        name="wo_ln",
    )(m, w_o, h, g, b)


def _swiglu_step(x_bf, wg_ref, wu_ref, wd_ref):
    gate = jnp.dot(x_bf, wg_ref[...], preferred_element_type=F32)
    up = jnp.dot(x_bf, wu_ref[...], preferred_element_type=F32)
    mid = (_silu(gate) * up).astype(BF16)
    return jnp.dot(mid, wd_ref[...], preferred_element_type=F32)


def _ffn_gu_kernel(nblk, x_ref, wg_ref, wu_ref, *rest):
    if nblk is None:
        (o_ref,) = rest
    else:
        c_ref, o_ref, co_ref = rest
        _side_cast(nblk, pl.program_id(0) * pl.num_programs(1) + pl.program_id(1), c_ref, co_ref)
    for rs in _row_chunks(x_ref.shape[0]):
        x = x_ref[rs, :]
        gate = jnp.dot(x, wg_ref[...], preferred_element_type=F32)
        up = jnp.dot(x, wu_ref[...], preferred_element_type=F32)
        o_ref[rs, :] = (_silu(gate) * up).astype(BF16)


def _ffn_down_kernel(a_ref, w_ref, h_ref, g_ref, b_ref, hf_ref, hb_ref):
    valid = _valid_rows(pl.program_id(0), TM_FFN_DOWN)
    for rs in _row_chunks(TM_FFN_DOWN):
        f = jnp.dot(a_ref[rs, :], w_ref[...], preferred_element_type=F32)
        y = _layer_norm(ALPHA * h_ref[rs, :] + f, g_ref[...], b_ref[...])
        y = jnp.where(valid[rs, :], y, 0.0)
        hf_ref[rs, :] = y
        hb_ref[rs, :] = y.astype(BF16)


def _ffn_dense(h_bf, h_f, wg, wu, wd, g, b, side=None):
    tm, tn = TM_FFN, TN_FFN
    grid = (FF_DENSE // tn, TP // tm)
    in_specs = [
        pl.BlockSpec((tm, D), lambda j, i: (i, 0)),
        pl.BlockSpec((D, tn), lambda j, i: (0, j), pipeline_mode=pl.Buffered(1)),
        pl.BlockSpec((D, tn), lambda j, i: (0, j), pipeline_mode=pl.Buffered(1)),
    ]
    out_specs = [pl.BlockSpec((tm, tn), lambda j, i: (i, j))]
    out_shape = [jax.ShapeDtypeStruct((TP, FF_DENSE), BF16)]
    args = [h_bf, wg, wu]
    nblk = None
    if side is not None:
        nblk, spec, shape = _side_cast_specs(side, lambda j, i: j * grid[1] + i)
        in_specs.append(spec)
        out_specs.append(spec)
        out_shape.append(shape)
        args.append(side[0])
    res = pl.pallas_call(
        functools.partial(_ffn_gu_kernel, nblk),
        grid=grid,
        in_specs=in_specs,
        out_specs=out_specs,
        out_shape=out_shape,
        compiler_params=_cp(("arbitrary", "arbitrary"), 56),
        name="ffn_gate_up",
    )(*args)
    mid, side_out = (res[0], res[1]) if side is not None else (res[0], None)

    tmd = TM_FFN_DOWN
    h_f, h_bf = pl.pallas_call(
        _ffn_down_kernel,
        grid=(TP // tmd,),
        in_specs=[
            pl.BlockSpec((tmd, FF_DENSE), lambda i: (i, 0)),
            pl.BlockSpec((FF_DENSE, D), lambda i: (0, 0), pipeline_mode=pl.Buffered(1)),
            pl.BlockSpec((tmd, D), lambda i: (i, 0)),
            pl.BlockSpec((1, D), lambda i: (0, 0)),
            pl.BlockSpec((1, D), lambda i: (0, 0)),
        ],
        out_specs=[pl.BlockSpec((tmd, D), lambda i: (i, 0)), pl.BlockSpec((tmd, D), lambda i: (i, 0))],
        out_shape=[jax.ShapeDtypeStruct((TP, D), F32), jax.ShapeDtypeStruct((TP, D), BF16)],
        compiler_params=_cp(("arbitrary",), 56),
        name="ffn_down",
    )(mid, wd, h_f, g, b)
    return h_f, h_bf, side_out


def _router_kernel(h_ref, r_ref, meta_ref, cnt_ref, carry_ref):
    i = pl.program_id(0)
    tm = TOK_TILE

    @pl.when(i == 0)
    def _():
        carry_ref[...] = jnp.zeros_like(carry_ref)

    logits = jnp.dot(h_ref[...], r_ref[...], precision=lax.Precision.HIGHEST,
                     preferred_element_type=F32)
    lane = lax.broadcasted_iota(jnp.int32, (tm, LANES), 1).astype(F32)
    neg = jnp.float32(-jnp.inf)
    logits = jnp.where(lane < N_EXPERTS, logits, neg)
    m1 = jnp.max(logits, axis=-1, keepdims=True)
    i1 = jnp.min(jnp.where(logits == m1, lane, float(LANES)), axis=-1, keepdims=True)
    rest = jnp.where(lane == i1, neg, logits)
    m2 = jnp.max(rest, axis=-1, keepdims=True)
    i2 = jnp.min(jnp.where(rest == m2, lane, float(LANES)), axis=-1, keepdims=True)
    e = jnp.exp(m2 - m1)
    w1 = 1.0 / (1.0 + e)
    w2 = e / (1.0 + e)

    valid = _valid_rows(i, tm)
    sel = jnp.logical_and(jnp.logical_or(lane == i1, lane == i2), valid)
    onehot = sel.astype(F32)
    r = lax.broadcasted_iota(jnp.int32, (tm, tm), 0)
    c = lax.broadcasted_iota(jnp.int32, (tm, tm), 1)
    strict = (r > c).astype(BF16)
    before = jnp.dot(strict, onehot.astype(BF16), preferred_element_type=F32) + carry_ref[0:1, :]
    rank1 = jnp.sum(jnp.where(lane == i1, before, 0.0), axis=-1, keepdims=True)
    rank2 = jnp.sum(jnp.where(lane == i2, before, 0.0), axis=-1, keepdims=True)
    total = carry_ref[0:1, :] + jnp.sum(onehot, axis=0, keepdims=True)
    carry_ref[...] = jnp.broadcast_to(total, carry_ref.shape)
    cnt_ref[...] = jnp.broadcast_to(total, cnt_ref.shape)

    meta = jnp.where(lane == 0, i1, 0.0)
    meta = jnp.where(lane == 1, i2, meta)
    meta = jnp.where(lane == 2, w1, meta)
    meta = jnp.where(lane == 3, w2, meta)
    meta = jnp.where(lane == 4, rank1, meta)
    meta = jnp.where(lane == 5, rank2, meta)
    meta_ref[...] = meta


def _router(h_f, router_w):
    tm = TOK_TILE
    return pl.pallas_call(
        _router_kernel,
        grid=(TP // tm,),
        in_specs=[
            pl.BlockSpec((tm, D), lambda i: (i, 0)),
            pl.BlockSpec((D, LANES), lambda i: (0, 0)),
        ],
        out_specs=[
            pl.BlockSpec((tm, LANES), lambda i: (i, 0)),
            pl.BlockSpec((8, LANES), lambda i: (0, 0)),
        ],
        out_shape=[jax.ShapeDtypeStruct((TP, LANES), F32), jax.ShapeDtypeStruct((8, LANES), F32)],
        scratch_shapes=[pltpu.VMEM((8, LANES), F32)],
        compiler_params=_cp(("arbitrary",), 32),
        name="router",
    )(h_f, router_w)


def _row_dma_ops(copy, count):
    def start(p, carry):
        for k in range(DMA_PRIORITIES):
            copy(DMA_PRIORITIES * p + k).start(priority=k)
        return carry

    def wait(t, carry):
        copy(t).wait()
        return carry

    return (lambda: lax.fori_loop(0, count // DMA_PRIORITIES, start, 0, unroll=4),
            lambda: lax.fori_loop(0, count, wait, 0, unroll=8))


def _gather_kernel(nu_ref, cur_ref, nxt_ref, h_ref, o_ref, buf, sem):
    i = pl.program_id(0)
    nu = nu_ref[0]
    slot = i % 2

    def copies(idx_ref, s):
        return _row_dma_ops(
            lambda t: pltpu.make_async_copy(h_ref.at[pl.ds(idx_ref[0, 0, t], 1)],
                                            buf.at[s, pl.ds(t, 1)], sem.at[s]), TM_MOE)

    @pl.when(jnp.logical_and(i == 0, nu > 0))
    def _():
        copies(cur_ref, 0)[0]()

    @pl.when(i + 1 < nu)
    def _():
        copies(nxt_ref, 1 - slot)[0]()

    @pl.when(i < nu)
    def _():
        copies(cur_ref, slot)[1]()
        o_ref[...] = buf[slot].astype(BF16)

    @pl.when(i >= nu)
    def _():
        o_ref[...] = jnp.zeros_like(o_ref)


def _gather_rows(n_used, src, h_f):
    last = N_MOE_TILES - 1
    grid_spec = pltpu.PrefetchScalarGridSpec(
        num_scalar_prefetch=1,
        grid=(N_MOE_TILES,),
        in_specs=[
            pl.BlockSpec((1, 1, TM_MOE), lambda i, nu: (i, 0, 0), memory_space=pltpu.SMEM),
            pl.BlockSpec((1, 1, TM_MOE), lambda i, nu: (jnp.minimum(i + 1, last), 0, 0),
                         memory_space=pltpu.SMEM),
            pl.BlockSpec(memory_space=pl.ANY),
        ],
        out_specs=pl.BlockSpec((TM_MOE, D), lambda i, nu: (i, 0)),
        scratch_shapes=[pltpu.VMEM((2, TM_MOE, D), F32), pltpu.SemaphoreType.DMA((2,))],
    )
    return pl.pallas_call(
        _gather_kernel,
        grid_spec=grid_spec,
        out_shape=jax.ShapeDtypeStruct((P_MOE, D), BF16),
        compiler_params=_cp(("arbitrary",), 32),
        name="moe_gather",
    )(n_used, src, src, h_f)


def _ffn_moe_kernel(te_ref, nu_ref, x_ref, wg_ref, wu_ref, wd_ref, o_ref, acc_ref):
    del te_ref
    i = pl.program_id(0)
    f = pl.program_id(1)
    used = i < nu_ref[0]

    @pl.when(jnp.logical_and(used, f == 0))
    def _():
        acc_ref[...] = jnp.zeros_like(acc_ref)

    @pl.when(used)
    def _():
        acc_ref[...] += _swiglu_step(x_ref[...], wg_ref, wu_ref, wd_ref)

    @pl.when(jnp.logical_and(used, f == pl.num_programs(1) - 1))
    def _():
        o_ref[...] = acc_ref[...]

    @pl.when(jnp.logical_and(jnp.logical_not(used), f == 0))
    def _():
        o_ref[...] = jnp.zeros_like(o_ref)


def _ffn_moe(tile_e, n_used, xs, wg, wu, wd):
    tm, tf = TM_MOE, TF_MOE
    nf = FF_EXPERT // tf

    def fblk(i, f, nu):
        return jnp.where(i < nu[0], f, nf - 1)

    grid_spec = pltpu.PrefetchScalarGridSpec(
        num_scalar_prefetch=2,
        grid=(N_MOE_TILES, nf),
        in_specs=[
            pl.BlockSpec((tm, D), lambda i, f, te, nu: (i, 0)),
            pl.BlockSpec((None, D, tf), lambda i, f, te, nu: (te[i], 0, fblk(i, f, nu))),
            pl.BlockSpec((None, D, tf), lambda i, f, te, nu: (te[i], 0, fblk(i, f, nu))),
            pl.BlockSpec((None, tf, D), lambda i, f, te, nu: (te[i], fblk(i, f, nu), 0)),
        ],
        out_specs=pl.BlockSpec((tm, D), lambda i, f, te, nu: (i, 0)),
        scratch_shapes=[pltpu.VMEM((tm, D), F32)],
    )
    return pl.pallas_call(
        _ffn_moe_kernel,
        grid_spec=grid_spec,
        out_shape=jax.ShapeDtypeStruct((P_MOE, D), F32),
        compiler_params=_cp(("arbitrary", "arbitrary"), 58),
        name="ffn_moe",
    )(tile_e, n_used, xs, wg, wu, wd)


def _combine_kernel(final, cur_ref, nxt_ref, y_ref, meta_ref, h_ref, g_ref, b_ref, *rest):
    if final:
        out_ref, buf, sem = rest
    else:
        hf_ref, hb_ref, buf, sem = rest
    i = pl.program_id(0)
    tm = COMBINE_TILE
    slot = i % 2

    def copies(idx_ref, s):
        def copy(t, k):
            return pltpu.make_async_copy(y_ref.at[pl.ds(idx_ref[0, 0, TOP_K * t + k], 1)],
                                         buf.at[s, k, pl.ds(t, 1)], sem.at[s])

        def start(t, carry):
            for k in range(TOP_K):
                copy(t, k).start(priority=k % DMA_PRIORITIES)
            return carry

        def wait(t, carry):
            for k in range(TOP_K):
                copy(t, k).wait()
            return carry

        return (lambda: lax.fori_loop(0, tm, start, 0, unroll=8),
                lambda: lax.fori_loop(0, tm, wait, 0, unroll=8))

    @pl.when(i == 0)
    def _():
        copies(cur_ref, 0)[0]()

    @pl.when(i + 1 < pl.num_programs(0))
    def _():
        copies(nxt_ref, 1 - slot)[0]()

    copies(cur_ref, slot)[1]()

    meta = meta_ref[...]
    w1 = meta[:, 2:3]
    w2 = meta[:, 3:4]
    f = w1 * buf[slot, 0] + w2 * buf[slot, 1]
    y = _layer_norm(ALPHA * h_ref[...] + f, g_ref[...], b_ref[...])
    if final:
        out_ref[0] = y
    else:
        y = jnp.where(_valid_rows(i, tm), y, 0.0)
        hf_ref[...] = y
        hb_ref[...] = y.astype(BF16)


def _combine(pos, y, meta, h_f, g, b, final):
    tm = COMBINE_TILE
    if final:
        out_specs = pl.BlockSpec((1, tm, D), lambda i: (i // NCH, jnp.maximum(i % NCH - 1, 0), 0))
        out_shape = jax.ShapeDtypeStruct((BATCH, SEQ, D), F32)
    else:
        out_specs = [pl.BlockSpec((tm, D), lambda i: (i, 0)), pl.BlockSpec((tm, D), lambda i: (i, 0))]
        out_shape = [jax.ShapeDtypeStruct((TP, D), F32), jax.ShapeDtypeStruct((TP, D), BF16)]
    last = TP // tm - 1
    return pl.pallas_call(
        functools.partial(_combine_kernel, final),
        grid=(TP // tm,),
        in_specs=[
            pl.BlockSpec((1, 1, 2 * tm), lambda i: (i, 0, 0), memory_space=pltpu.SMEM),
            pl.BlockSpec((1, 1, 2 * tm), lambda i: (jnp.minimum(i + 1, last), 0, 0), memory_space=pltpu.SMEM),
            pl.BlockSpec(memory_space=pl.ANY),
            pl.BlockSpec((tm, LANES), lambda i: (i, 0)),
            pl.BlockSpec((tm, D), lambda i: (i, 0)),
            pl.BlockSpec((1, D), lambda i: (0, 0)),
            pl.BlockSpec((1, D), lambda i: (0, 0)),
        ],
        out_specs=out_specs,
        out_shape=out_shape,
        scratch_shapes=[pltpu.VMEM((2, TOP_K, tm, D), F32), pltpu.SemaphoreType.DMA((2,))],
        compiler_params=_cp(("arbitrary",), 32),
        name="moe_combine",
    )(pos, pos, y, meta, h_f, g, b)


def _pad_lanes(v, width=LANES):
    return jnp.pad(v, ((0, 0), (0, width - v.shape[1])))


def _token_mixer(h_f, h_bf, w_in, b_gate, conv_a_w, w_out_a, ssd_conv_w, ssd_conv_b, dt_bias, a_log,
                 d_skip, ssd_norm_w, w_out_b, w_o, ln_g, ln_b, side=None):
    w_bf = w_in.astype(BF16)
    lane = jnp.arange(LANES)[None, :]
    w_dt = jnp.where(lane < HEADS, w_in[:, G0:G0 + LANES], 0.0).astype(BF16)
    win_a = w_bf[:, G0:G0 + GATE_W]
    win_b = w_bf[:, G0 + D:]
    win_b = jnp.pad(win_b, ((0, 0), (0, GATE_W - win_b.shape[1])))
    w_gates = jnp.stack([win_a, win_b])
    proj, side_out = _in_proj(h_bf, w_bf, side)
    gates = _gate_proj(h_bf, w_gates)
    dt, cum = _dt_proj(h_bf, w_dt, _pad_lanes(dt_bias[None, :]), _pad_lanes(a_log[None, :]))
    ga_ya = _mixer_a(proj, gates, conv_a_w, w_out_a.astype(BF16), b_gate[None, :D])
    dskip = jnp.repeat(d_skip, HEADDIM)[None, :]
    yn = _ssd(proj, dt, cum, ssd_conv_w, ssd_conv_b[None, :], dskip, ssd_norm_w[None, :])
    m = _out_b(yn, w_out_b.astype(BF16), gates, b_gate[None, D:], ga_ya)
    h_f, h_bf = _wo_ln(m, w_o.astype(BF16), h_f, ln_g[None, :], ln_b[None, :])
    return h_f, h_bf, side_out


def _moe_layer(h_f, router_w, wg, wu, wd, ln_g, ln_b, final):
    meta, cnt = _router(h_f, _pad_lanes(router_w))
    i1 = meta[:, 0].astype(jnp.int32)
    i2 = meta[:, 1].astype(jnp.int32)
    counts = cnt[0, :N_EXPERTS].astype(jnp.int32)
    padded = ((counts + TM_MOE - 1) // TM_MOE) * TM_MOE
    ends = jnp.cumsum(padded)
    starts = ends - padded
    pos = jnp.stack([starts[i1] + meta[:, 4].astype(jnp.int32),
                     starts[i2] + meta[:, 5].astype(jnp.int32)], axis=1)
    pos = jnp.clip(pos, 0, P_MOE - 1)
    tok = jnp.arange(TP, dtype=jnp.int32)
    valid = (tok % LP) >= PADF
    dst = jnp.where(valid[:, None], pos, P_MOE).reshape(-1)
    src = jnp.zeros((P_MOE,), jnp.int32).at[dst].set(jnp.repeat(tok, 2), mode="drop")
    src = src.reshape(N_MOE_TILES, 1, TM_MOE)
    pos = pos.reshape(TP // COMBINE_TILE, 1, TOP_K * COMBINE_TILE)
    n_used = (ends[-1] // TM_MOE).reshape(1).astype(jnp.int32)
    tile_start = jnp.arange(N_MOE_TILES, dtype=jnp.int32) * TM_MOE
    tile_e = jnp.sum((tile_start[:, None] >= ends[None, :]).astype(jnp.int32), axis=1)
    last_e = jnp.minimum(tile_e[jnp.maximum(n_used[0] - 1, 0)], N_EXPERTS - 1)
    tile_e = jnp.where(tile_start < ends[-1], jnp.minimum(tile_e, N_EXPERTS - 1), last_e)
    xs = _gather_rows(n_used, src, h_f)
    y = _ffn_moe(tile_e, n_used, xs, wg, wu, wd)
    return _combine(pos, y, meta, h_f, ln_g[None, :], ln_b[None, :], final)


def kernel(x, meta_tokens, ln_in_g, ln_in_b, w_in, b_gate, conv_a_w, w_out_a, ssd_conv_w, ssd_conv_b,
           dt_bias, a_log, d_skip, ssd_norm_w, w_out_b, w_o, ln1_g, ln1_b, ffn_w_gate, ffn_w_up,
           ffn_w_down, router, moe_w_gate, moe_w_up, moe_w_down, ln2_g, ln2_b):
    h_f, h_bf = _ln_in(x, meta_tokens.astype(x.dtype), ln_in_g[None, :], ln_in_b[None, :])

    pending = [((i, k), w[i // 2]) for i in range(DEPTH) if i % 2 == 1
               for k, w in enumerate((moe_w_gate, moe_w_up, moe_w_down))]
    cast = {}

    def next_side(layer, steps):
        while pending and pending[0][0][0] < layer:
            pending.pop(0)
        if not pending:
            return None, None
        key, w = pending.pop(0)
        w2d = w.reshape(-1, w.shape[-1])
        return key, (w2d, _side_block_rows(w2d.shape[0], steps))

    for i in range(DEPTH):
        key, side = next_side(i, IN_PROJ_STEPS)
        h_f, h_bf, side_out = _token_mixer(h_f, h_bf, w_in[i], b_gate[i], conv_a_w[i], w_out_a[i],
                                           ssd_conv_w[i], ssd_conv_b[i], dt_bias[i], a_log[i], d_skip[i],
                                           ssd_norm_w[i], w_out_b[i], w_o[i], ln1_g[i], ln1_b[i], side)
        if key is not None:
            cast[key] = side_out
        j = i // 2
        if i % 2 == 0:
            key, side = next_side(i + 1, FFN_DENSE_STEPS)
            h_f, h_bf, side_out = _ffn_dense(h_bf, h_f, ffn_w_gate[j].astype(BF16), ffn_w_up[j].astype(BF16),
                                             ffn_w_down[j].astype(BF16), ln2_g[i][None, :], ln2_b[i][None, :],
                                             side)
            if key is not None:
                cast[key] = side_out
        else:
            final = i == DEPTH - 1
            wts = [cast[(i, k)].reshape(w[j].shape) if (i, k) in cast else w[j].astype(BF16)
                   for k, w in enumerate((moe_w_gate, moe_w_up, moe_w_down))]
            res = _moe_layer(h_f, router[j], wts[0], wts[1], wts[2], ln2_g[i], ln2_b[i], final)
            if final:
                return res
            h_f, h_bf = res
    return h_f.reshape(BATCH, LP, D)[:, PADF + N_META:, :]
```
